```python
import math
import jax, jax.numpy as jnp
from jax import lax
import numpy as np


D_MODEL = 1024
BATCH = 16
SEQ = 2048
DEPTH = 1
DEC_BATCH = 1
DEC_SEQ = 16384
PAST_LEN = 128

GRID_W = 64
EPS = 1e-6
ATT_HEADS = 8
ATT_KV_HEADS = 2
ATT_HEAD_DIM = 64
ATT_GROUP = ATT_HEADS // ATT_KV_HEADS
AXIS_DIM = ATT_HEAD_DIM // 2
ROPE_THETA = 10000.0
Q_BLOCK = 128
DN_HEADS = 8
DN_KEY_DIM = 64
DN_VAL_DIM = 64
DN_CONV = 5
DN_CHUNK = 64
N_GROUPS = 4
EXPERTS_PER_GROUP = 8
N_EXPERTS = N_GROUPS * EXPERTS_PER_GROUP
TOP_K_IN_GROUP = 2
EXPERT_FF = 256

ATT_Q_W = ATT_HEADS * ATT_HEAD_DIM
ATT_KV_W = ATT_KV_HEADS * ATT_HEAD_DIM
DN_QK_W = DN_HEADS * DN_KEY_DIM
DN_V_W = DN_HEADS * DN_VAL_DIM
IN_SPLITS = (ATT_Q_W, ATT_KV_W, ATT_KV_W, DN_QK_W, DN_QK_W, DN_V_W, DN_V_W, 2 * DN_HEADS, 2 * DN_HEADS, D_MODEL, D_MODEL)
IN_COLS = sum(IN_SPLITS)

kernel_name = 'hybrid_gqa_deltanet_hmoe_encoder'


def split_points():
    pts, acc = [], 0
    for w in IN_SPLITS[:-1]:
        acc += w
        pts.append(acc)
    return pts


def rmsnorm(x, w):
    xf = x.astype(jnp.float32)
    y = xf * lax.rsqrt(jnp.mean(xf * xf, axis=-1, keepdims=True) + EPS)
    return (y * w.astype(jnp.float32)).astype(x.dtype)


def l2norm(x):
    return x * lax.rsqrt(jnp.sum(x * x, axis=-1, keepdims=True) + EPS)


def axial_rope_tables(T):
    rows = T // GRID_W
    r = jnp.repeat(jnp.arange(rows, dtype=jnp.float32), GRID_W)
    c = jnp.tile(jnp.arange(GRID_W, dtype=jnp.float32), rows)
    inv = ROPE_THETA ** (-jnp.arange(0, AXIS_DIM, 2, dtype=jnp.float32) / AXIS_DIM)
    ang = jnp.concatenate([r[:, None] * inv, c[:, None] * inv], axis=-1)
    return jnp.cos(ang), jnp.sin(ang)


def apply_rope(x, cos, sin):
    xf = x.astype(jnp.float32).reshape(x.shape[:-1] + (ATT_HEAD_DIM // 2, 2))
    x1, x2 = xf[..., 0], xf[..., 1]
    c = cos[None, :, None, :]
    s = sin[None, :, None, :]
    out = jnp.stack([x1 * c - x2 * s, x1 * s + x2 * c], axis=-1).reshape(x.shape)
    return out.astype(x.dtype)


def block_attention(q, k, v):
    B, T, _, _ = q.shape
    nb = T // Q_BLOCK
    q = q * (ATT_HEAD_DIM ** -0.5)
    qg = q.reshape(B, nb, Q_BLOCK, ATT_KV_HEADS, ATT_GROUP, ATT_HEAD_DIM).transpose(1, 0, 3, 4, 2, 5)
    kt = k.transpose(0, 2, 1, 3)
    vt = v.transpose(0, 2, 1, 3)

    def one_block(qb):
        s = jnp.einsum('bgrqd,bgkd->bgrqk', qb, kt).astype(jnp.float32)
        p = jax.nn.softmax(s, axis=-1).astype(vt.dtype)
        return jnp.einsum('bgrqk,bgkd->bgrqd', p, vt)

    o = lax.map(one_block, qg)
    return o.transpose(1, 0, 4, 2, 3, 5).reshape(B, T, ATT_Q_W)


def centred_conv(x, w):
    C = x.shape[-1]
    pad = DN_CONV // 2
    return lax.conv_general_dilated(x, w.astype(x.dtype)[:, None, :], (1,), [(pad, pad)],
                                    dimension_numbers=('NWC', 'WIO', 'NWC'), feature_group_count=C)


def gated_delta_chunked(q, k, v, g, beta):
    B, H, T, DK = q.shape
    DV = v.shape[-1]
    C = DN_CHUNK
    N = T // C
    q = (q * DK ** -0.5).reshape(B, H, N, C, DK)
    k = k.reshape(B, H, N, C, DK)
    v = v.reshape(B, H, N, C, DV)
    beta = beta.reshape(B, H, N, C, 1)
    gc = jnp.cumsum(g.reshape(B, H, N, C), axis=-1)
    incl = jnp.tril(jnp.ones((C, C), dtype=bool))
    strict = jnp.tril(jnp.ones((C, C), dtype=bool), -1)
    decay = jnp.exp(jnp.where(incl, gc[..., :, None] - gc[..., None, :], -jnp.inf))
    kb = k * beta
    a_low = jnp.where(strict, jnp.einsum('bhnck,bhnsk->bhncs', kb, k) * decay, 0.0)
    lhs = a_low + jnp.eye(C, dtype=q.dtype)
    rhs = jnp.concatenate([v * beta, kb * jnp.exp(gc)[..., None]], axis=-1)
    sol = lax.linalg.triangular_solve(lhs, rhs, left_side=True, lower=True, unit_diagonal=True)
    u, w = sol[..., :DV], sol[..., DV:]
    qk = jnp.einsum('bhnck,bhnsk->bhncs', q, k) * decay
    q_dec = q * jnp.exp(gc)[..., None]
    k_dec = k * jnp.exp(gc[..., -1:] - gc)[..., None]
    g_tot = jnp.exp(gc[..., -1])

    def step(S, xs):
        u_i, w_i, qd_i, kd_i, qk_i, gt_i = xs
        v_new = u_i - jnp.einsum('bhck,bhkv->bhcv', w_i, S)
        o = jnp.einsum('bhck,bhkv->bhcv', qd_i, S) + jnp.einsum('bhcs,bhsv->bhcv', qk_i, v_new)
        S = S * gt_i[..., None, None] + jnp.einsum('bhck,bhcv->bhkv', kd_i, v_new)
        return S, o

    xs = tuple(jnp.moveaxis(a, 2, 0) for a in (u, w, q_dec, k_dec, qk, g_tot))
    S0 = jnp.zeros((B, H, DK, DV), q.dtype)
    _, o = lax.scan(step, S0, xs)
    return jnp.moveaxis(o, 0, 2).reshape(B, H, T, DV)


def mixer(xn, w_in, att_q_norm, att_k_norm, dn_conv_w, dn_a_log, dn_dt_bias, dn_out_norm,
          w_branch_att, w_branch_dn, w_out):
    B, T, _ = xn.shape
    h = jnp.einsum('btd,dc->btc', xn, w_in)
    aq, ak, av, dq, dk, dv, dz, da, db, ga, gb = jnp.split(h, split_points(), axis=-1)

    q = rmsnorm(aq.reshape(B, T, ATT_HEADS, ATT_HEAD_DIM), att_q_norm)
    k = rmsnorm(ak.reshape(B, T, ATT_KV_HEADS, ATT_HEAD_DIM), att_k_norm)
    v = av.reshape(B, T, ATT_KV_HEADS, ATT_HEAD_DIM)
    cos, sin = axial_rope_tables(T)
    q = apply_rope(q, cos, sin)
    k = apply_rope(k, cos, sin)
    att = block_attention(q, k, v)

    qkv = jax.nn.silu(centred_conv(jnp.concatenate([dq, dk, dv], axis=-1), dn_conv_w))
    cq, ck, cv = jnp.split(qkv, [DN_QK_W, 2 * DN_QK_W], axis=-1)
    def to_heads(t, d):
        return t.astype(jnp.float32).reshape(B, T, DN_HEADS, d).transpose(0, 2, 1, 3)
    qh = l2norm(to_heads(cq, DN_KEY_DIM))
    kh = l2norm(to_heads(ck, DN_KEY_DIM))
    vh = to_heads(cv, DN_VAL_DIM)
    a = da.astype(jnp.float32).reshape(B, T, 2, DN_HEADS)
    b = db.astype(jnp.float32).reshape(B, T, 2, DN_HEADS)
    g = -jnp.exp(dn_a_log.astype(jnp.float32)) * jax.nn.softplus(a + dn_dt_bias.astype(jnp.float32))
    g = g.transpose(2, 0, 3, 1)
    beta = jax.nn.sigmoid(b).transpose(2, 0, 3, 1)
    o_f = gated_delta_chunked(qh, kh, vh, g[0], beta[0])
    flip = lambda t: jnp.flip(t, axis=2)
    o_b = flip(gated_delta_chunked(flip(qh), flip(kh), flip(vh), flip(g[1]), flip(beta[1])))
    o = (o_f + o_b).transpose(0, 2, 1, 3)
    o = rmsnorm(o, dn_out_norm) * jax.nn.silu(dz.astype(jnp.float32).reshape(B, T, DN_HEADS, DN_VAL_DIM))
    dn = o.reshape(B, T, DN_V_W).astype(xn.dtype)

    ya = jnp.einsum('btc,cd->btd', att, w_branch_att)
    yb = jnp.einsum('btc,cd->btd', dn, w_branch_dn)
    m = jax.nn.sigmoid(ga) * ya + jax.nn.sigmoid(gb) * yb
    return jnp.einsum('btd,de->bte', m, w_out)


def hier_moe(x, group_router, expert_router, w_gate, w_up, w_down):
    Ntok = x.shape[0]
    gl = jnp.einsum('nd,dg->ng', x, group_router).astype(jnp.float32)
    gp = jax.nn.softmax(gl, axis=-1)
    gval, gidx = lax.top_k(gp, 1)
    el = jnp.einsum('nd,de->ne', x, expert_router).astype(jnp.float32).reshape(Ntok, N_GROUPS, EXPERTS_PER_GROUP)
    el_sel = jnp.einsum('ng,nge->ne', jax.nn.one_hot(gidx[:, 0], N_GROUPS, dtype=jnp.float32), el)
    ep = jax.nn.softmax(el_sel, axis=-1)
    tv, ti = lax.top_k(ep, TOP_K_IN_GROUP)
    wts = tv / jnp.sum(tv, axis=-1, keepdims=True) * gval
    eidx = gidx * EXPERTS_PER_GROUP + ti
    gates = jnp.einsum('nk,nke->ne', wts, jax.nn.one_hot(eidx, N_EXPERTS, dtype=jnp.float32)).astype(x.dtype)
    y = jnp.zeros_like(x)
    for e in range(N_EXPERTS):
        hdn = jax.nn.silu(x @ w_gate[e]) * (x @ w_up[e])
        y = y + gates[:, e:e + 1] * (hdn @ w_down[e])
    return y


def trunk(x, norm1_w, w_in, att_q_norm, att_k_norm, dn_conv_w, dn_a_log, dn_dt_bias, dn_out_norm,
          w_branch_att, w_branch_dn, w_out, norm2_w, moe_group_router, moe_expert_router,
          moe_w_gate, moe_w_up, moe_w_down, final_norm_w):
    B, T, D = x.shape
    for l in range(DEPTH):
        x = x + mixer(rmsnorm(x, norm1_w[l]), w_in[l], att_q_norm[l], att_k_norm[l], dn_conv_w[l],
                      dn_a_log[l], dn_dt_bias[l], dn_out_norm[l], w_branch_att[l], w_branch_dn[l], w_out[l])
        xn = rmsnorm(x, norm2_w[l]).reshape(B * T, D)
        x = x + hier_moe(xn, moe_group_router[l], moe_expert_router[l], moe_w_gate[l], moe_w_up[l],
                         moe_w_down[l]).reshape(B, T, D)
    return rmsnorm(x, final_norm_w)


def setup_inputs(seed: int = 0) -> dict:
    key = jax.random.key(seed)
    ks = jax.random.split(key, 24)
    f32 = jnp.float32
    def nrm(k, shape, scale):
        return jax.random.normal(k, shape, f32) * scale
    def gain(k, shape):
        return 1.0 + 0.02 * jax.random.normal(k, shape, f32)
    dt = jnp.exp(jax.random.uniform(ks[9], (DEPTH, 2, DN_HEADS), f32, math.log(1e-3), math.log(1e-1)))
    return {
        'x_prompt': jax.random.normal(ks[0], (BATCH, SEQ, D_MODEL), f32),
        'x_sample': jax.random.normal(ks[1], (DEC_BATCH, DEC_SEQ, D_MODEL), f32),
        'norm1_w': gain(ks[2], (DEPTH, D_MODEL)),
        'w_in': nrm(ks[3], (DEPTH, D_MODEL, IN_COLS), D_MODEL ** -0.5),
        'att_q_norm': gain(ks[4], (DEPTH, ATT_HEAD_DIM)),
        'att_k_norm': gain(ks[5], (DEPTH, ATT_HEAD_DIM)),
        'dn_conv_w': nrm(ks[6], (DEPTH, DN_CONV, 2 * DN_QK_W + DN_V_W), DN_CONV ** -0.5),
        'dn_a_log': jnp.log(jax.random.uniform(ks[7], (DEPTH, 2, DN_HEADS), f32, 1.0, 16.0)),
        'dn_dt_bias': dt + jnp.log(-jnp.expm1(-dt)),
        'dn_out_norm': gain(ks[8], (DEPTH, DN_VAL_DIM)),
        'w_branch_att': nrm(ks[10], (DEPTH, ATT_Q_W, D_MODEL), ATT_Q_W ** -0.5),
        'w_branch_dn': nrm(ks[11], (DEPTH, DN_V_W, D_MODEL), DN_V_W ** -0.5),
        'w_out': nrm(ks[12], (DEPTH, D_MODEL, D_MODEL), D_MODEL ** -0.5),
        'norm2_w': gain(ks[13], (DEPTH, D_MODEL)),
        'moe_group_router': nrm(ks[14], (DEPTH, D_MODEL, N_GROUPS), D_MODEL ** -0.5),
        'moe_expert_router': nrm(ks[15], (DEPTH, D_MODEL, N_EXPERTS), D_MODEL ** -0.5),
        'moe_w_gate': nrm(ks[16], (DEPTH, N_EXPERTS, D_MODEL, EXPERT_FF), D_MODEL ** -0.5),
        'moe_w_up': nrm(ks[17], (DEPTH, N_EXPERTS, D_MODEL, EXPERT_FF), D_MODEL ** -0.5),
        'moe_w_down': nrm(ks[18], (DEPTH, N_EXPERTS, EXPERT_FF, D_MODEL), EXPERT_FF ** -0.5),
        'final_norm_w': gain(ks[19], (D_MODEL,)),
    }


def reference(x_prompt, x_sample, norm1_w, w_in, att_q_norm, att_k_norm, dn_conv_w, dn_a_log, dn_dt_bias,
              dn_out_norm, w_branch_att, w_branch_dn, w_out, norm2_w, moe_group_router, moe_expert_router,
              moe_w_gate, moe_w_up, moe_w_down, final_norm_w):
    y_prompt = trunk(x_prompt, norm1_w, w_in, att_q_norm, att_k_norm, dn_conv_w, dn_a_log, dn_dt_bias,
                     dn_out_norm, w_branch_att, w_branch_dn, w_out, norm2_w, moe_group_router,
                     moe_expert_router, moe_w_gate, moe_w_up, moe_w_down, final_norm_w)
    y_sample = trunk(x_sample, norm1_w, w_in, att_q_norm, att_k_norm, dn_conv_w, dn_a_log, dn_dt_bias,
                     dn_out_norm, w_branch_att, w_branch_dn, w_out, norm2_w, moe_group_router,
                     moe_expert_router, moe_w_gate, moe_w_up, moe_w_down, final_norm_w)
    return (y_prompt, y_sample)
```

```python
import functools
import math

import numpy as np
import jax
import jax.numpy as jnp
from jax import lax
from jax.experimental import pallas as pl
from jax.experimental.pallas import tpu as pltpu

F32 = jnp.float32
BF16 = jnp.bfloat16

D_MODEL = 1024
GRID_W = 64
EPS = 1e-6
ATT_HEADS = 8
ATT_KV_HEADS = 2
ATT_GROUP = ATT_HEADS // ATT_KV_HEADS
HEAD_DIM = 64
ROPE_THETA = 10000.0
DN_HEADS = 8
DN_DIM = 64
DN_CONV = 5
N_GROUPS = 4
EXPERTS_PER_GROUP = 8
N_EXPERTS = N_GROUPS * EXPERTS_PER_GROUP
EXPERT_FF = 256

ATT_Q_W = ATT_HEADS * HEAD_DIM
ATT_KV_W = ATT_KV_HEADS * HEAD_DIM
DN_W = DN_HEADS * DN_DIM
LANES = 128
DN_CHUNK = 128
ROUTER_OFF = N_GROUPS
VMEM_LIMIT = 52 * 1024 * 1024


def _mm(a, b):
    return jnp.dot(a.astype(BF16), b.astype(BF16), preferred_element_type=F32)


def _cparams(sem):
    return pltpu.CompilerParams(dimension_semantics=sem, vmem_limit_bytes=VMEM_LIMIT)


def _full(shape):
    nd = len(shape)
    return pl.BlockSpec(shape, lambda *_: (0,) * nd)


def _rope(x, cos, sin_signed):
    n = x.shape[1]
    lane = lax.broadcasted_iota(jnp.int32, x.shape, 1)
    first = (lane % HEAD_DIM) < (HEAD_DIM // 2)
    partner = jnp.where(first, pltpu.roll(x, n - HEAD_DIM // 2, 1), pltpu.roll(x, HEAD_DIM // 2, 1))
    return x * cos + partner * sin_signed


def _inproj_kernel(x_ref, n1_ref, watt_ref, wdn_ref, wdz_ref, wab_ref, wg_ref, bd_ref, qnw_ref, knw_ref,
                   cos_ref, sin_ref, qt_ref, k_ref, vt_ref, dn_ref, dz_ref, dab_ref, sg_ref):
    tm = x_ref.shape[0]
    x = x_ref[...]
    ms = jnp.mean(x * x, axis=-1, keepdims=True)
    xn = (x * lax.rsqrt(ms + EPS) * n1_ref[...]).astype(BF16)

    att = jnp.dot(xn, watt_ref[...], preferred_element_type=F32)
    aq = att[:, :ATT_Q_W]
    ak = att[:, ATT_Q_W:ATT_Q_W + ATT_KV_W]
    av = att[:, ATT_Q_W + ATT_KV_W:]
    bd = bd_ref[...]
    qss = _mm(aq * aq, bd)
    kss = _mm(ak * ak, bd[:ATT_KV_W, :ATT_KV_W])
    cos = cos_ref[...]
    sin = sin_ref[...]
    cos4 = jnp.concatenate([cos] * (ATT_Q_W // LANES), axis=1)
    sin4 = jnp.concatenate([sin] * (ATT_Q_W // LANES), axis=1)
    q = aq * lax.rsqrt(qss * (1.0 / HEAD_DIM) + EPS) * qnw_ref[...]
    k = ak * lax.rsqrt(kss * (1.0 / HEAD_DIM) + EPS) * knw_ref[...]
    q = _rope(q, cos4, sin4) * (HEAD_DIM ** -0.5)
    k = _rope(k, cos, sin)
    qt_ref[...] = q.T.reshape(ATT_HEADS, HEAD_DIM, tm).astype(BF16)
    k_ref[0] = k[:, :HEAD_DIM].astype(BF16)
    k_ref[1] = k[:, HEAD_DIM:].astype(BF16)
    vt_ref[...] = av.T.reshape(ATT_KV_HEADS, HEAD_DIM, tm).astype(BF16)

    dn_ref[...] = jnp.dot(xn, wdn_ref[...], preferred_element_type=F32).astype(BF16)
    dz_ref[...] = jnp.dot(xn, wdz_ref[...], preferred_element_type=F32).astype(BF16)
    ab = jnp.dot(xn, wab_ref[...], preferred_element_type=F32)
    dab_ref[0] = ab[:, 0:2 * DN_HEADS]
    dab_ref[1] = ab[:, LANES:LANES + 2 * DN_HEADS]
    sg_ref[...] = jax.nn.sigmoid(jnp.dot(xn, wg_ref[...], preferred_element_type=F32)).astype(BF16)


def _inproj(x2, seq, n1, watt, wdn, wdz, wab, wg, bd, qnw, knw, cos_t, sin_t):
    m = x2.shape[0]
    tm = min(256, seq)
    per_seq = seq // tm
    row = lambda i: (i, 0)
    out_shape = (
        jax.ShapeDtypeStruct((ATT_HEADS, HEAD_DIM, m), BF16),
        jax.ShapeDtypeStruct((ATT_KV_HEADS, m, HEAD_DIM), BF16),
        jax.ShapeDtypeStruct((ATT_KV_HEADS, HEAD_DIM, m), BF16),
        jax.ShapeDtypeStruct((m, 4 * DN_W), BF16),
        jax.ShapeDtypeStruct((m, 2 * DN_W), BF16),
        jax.ShapeDtypeStruct((2, m, 2 * DN_HEADS), F32),
        jax.ShapeDtypeStruct((m, 2 * D_MODEL), BF16),
    )
    in_specs = [
        pl.BlockSpec((tm, D_MODEL), row),
        _full(n1.shape), _full(watt.shape), _full(wdn.shape), _full(wdz.shape), _full(wab.shape),
        _full(wg.shape), _full(bd.shape), _full(qnw.shape), _full(knw.shape),
        pl.BlockSpec((tm, LANES), lambda i: (i % per_seq, 0)),
        pl.BlockSpec((tm, LANES), lambda i: (i % per_seq, 0)),
    ]
    out_specs = (
        pl.BlockSpec((ATT_HEADS, HEAD_DIM, tm), lambda i: (0, 0, i)),
        pl.BlockSpec((ATT_KV_HEADS, tm, HEAD_DIM), lambda i: (0, i, 0)),
        pl.BlockSpec((ATT_KV_HEADS, HEAD_DIM, tm), lambda i: (0, 0, i)),
        pl.BlockSpec((tm, 4 * DN_W), row),
        pl.BlockSpec((tm, 2 * DN_W), row),
        pl.BlockSpec((2, tm, 2 * DN_HEADS), lambda i: (0, i, 0)),
        pl.BlockSpec((tm, 2 * D_MODEL), row),
    )
    return pl.pallas_call(
        _inproj_kernel, name="inproj", grid=(m // tm,), in_specs=in_specs, out_specs=out_specs, out_shape=out_shape,
        compiler_params=_cparams(("parallel",)),
    )(x2, n1, watt, wdn, wdz, wab, wg, bd, qnw, knw, cos_t, sin_t)


def _attn_kernel(qt_ref, k_ref, vt_ref, ot_ref, *, tk, nk):
    tq = qt_ref.shape[2]
    q = jnp.concatenate([qt_ref[r] for r in range(ATT_GROUP)], axis=1)
    n = q.shape[1]

    def body(kb, carry):
        m_run, l_run, acc = carry
        off = pl.multiple_of(kb * tk, tk)
        kblk = k_ref[0, pl.ds(off, tk), :]
        vblk = vt_ref[0, :, pl.ds(off, tk)]
        s = jnp.dot(kblk, q, preferred_element_type=F32)
        m_new = jnp.maximum(m_run, jnp.max(s, axis=0, keepdims=True))
        p = jnp.exp(s - m_new)
        alpha = jnp.exp(m_run - m_new)
        l_new = alpha * l_run + jnp.sum(p, axis=0, keepdims=True)
        acc_new = alpha * acc + jnp.dot(vblk, p.astype(BF16), preferred_element_type=F32)
        return m_new, l_new, acc_new

    init = (jnp.full((1, n), -jnp.inf, F32), jnp.zeros((1, n), F32), jnp.zeros((HEAD_DIM, n), F32))
    _, l_fin, acc = lax.fori_loop(0, nk, body, init)
    out = acc / l_fin
    for r in range(ATT_GROUP):
        ot_ref[r] = out[:, r * tq:(r + 1) * tq].astype(BF16)


def _attention(qt, kn, vt, batch, seq):
    m = qt.shape[2]
    tq = min(256, seq)
    tk = min(512, seq)
    nq = seq // tq
    qspec = pl.BlockSpec((ATT_GROUP, HEAD_DIM, tq), lambda b, g, i: (g, 0, b * nq + i))
    return pl.pallas_call(
        functools.partial(_attn_kernel, tk=tk, nk=seq // tk), name="attention",
        grid=(batch, ATT_KV_HEADS, nq),
        in_specs=[
            qspec,
            pl.BlockSpec((1, seq, HEAD_DIM), lambda b, g, i: (g, b, 0)),
            pl.BlockSpec((1, HEAD_DIM, seq), lambda b, g, i: (g, 0, b)),
        ],
        out_specs=qspec,
        out_shape=jax.ShapeDtypeStruct((ATT_HEADS, HEAD_DIM, m), BF16),
        compiler_params=_cparams(("parallel", "parallel", "parallel")),
    )(qt, kn, vt)


def _dnprep_kernel(cur_ref, prev_ref, next_ref, cw_ref, bd_ref, donorm_ref, scale_ref, kq_ref, vk_ref, buf_ref,
                   *, seq):
    tm = cur_ref.shape[0]
    halo = prev_ref.shape[0]
    i = pl.program_id(0)
    pos = (i * tm) % seq
    has_prev = pos != 0
    has_next = pos + tm != seq
    buf_ref[0:halo, :] = jnp.where(has_prev, prev_ref[...].astype(F32), 0.0)
    buf_ref[halo:halo + tm, :] = cur_ref[...].astype(F32)
    buf_ref[halo + tm:, :] = jnp.where(has_next, next_ref[...].astype(F32), 0.0)
    cw = cw_ref[...]
    pad = DN_CONV // 2
    y = None
    for j in range(DN_CONV):
        term = buf_ref[halo - pad + j:halo - pad + j + tm, :] * cw[j:j + 1, :]
        y = term if y is None else y + term
    y = y * jax.nn.sigmoid(y)
    bd = bd_ref[...]
    w = bd.shape[0]
    ysq = y * y
    ss = jnp.concatenate([_mm(ysq[:, c * w:(c + 1) * w], bd) for c in range(y.shape[1] // w)], axis=1)
    factor = jnp.where(donorm_ref[...] > 0.5, lax.rsqrt(ss + EPS), 1.0) * scale_ref[...]
    y = y * factor
    half = y.shape[1] // 2
    kq_ref[...] = y[:, :half].astype(BF16)
    vk_ref[...] = y[:, half:].astype(BF16)


def _dnprep(dnpre, seq, cw, bd, donorm, scale):
    m, width = dnpre.shape
    tm = min(512, seq)
    halo = 16
    hb = tm // halo
    last = m // halo - 1
    return pl.pallas_call(
        functools.partial(_dnprep_kernel, seq=seq), name="dnprep",
        grid=(m // tm,),
        in_specs=[
            pl.BlockSpec((tm, width), lambda i: (i, 0)),
            pl.BlockSpec((halo, width), lambda i: (jnp.maximum(i * hb - 1, 0), 0)),
            pl.BlockSpec((halo, width), lambda i: (jnp.minimum((i + 1) * hb, last), 0)),
            _full(cw.shape), _full(bd.shape), _full(donorm.shape), _full(scale.shape),
        ],
        out_specs=(pl.BlockSpec((tm, width // 2), lambda i: (i, 0)),
                   pl.BlockSpec((tm, width // 2), lambda i: (i, 0))),
        out_shape=(jax.ShapeDtypeStruct((m, width // 2), BF16), jax.ShapeDtypeStruct((m, width // 2), BF16)),
        scratch_shapes=[pltpu.VMEM((tm + 2 * halo, width), F32)],
        compiler_params=_cparams(("parallel",)),
    )(dnpre, dnpre, dnpre, cw, bd, donorm, scale)


def _bd2(a, b):
    return jnp.concatenate([jnp.concatenate([a, jnp.zeros_like(b)], axis=1),
                            jnp.concatenate([jnp.zeros_like(a), b], axis=1)], axis=0)


def _softplus(x):
    return jnp.maximum(x, 0.0) + jnp.log1p(jnp.exp(-jnp.abs(x)))


def _delta_kernel(kq_ref, vk_ref, dab_ref, ea_ref, bias_ref, o_ref, s_ref):
    C = DN_CHUNK
    H = DN_DIM
    d = pl.program_id(0)
    n = pl.program_id(2)

    @pl.when(n == 0)
    def _():
        s_ref[...] = jnp.zeros_like(s_ref)

    fwd = d == 0
    row = lax.broadcasted_iota(jnp.int32, (C, C), 0)
    col = lax.broadcasted_iota(jnp.int32, (C, C), 1)
    dist = jnp.where(fwd, row - col, col - row)
    incl = dist >= 0
    strict = dist > 0
    rowp = lax.broadcasted_iota(jnp.int32, (C, 2 * C), 0)
    colp = lax.broadcasted_iota(jnp.int32, (C, 2 * C), 1) % C
    lane = lax.broadcasted_iota(jnp.int32, (C, LANES), 1)
    lo_half = lane < H
    lane_s = lax.broadcasted_iota(jnp.int32, (H, LANES), 1)

    ab = dab_ref[0]
    g_all = -ea_ref[0] * _softplus(ab + bias_ref[0])
    beta_all = jax.nn.sigmoid(ab)
    tri = jnp.where(incl, 1.0, 0.0).astype(BF16)
    g_hi = g_all.astype(BF16)
    g_lo = (g_all - g_hi.astype(F32)).astype(BF16)
    gc_all = (jnp.dot(tri, g_hi, preferred_element_type=F32)
              + jnp.dot(tri, g_lo, preferred_element_type=F32))
    gl_all = jnp.sum(g_all, axis=0, keepdims=True)

    eye_p = jnp.where(rowp == colp, 1.0, 0.0)

    for j in range(DN_HEADS // 2):
        heads = (2 * j, 2 * j + 1)
        sl = slice(2 * j * LANES, (2 * j + 2) * LANES)
        kq_p = kq_ref[:, sl]
        kqf = kq_p.astype(F32)
        vkf = vk_ref[:, sl].astype(F32)

        gccol, gcrow, bcol, gl, kT, decay = [], [], [], [], [], []
        for hh, h in enumerate(heads):
            gccol.append(jnp.broadcast_to(gc_all[:, h:h + 1], (C, LANES)))
            bcol.append(jnp.broadcast_to(beta_all[:, DN_HEADS + h:DN_HEADS + h + 1], (C, LANES)))
            gl.append(gl_all[:, h:h + 1])
            gcrow.append(gccol[hh].T)
            kT.append(kqf[:, hh * LANES:(hh + 1) * LANES].T[0:H, :])
            decay.append(jnp.where(incl, jnp.exp(jnp.minimum(gccol[hh] - gcrow[hh], 0.0)), 0.0))

        def rhs1(t):
            z = jnp.zeros_like(t)
            return jnp.concatenate([jnp.concatenate([t, z], axis=1), jnp.concatenate([z, t], axis=1)], axis=0)

        p1 = _mm(kq_p, _bd2(rhs1(kT[0]), rhs1(kT[1])))
        a_h, qkd = [], []
        for hh in range(2):
            kk = p1[:, (2 * hh) * C:(2 * hh + 1) * C]
            qk = p1[:, (2 * hh + 1) * C:(2 * hh + 2) * C]
            a_h.append(jnp.where(strict, kk * decay[hh], 0.0) * bcol[hh])
            qkd.append(qk * decay[hh])
        a_p = jnp.concatenate(a_h, axis=1)

        x_p = eye_p - jnp.where(rowp // 2 == colp // 2, a_p, 0.0)
        b = 2
        while b < C:
            msk = (rowp // (2 * b) == colp // (2 * b)) & (rowp // b != colp // b)
            e_p = jnp.where(msk, a_p, 0.0)
            g_p = _mm(e_p, _bd2(x_p[:, :C], x_p[:, C:]))
            x_p = x_p - _mm(x_p, _bd2(g_p[:, :C], g_p[:, C:]))
            b *= 2

        rhs2 = [vkf[:, hh * LANES:(hh + 1) * LANES] * bcol[hh] * jnp.where(lo_half, 1.0, jnp.exp(gccol[hh]))
                for hh in range(2)]
        uw = _mm(x_p, _bd2(rhs2[0], rhs2[1]))
        ol = _mm(jnp.concatenate(qkd, axis=1), _bd2(uw[:, :LANES], uw[:, LANES:]))
        zmult = jnp.concatenate(
            [jnp.where(lo_half, jnp.exp(gl[hh] - gccol[hh]), jnp.exp(gccol[hh])) for hh in range(2)], axis=1)
        z_p = kqf * zmult - ol
        kdT = jnp.concatenate([kT[hh] * jnp.exp(gl[hh] - gcrow[hh][0:H, :]) for hh in range(2)], axis=0)
        nw = _mm(kdT, uw)
        nw0 = nw[0:H, 0:LANES]
        nw1 = nw[H:2 * H, LANES:2 * LANES]
        nwm = _bd2(nw0, nw1)
        s_old = [s_ref[heads[0]], s_ref[heads[1]]]
        sblk = [jnp.concatenate([jnp.zeros((H, LANES), F32), s_old[hh]], axis=0) for hh in range(2)]
        r = _mm(jnp.concatenate([nwm, z_p], axis=0), _bd2(sblk[0], sblk[1]))
        lane_p = lax.broadcasted_iota(jnp.int32, (C, 2 * LANES), 1) % LANES
        o_ref[0, :, sl] = jnp.where(lane_p < H, r[2 * H:, :] + ol, 0.0)
        wks = (r[0:H, 0:LANES], r[H:2 * H, LANES:2 * LANES])
        for hh, nwh in enumerate((nw0, nw1)):
            s_new = jnp.exp(gl[hh]) * s_old[hh] + nwh - wks[hh]
            s_ref[heads[hh]] = jnp.where(lane_s < H, s_new, 0.0)


def _delta_rule(kq, vk, dab, ea, bias, batch, seq):
    m = kq.shape[0]
    C = DN_CHUNK
    nc = seq // C
    width = kq.shape[1]

    def blk(d, b, n):
        return b * nc + jnp.where(d == 0, n, nc - 1 - n)

    return pl.pallas_call(
        _delta_kernel, name="delta",
        grid=(2, batch, nc),
        in_specs=[
            pl.BlockSpec((C, width), lambda d, b, n: (blk(d, b, n), 0)),
            pl.BlockSpec((C, width), lambda d, b, n: (blk(d, b, n), 0)),
            pl.BlockSpec((1, C, 2 * DN_HEADS), lambda d, b, n: (d, blk(d, b, n), 0)),
            pl.BlockSpec((1, 1, 2 * DN_HEADS), lambda d, b, n: (d, 0, 0)),
            pl.BlockSpec((1, 1, 2 * DN_HEADS), lambda d, b, n: (d, 0, 0)),
        ],
        out_specs=pl.BlockSpec((1, C, width), lambda d, b, n: (d, blk(d, b, n), 0)),
        out_shape=jax.ShapeDtypeStruct((2, m, width), F32),
        scratch_shapes=[pltpu.VMEM((DN_HEADS, DN_DIM, LANES), F32)],
        compiler_params=_cparams(("parallel", "parallel", "arbitrary")),
    )(kq, vk, dab, ea, bias)


def _merge_kernel(ot_ref, o2_ref, dz_ref, sg_ref, x_ref, wa_ref, wb_ref, wo_ref, bdo_ref, onw_ref, n2_ref,
                  wr_hi_ref, wr_lo_ref, x1_ref, xn_ref, lg_ref):
    tm = x_ref.shape[0]
    att = ot_ref[...].reshape(ATT_Q_W, tm).astype(F32).T
    ya = _mm(att, wa_ref[...])
    o = o2_ref[0] + o2_ref[1]
    osq = o * o
    bdo = bdo_ref[...]
    w = bdo.shape[0]
    ss = jnp.concatenate([_mm(osq[:, c * w:(c + 1) * w], bdo) for c in range(o.shape[1] // w)], axis=1)
    dz = dz_ref[...].astype(F32)
    dn = o * lax.rsqrt(ss * (1.0 / DN_DIM) + EPS) * onw_ref[...] * (dz * jax.nn.sigmoid(dz))
    yb = _mm(dn, wb_ref[...])
    sg = sg_ref[...].astype(F32)
    mix = sg[:, :D_MODEL] * ya + sg[:, D_MODEL:] * yb
    x1 = x_ref[...] + _mm(mix, wo_ref[...])
    x1_ref[...] = x1
    ms = jnp.mean(x1 * x1, axis=-1, keepdims=True)
    xn = x1 * lax.rsqrt(ms + EPS) * n2_ref[...]
    xn_hi = xn.astype(BF16)
    xn_lo = (xn - xn_hi.astype(F32)).astype(BF16)
    xn_ref[...] = xn_hi
    whi = wr_hi_ref[...]
    lg_ref[...] = (jnp.dot(xn_hi, whi, preferred_element_type=F32)
                   + jnp.dot(xn_lo, whi, preferred_element_type=F32)
                   + jnp.dot(xn_hi, wr_lo_ref[...], preferred_element_type=F32))


def _merge(ot, o2, dz, sg, x2, wa, wb, wo, bdo, onw, n2, wr_hi, wr_lo):
    m = x2.shape[0]
    tm = 256
    row = lambda i: (i, 0)
    return pl.pallas_call(
        _merge_kernel, name="merge", grid=(m // tm,),
        in_specs=[
            pl.BlockSpec((ATT_HEADS, HEAD_DIM, tm), lambda i: (0, 0, i)),
            pl.BlockSpec((2, tm, o2.shape[2]), lambda i: (0, i, 0)),
            pl.BlockSpec((tm, dz.shape[1]), row),
            pl.BlockSpec((tm, sg.shape[1]), row),
            pl.BlockSpec((tm, D_MODEL), row),
            _full(wa.shape), _full(wb.shape), _full(wo.shape), _full(bdo.shape), _full(onw.shape),
            _full(n2.shape), _full(wr_hi.shape), _full(wr_lo.shape),
        ],
        out_specs=(pl.BlockSpec((tm, D_MODEL), row), pl.BlockSpec((tm, D_MODEL), row),
                   pl.BlockSpec((tm, LANES), row)),
        out_shape=(jax.ShapeDtypeStruct((m, D_MODEL), F32), jax.ShapeDtypeStruct((m, D_MODEL), BF16),
                   jax.ShapeDtypeStruct((m, LANES), F32)),
        compiler_params=_cparams(("parallel",)),
    )(ot, o2, dz, sg, x2, wa, wb, wo, bdo, onw, n2, wr_hi, wr_lo)


def _route_kernel(lg_ref, gate_ref):
    lg = lg_ref[...]
    lane_i = lax.broadcasted_iota(jnp.int32, lg.shape, 1)
    lane = lane_i.astype(F32)
    group_of = ((lane_i - ROUTER_OFF) // EXPERTS_PER_GROUP).astype(F32)
    neg = -jnp.inf

    def first_argmax(v):
        mx = jnp.max(v, axis=-1, keepdims=True)
        idx = jnp.min(jnp.where(v == mx, lane, float(LANES)), axis=-1, keepdims=True)
        return mx, idx

    gl = jnp.where(lane_i < N_GROUPS, lg, neg)
    gmax, gidx = first_argmax(gl)
    gval = 1.0 / jnp.sum(jnp.exp(gl - gmax), axis=-1, keepdims=True)
    is_exp = (lane_i >= ROUTER_OFF) & (lane_i < ROUTER_OFF + N_EXPERTS)
    sel = is_exp & (group_of == gidx)
    el = jnp.where(sel, lg, neg)
    m1, i1 = first_argmax(el)
    el2 = jnp.where(lane == i1, neg, el)
    m2, i2 = first_argmax(el2)
    r = jnp.exp(m2 - m1)
    w1 = gval / (1.0 + r)
    w2 = gval * r / (1.0 + r)
    gate_ref[...] = jnp.where(lane == i1, w1, 0.0) + jnp.where(lane == i2, w2, 0.0)


def _route(logits):
    m = logits.shape[0]
    tm = 1024 if m % 1024 == 0 else 256
    spec = pl.BlockSpec((tm, LANES), lambda i: (i, 0))
    return pl.pallas_call(
        _route_kernel, name="route", grid=(m // tm,), in_specs=[spec], out_specs=spec,
        out_shape=jax.ShapeDtypeStruct((m, LANES), F32),
        compiler_params=_cparams(("parallel",)),
    )(logits)


def _moe_kernel(xn_ref, x1_ref, gate_ref, wg_ref, wu_ref, wd_ref, fw_ref, y_ref, acc_ref):
    e = pl.program_id(1)

    @pl.when(e == 0)
    def _():
        acc_ref[...] = jnp.zeros_like(acc_ref)

    xn = xn_ref[...]
    hg = jnp.dot(xn, wg_ref[0], preferred_element_type=F32)
    hu = jnp.dot(xn, wu_ref[0], preferred_element_type=F32)
    gcol = pltpu.roll(gate_ref[...], LANES - ROUTER_OFF - e, 1)[:, 0:1]
    hid = hg * jax.nn.sigmoid(hg) * hu
    acc_ref[...] += gcol * _mm(hid, wd_ref[0])

    @pl.when(e == N_EXPERTS - 1)
    def _():
        x = x1_ref[...] + acc_ref[...]
        ms = jnp.mean(x * x, axis=-1, keepdims=True)
        y_ref[...] = x * lax.rsqrt(ms + EPS) * fw_ref[...]


def _moe_final(xn, x1, gates, wg, wu, wd, fw):
    m = xn.shape[0]
    tm = 1024 if m % 1024 == 0 else 256
    row = lambda i, e: (i, 0)
    return pl.pallas_call(
        _moe_kernel, name="moe", grid=(m // tm, N_EXPERTS),
        in_specs=[
            pl.BlockSpec((tm, D_MODEL), row), pl.BlockSpec((tm, D_MODEL), row), pl.BlockSpec((tm, LANES), row),
            pl.BlockSpec((1, D_MODEL, EXPERT_FF), lambda i, e: (e, 0, 0)),
            pl.BlockSpec((1, D_MODEL, EXPERT_FF), lambda i, e: (e, 0, 0)),
            pl.BlockSpec((1, EXPERT_FF, D_MODEL), lambda i, e: (e, 0, 0)),
            pl.BlockSpec((1, D_MODEL), lambda i, e: (0, 0)),
        ],
        out_specs=pl.BlockSpec((tm, D_MODEL), row),
        out_shape=jax.ShapeDtypeStruct((m, D_MODEL), F32),
        scratch_shapes=[pltpu.VMEM((tm, D_MODEL), F32)],
        compiler_params=_cparams(("parallel", "arbitrary")),
    )(xn, x1, gates, wg, wu, wd, fw)


def _block_ones(n, blk):
    idx = np.arange(n)
    return jnp.asarray((idx[:, None] // blk == idx[None, :] // blk), dtype=BF16)


def _prepare(norm1_w, w_in, att_q_norm, att_k_norm, dn_conv_w, dn_a_log, dn_dt_bias, dn_out_norm,
             w_branch_att, w_branch_dn, w_out, norm2_w, moe_group_router, moe_expert_router,
             moe_w_gate, moe_w_up, moe_w_down, final_norm_w):
    w_in = w_in[0]
    o_q, o_k, o_v = 0, ATT_Q_W, ATT_Q_W + ATT_KV_W
    o_dq = o_v + ATT_KV_W
    o_dk, o_dv, o_dz = o_dq + DN_W, o_dq + 2 * DN_W, o_dq + 3 * DN_W
    o_da = o_dz + DN_W
    o_db = o_da + 2 * DN_HEADS
    o_ga = o_db + 2 * DN_HEADS
    o_gb = o_ga + D_MODEL

    deint = np.concatenate([np.arange(0, HEAD_DIM, 2), np.arange(1, HEAD_DIM, 2)])
    q_cols = np.concatenate([o_q + h * HEAD_DIM + deint for h in range(ATT_HEADS)])
    k_cols = np.concatenate([o_k + h * HEAD_DIM + deint for h in range(ATT_KV_HEADS)])
    v_cols = np.arange(o_v, o_v + ATT_KV_W)
    watt = w_in[:, np.concatenate([q_cols, k_cols, v_cols])].astype(BF16)
    qnw = jnp.tile(att_q_norm[0][deint], ATT_HEADS)[None, :]
    knw = jnp.tile(att_k_norm[0][deint], ATT_KV_HEADS)[None, :]

    hd = np.arange(DN_DIM)
    kq_cols = np.concatenate([np.concatenate([o_dk + h * DN_DIM + hd, o_dq + h * DN_DIM + hd]) for h in range(DN_HEADS)])
    vk_cols = np.concatenate([np.concatenate([o_dv + h * DN_DIM + hd, o_dk + h * DN_DIM + hd]) for h in range(DN_HEADS)])
    dn_cols = np.concatenate([kq_cols, vk_cols])
    wdn = w_in[:, dn_cols].astype(BF16)
    cw = jnp.concatenate([dn_conv_w[0][:, dn_cols - o_dq], jnp.zeros((8 - DN_CONV, dn_cols.size), F32)], axis=0)
    is_q = np.concatenate([np.tile(np.concatenate([np.zeros(DN_DIM), np.ones(DN_DIM)]), DN_HEADS), np.zeros(2 * DN_W)])
    is_v = np.concatenate([np.zeros(2 * DN_W), np.tile(np.concatenate([np.ones(DN_DIM), np.zeros(DN_DIM)]), DN_HEADS)])
    donorm = jnp.asarray(1.0 - is_v, F32)[None, :]
    scale = jnp.asarray(np.where(is_q > 0, DN_DIM ** -0.5, 1.0), F32)[None, :]

    pad_cols = np.concatenate([np.concatenate([h * DN_DIM + hd, np.full(DN_DIM, -1)]) for h in range(DN_HEADS)])
    valid = jnp.asarray(pad_cols >= 0)
    take = np.maximum(pad_cols, 0)
    wdz = jnp.where(valid[None, :], w_in[:, o_dz + take], 0.0).astype(BF16)
    onw = jnp.where(valid, jnp.tile(dn_out_norm[0], 2 * DN_HEADS), 0.0)[None, :]
    wb = jnp.where(valid[:, None], w_branch_dn[0][take, :], 0.0).astype(BF16)

    wab = jnp.zeros((D_MODEL, 2 * LANES), F32)
    for dirn in range(2):
        wab = wab.at[:, dirn * LANES:dirn * LANES + DN_HEADS].set(
            w_in[:, o_da + dirn * DN_HEADS:o_da + (dirn + 1) * DN_HEADS])
        wab = wab.at[:, dirn * LANES + DN_HEADS:dirn * LANES + 2 * DN_HEADS].set(
            w_in[:, o_db + dirn * DN_HEADS:o_db + (dirn + 1) * DN_HEADS])
    wab = wab.astype(BF16)
    wg = w_in[:, o_ga:o_gb + D_MODEL].astype(BF16)

    zeros8 = jnp.zeros((2, DN_HEADS), F32)
    ea = jnp.concatenate([jnp.exp(dn_a_log[0]), zeros8], axis=1)[:, None, :]
    bias = jnp.concatenate([dn_dt_bias[0], zeros8], axis=1)[:, None, :]

    idx = np.arange(2 * LANES)
    ok = (idx % LANES) < DN_DIM
    bdo = jnp.asarray((idx[:, None] // LANES == idx[None, :] // LANES) & ok[:, None] & ok[None, :], dtype=BF16)

    wr = jnp.concatenate([moe_group_router[0], moe_expert_router[0],
                          jnp.zeros((D_MODEL, LANES - N_GROUPS - N_EXPERTS), F32)], axis=1)
    wr_hi = wr.astype(BF16)
    wr_lo = (wr - wr_hi.astype(F32)).astype(BF16)

    return dict(
        n1=norm1_w[0][None, :], watt=watt, wdn=wdn, wdz=wdz, wab=wab, wg=wg,
        bd_att=_block_ones(ATT_Q_W, HEAD_DIM), qnw=qnw, knw=knw,
        cw=cw, bd_dn=_block_ones(2 * LANES, DN_DIM), donorm=donorm, scale=scale, ea=ea, bias=bias,
        wa=w_branch_att[0].astype(BF16), wb=wb, wo=w_out[0].astype(BF16), bdo=bdo, onw=onw,
        n2=norm2_w[0][None, :], wr_hi=wr_hi, wr_lo=wr_lo,
        wge=moe_w_gate[0].astype(BF16), wue=moe_w_up[0].astype(BF16), wde=moe_w_down[0].astype(BF16),
        fw=final_norm_w[None, :],
    )


def _rope_tables(seq):
    t = np.arange(seq)
    axis_dim = HEAD_DIM // 2
    inv = ROPE_THETA ** (-np.arange(0, axis_dim, 2, dtype=np.float32) / axis_dim)
    r = (t // GRID_W).astype(np.float32)
    c = (t % GRID_W).astype(np.float32)
    ang = np.concatenate([r[:, None] * inv, c[:, None] * inv], axis=-1).astype(np.float32)
    ang = jnp.asarray(ang)
    cos, sin = jnp.cos(ang), jnp.sin(ang)
    cos_t = jnp.tile(jnp.concatenate([cos, cos], axis=1), (1, LANES // HEAD_DIM))
    sin_t = jnp.tile(jnp.concatenate([-sin, sin], axis=1), (1, LANES // HEAD_DIM))
    return cos_t, sin_t


def _trunk(x, p):
    batch, seq, _ = x.shape
    x2 = x.reshape(batch * seq, D_MODEL)
    cos_t, sin_t = _rope_tables(seq)
    qt, kn, vt, dnpre, dz, dab, sg = _inproj(x2, seq, p["n1"], p["watt"], p["wdn"], p["wdz"], p["wab"], p["wg"],
                                            p["bd_att"], p["qnw"], p["knw"], cos_t, sin_t)
    ot = _attention(qt, kn, vt, batch, seq)
    kq, vk = _dnprep(dnpre, seq, p["cw"], p["bd_dn"], p["donorm"], p["scale"])
    o2 = _delta_rule(kq, vk, dab, p["ea"], p["bias"], batch, seq)
    x1, xn, logits = _merge(ot, o2, dz, sg, x2, p["wa"], p["wb"], p["wo"], p["bdo"], p["onw"], p["n2"],
                            p["wr_hi"], p["wr_lo"])
    gates = _route(logits)
    y = _moe_final(xn, x1, gates, p["wge"], p["wue"], p["wde"], p["fw"])
    return y.reshape(batch, seq, D_MODEL)


def kernel(x_prompt, x_sample, norm1_w, w_in, att_q_norm, att_k_norm, dn_conv_w, dn_a_log, dn_dt_bias, dn_out_norm, w_branch_att, w_branch_dn, w_out, norm2_w, moe_group_router, moe_expert_router, moe_w_gate, moe_w_up, moe_w_down, final_norm_w):
    p = _prepare(norm1_w, w_in, att_q_norm, att_k_norm, dn_conv_w, dn_a_log, dn_dt_bias, dn_out_norm,
                 w_branch_att, w_branch_dn, w_out, norm2_w, moe_group_router, moe_expert_router,
                 moe_w_gate, moe_w_up, moe_w_down, final_norm_w)
    return (_trunk(x_prompt, p), _trunk(x_sample, p))
```

```python
import functools
import math

import numpy as np
import jax
import jax.numpy as jnp
from jax import lax
from jax.experimental import pallas as pl
from jax.experimental.pallas import tpu as pltpu

F32 = jnp.float32
BF16 = jnp.bfloat16

D_MODEL = 1024
GRID_W = 64
EPS = 1e-6
ATT_HEADS = 8
ATT_KV_HEADS = 2
ATT_GROUP = ATT_HEADS // ATT_KV_HEADS
HEAD_DIM = 64
ROPE_THETA = 10000.0
DN_HEADS = 8
DN_DIM = 64
DN_CONV = 5
N_GROUPS = 4
EXPERTS_PER_GROUP = 8
N_EXPERTS = N_GROUPS * EXPERTS_PER_GROUP
EXPERT_FF = 256

ATT_Q_W = ATT_HEADS * HEAD_DIM
ATT_KV_W = ATT_KV_HEADS * HEAD_DIM
DN_W = DN_HEADS * DN_DIM
LANES = 128
DN_CHUNK = 128
ROUTER_OFF = N_GROUPS
VMEM_LIMIT = 52 * 1024 * 1024
LOG2E = math.log2(math.e)
SCORE_BOUND_MAX = 50.0
SCORE_BOUND_SLACK = 1.05


def _mm(a, b):
    return jnp.dot(a.astype(BF16), b.astype(BF16), preferred_element_type=F32)


def _cparams(sem):
    return pltpu.CompilerParams(dimension_semantics=sem, vmem_limit_bytes=VMEM_LIMIT)


def _full(shape):
    nd = len(shape)
    return pl.BlockSpec(shape, lambda *_: (0,) * nd)


def _rope(x, cos, sin_signed):
    n = x.shape[1]
    lane = lax.broadcasted_iota(jnp.int32, x.shape, 1)
    first = (lane % HEAD_DIM) < (HEAD_DIM // 2)
    partner = jnp.where(first, pltpu.roll(x, n - HEAD_DIM // 2, 1), pltpu.roll(x, HEAD_DIM // 2, 1))
    return x * cos + partner * sin_signed


def _inproj_kernel(x_ref, n1_ref, watt_ref, wdn_ref, wdz_ref, wab_ref, wg_ref, bd_ref, qnw_ref, knw_ref,
                   cos_ref, sin_ref, qt_ref, k_ref, vt_ref, dn_ref, dz_ref, dab_ref, sg_ref, kmx_ref):
    tm = x_ref.shape[0]
    x = x_ref[...]
    ms = jnp.mean(x * x, axis=-1, keepdims=True)
    xn = (x * lax.rsqrt(ms + EPS) * n1_ref[...]).astype(BF16)

    att = jnp.dot(xn, watt_ref[...], preferred_element_type=F32)
    aq = att[:, :ATT_Q_W]
    ak = att[:, ATT_Q_W:ATT_Q_W + ATT_KV_W]
    av = att[:, ATT_Q_W + ATT_KV_W:]
    bd = bd_ref[...]
    qss = _mm(aq * aq, bd)
    kss = _mm(ak * ak, bd[:ATT_KV_W, :ATT_KV_W])
    cos = cos_ref[...]
    sin = sin_ref[...]
    cos4 = jnp.concatenate([cos] * (ATT_Q_W // LANES), axis=1)
    sin4 = jnp.concatenate([sin] * (ATT_Q_W // LANES), axis=1)
    q = aq * lax.rsqrt(qss * (1.0 / HEAD_DIM) + EPS) * qnw_ref[...]
    k = ak * lax.rsqrt(kss * (1.0 / HEAD_DIM) + EPS) * knw_ref[...]
    q = _rope(q, cos4, sin4) * (HEAD_DIM ** -0.5 * LOG2E)
    kmx_ref[0] = jnp.broadcast_to(jnp.max(_mm(k * k, bd[:ATT_KV_W, :ATT_KV_W]), axis=0, keepdims=True),
                                  (8, ATT_KV_W))
    k = _rope(k, cos, sin)
    qt_ref[...] = q.T.reshape(ATT_HEADS, HEAD_DIM, tm).astype(BF16)
    k_ref[0] = k[:, :HEAD_DIM].astype(BF16)
    k_ref[1] = k[:, HEAD_DIM:].astype(BF16)
    vt_ref[...] = av.T.reshape(ATT_KV_HEADS, HEAD_DIM, tm).astype(BF16)

    dn_ref[...] = jnp.dot(xn, wdn_ref[...], preferred_element_type=F32).astype(BF16)
    dz_ref[...] = jnp.dot(xn, wdz_ref[...], preferred_element_type=F32).astype(BF16)
    ab = jnp.dot(xn, wab_ref[...], preferred_element_type=F32)
    dab_ref[0] = ab[:, 0:2 * DN_HEADS]
    dab_ref[1] = ab[:, LANES:LANES + 2 * DN_HEADS]
    sg_ref[...] = jax.nn.sigmoid(jnp.dot(xn, wg_ref[...], preferred_element_type=F32)).astype(BF16)


def _inproj(x2, seq, n1, watt, wdn, wdz, wab, wg, bd, qnw, knw, cos_t, sin_t):
    m = x2.shape[0]
    tm = min(256, seq)
    per_seq = seq // tm
    row = lambda i: (i, 0)
    out_shape = (
        jax.ShapeDtypeStruct((ATT_HEADS, HEAD_DIM, m), BF16),
        jax.ShapeDtypeStruct((ATT_KV_HEADS, m, HEAD_DIM), BF16),
        jax.ShapeDtypeStruct((ATT_KV_HEADS, HEAD_DIM, m), BF16),
        jax.ShapeDtypeStruct((m, 4 * DN_W), BF16),
        jax.ShapeDtypeStruct((m, 2 * DN_W), BF16),
        jax.ShapeDtypeStruct((2, m, 2 * DN_HEADS), F32),
        jax.ShapeDtypeStruct((m, 2 * D_MODEL), BF16),
        jax.ShapeDtypeStruct((m // tm, 8, ATT_KV_W), F32),
    )
    in_specs = [
        pl.BlockSpec((tm, D_MODEL), row),
        _full(n1.shape), _full(watt.shape), _full(wdn.shape), _full(wdz.shape), _full(wab.shape),
        _full(wg.shape), _full(bd.shape), _full(qnw.shape), _full(knw.shape),
        pl.BlockSpec((tm, LANES), lambda i: (i % per_seq, 0)),
        pl.BlockSpec((tm, LANES), lambda i: (i % per_seq, 0)),
    ]
    out_specs = (
        pl.BlockSpec((ATT_HEADS, HEAD_DIM, tm), lambda i: (0, 0, i)),
        pl.BlockSpec((ATT_KV_HEADS, tm, HEAD_DIM), lambda i: (0, i, 0)),
        pl.BlockSpec((ATT_KV_HEADS, HEAD_DIM, tm), lambda i: (0, 0, i)),
        pl.BlockSpec((tm, 4 * DN_W), row),
        pl.BlockSpec((tm, 2 * DN_W), row),
        pl.BlockSpec((2, tm, 2 * DN_HEADS), lambda i: (0, i, 0)),
        pl.BlockSpec((tm, 2 * D_MODEL), row),
        pl.BlockSpec((1, 8, ATT_KV_W), lambda i: (i, 0, 0)),
    )
    return pl.pallas_call(
        _inproj_kernel, name="inproj", grid=(m // tm,), in_specs=in_specs, out_specs=out_specs, out_shape=out_shape,
        compiler_params=_cparams(("parallel",)),
    )(x2, n1, watt, wdn, wdz, wab, wg, bd, qnw, knw, cos_t, sin_t)


def _attn_kernel(qt_ref, k_ref, vt_ref, kmx_ref, ot_ref, *, tk, nk):
    tq = qt_ref.shape[2]
    q = jnp.concatenate([qt_ref[r] for r in range(ATT_GROUP)], axis=1)
    n = q.shape[1]
    qf = q.astype(F32)
    bound = jnp.sqrt(jnp.sum(qf * qf, axis=0, keepdims=True) * kmx_ref[0, 0][0:1, 0:1]) * SCORE_BOUND_SLACK
    small = jnp.max(bound) <= SCORE_BOUND_MAX
    ones_rows = 16

    def blocks(kb):
        off = pl.multiple_of(kb * tk, tk)
        return k_ref[0, pl.ds(off, tk), :], vt_ref[0, :, pl.ds(off, tk)]

    def write(out):
        for r in range(ATT_GROUP):
            ot_ref[r] = out[:, r * tq:(r + 1) * tq].astype(BF16)

    @pl.when(small)
    def _():
        def body(kb, acc):
            kblk, vblk = blocks(kb)
            s = jnp.dot(kblk, q, preferred_element_type=F32)
            p = jnp.exp2(s - bound).astype(BF16)
            vaug = jnp.concatenate([vblk, jnp.ones((ones_rows, tk), BF16)], axis=0)
            return acc + jnp.dot(vaug, p, preferred_element_type=F32)

        acc = lax.fori_loop(0, nk, body, jnp.zeros((HEAD_DIM + ones_rows, n), F32))
        write(acc[0:HEAD_DIM] / acc[HEAD_DIM:HEAD_DIM + 1])

    @pl.when(jnp.logical_not(small))
    def _():
        def body(kb, carry):
            m_run, l_run, acc = carry
            kblk, vblk = blocks(kb)
            s = jnp.dot(kblk, q, preferred_element_type=F32)
            m_new = jnp.maximum(m_run, jnp.max(s, axis=0, keepdims=True))
            p = jnp.exp2(s - m_new)
            alpha = jnp.exp2(m_run - m_new)
            l_new = alpha * l_run + jnp.sum(p, axis=0, keepdims=True)
            acc_new = alpha * acc + jnp.dot(vblk, p.astype(BF16), preferred_element_type=F32)
            return m_new, l_new, acc_new

        init = (jnp.full((1, n), -jnp.inf, F32), jnp.zeros((1, n), F32), jnp.zeros((HEAD_DIM, n), F32))
        _, l_fin, acc = lax.fori_loop(0, nk, body, init)
        write(acc / l_fin)


def _attention(qt, kn, vt, kmx, batch, seq):
    m = qt.shape[2]
    tq = min(512, seq)
    tk = min(512, seq)
    nq = seq // tq
    qspec = pl.BlockSpec((ATT_GROUP, HEAD_DIM, tq), lambda b, g, i: (g, 0, b * nq + i))
    return pl.pallas_call(
        functools.partial(_attn_kernel, tk=tk, nk=seq // tk), name="attention",
        grid=(batch, ATT_KV_HEADS, nq),
        in_specs=[
            qspec,
            pl.BlockSpec((1, seq, HEAD_DIM), lambda b, g, i: (g, b, 0)),
            pl.BlockSpec((1, HEAD_DIM, seq), lambda b, g, i: (g, 0, b)),
            pl.BlockSpec((1, 1, 8, LANES), lambda b, g, i: (b, g, 0, 0)),
        ],
        out_specs=qspec,
        out_shape=jax.ShapeDtypeStruct((ATT_HEADS, HEAD_DIM, m), BF16),
        compiler_params=_cparams(("parallel", "parallel", "parallel")),
    )(qt, kn, vt, kmx)


def _dnprep_kernel(cur_ref, prev_ref, next_ref, cw_ref, bd_ref, donorm_ref, scale_ref, kq_ref, vk_ref, buf_ref,
                   *, seq):
    tm = cur_ref.shape[0]
    halo = prev_ref.shape[0]
    i = pl.program_id(0)
    pos = (i * tm) % seq
    has_prev = pos != 0
    has_next = pos + tm != seq
    buf_ref[0:halo, :] = jnp.where(has_prev, prev_ref[...].astype(F32), 0.0)
    buf_ref[halo:halo + tm, :] = cur_ref[...].astype(F32)
    buf_ref[halo + tm:, :] = jnp.where(has_next, next_ref[...].astype(F32), 0.0)
    cw = cw_ref[...]
    pad = DN_CONV // 2
    y = None
    for j in range(DN_CONV):
        term = buf_ref[halo - pad + j:halo - pad + j + tm, :] * cw[j:j + 1, :]
        y = term if y is None else y + term
    y = y * jax.nn.sigmoid(y)
    bd = bd_ref[...]
    w = bd.shape[0]
    ysq = y * y
    ss = jnp.concatenate([_mm(ysq[:, c * w:(c + 1) * w], bd) for c in range(y.shape[1] // w)], axis=1)
    factor = jnp.where(donorm_ref[...] > 0.5, lax.rsqrt(ss + EPS), 1.0) * scale_ref[...]
    y = y * factor
    half = y.shape[1] // 2
    kq_ref[...] = y[:, :half].astype(BF16)
    vk_ref[...] = y[:, half:].astype(BF16)


def _dnprep(dnpre, seq, cw, bd, donorm, scale):
    m, width = dnpre.shape
    tm = min(512, seq)
    halo = 16
    hb = tm // halo
    last = m // halo - 1
    return pl.pallas_call(
        functools.partial(_dnprep_kernel, seq=seq), name="dnprep",
        grid=(m // tm,),
        in_specs=[
            pl.BlockSpec((tm, width), lambda i: (i, 0)),
            pl.BlockSpec((halo, width), lambda i: (jnp.maximum(i * hb - 1, 0), 0)),
            pl.BlockSpec((halo, width), lambda i: (jnp.minimum((i + 1) * hb, last), 0)),
            _full(cw.shape), _full(bd.shape), _full(donorm.shape), _full(scale.shape),
        ],
        out_specs=(pl.BlockSpec((tm, width // 2), lambda i: (i, 0)),
                   pl.BlockSpec((tm, width // 2), lambda i: (i, 0))),
        out_shape=(jax.ShapeDtypeStruct((m, width // 2), BF16), jax.ShapeDtypeStruct((m, width // 2), BF16)),
        scratch_shapes=[pltpu.VMEM((tm + 2 * halo, width), F32)],
        compiler_params=_cparams(("parallel",)),
    )(dnpre, dnpre, dnpre, cw, bd, donorm, scale)


def _bd2(a, b):
    return jnp.concatenate([jnp.concatenate([a, jnp.zeros_like(b)], axis=1),
                            jnp.concatenate([jnp.zeros_like(a), b], axis=1)], axis=0)


def _softplus(x):
    return jnp.maximum(x, 0.0) + jnp.log1p(jnp.exp(-jnp.abs(x)))


def _delta_kernel(kqf_ref, kqb_ref, vkf_ref, vkb_ref, dabf_ref, dabb_ref, ea_ref, bias_ref, of_ref, ob_ref, s_ref):
    C = DN_CHUNK
    H = DN_DIM
    n = pl.program_id(1)

    @pl.when(n == 0)
    def _():
        s_ref[...] = jnp.zeros_like(s_ref)

    row = lax.broadcasted_iota(jnp.int32, (C, C), 0)
    col = lax.broadcasted_iota(jnp.int32, (C, C), 1)
    incl = (row >= col, row <= col)
    strict = (row > col, row < col)
    rowp = lax.broadcasted_iota(jnp.int32, (C, 2 * C), 0)
    colp = lax.broadcasted_iota(jnp.int32, (C, 2 * C), 1) % C
    lane = lax.broadcasted_iota(jnp.int32, (C, LANES), 1)
    lo_half = lane < H
    lane_s = lax.broadcasted_iota(jnp.int32, (H, LANES), 1)
    lane_p = lax.broadcasted_iota(jnp.int32, (C, 2 * LANES), 1) % LANES
    eye_p = jnp.where(rowp == colp, 1.0, 0.0)
    blk2 = rowp // 2 == colp // 2
    levels = []
    b = 2
    while b < C:
        levels.append((rowp // (2 * b) == colp // (2 * b)) & (rowp // b != colp // b))
        b *= 2

    kq_refs = (kqf_ref, kqb_ref)
    vk_refs = (vkf_ref, vkb_ref)
    dab_refs = (dabf_ref, dabb_ref)
    o_refs = (of_ref, ob_ref)

    gc_all, gl_all, beta_all = [], [], []
    for d in range(2):
        ab = dab_refs[d][0]
        g = -ea_ref[d] * _softplus(ab + bias_ref[d])
        beta_all.append(jax.nn.sigmoid(ab))
        tri = jnp.where(incl[d], 1.0, 0.0).astype(BF16)
        g_hi = g.astype(BF16)
        g_lo = (g - g_hi.astype(F32)).astype(BF16)
        gc_all.append(jnp.dot(tri, g_hi, preferred_element_type=F32)
                      + jnp.dot(tri, g_lo, preferred_element_type=F32))
        gl_all.append(jnp.sum(g, axis=0, keepdims=True))

    units = [(d, j) for d in range(2) for j in range(DN_HEADS // 2)]

    def halves(x):
        return x[:, :x.shape[1] // 2], x[:, x.shape[1] // 2:]

    def rhs1(t):
        z = jnp.zeros_like(t)
        return jnp.concatenate([jnp.concatenate([t, z], axis=1), jnp.concatenate([z, t], axis=1)], axis=0)

    kq_p, kqf, vkf, gccol, gcrow, bcol, gl, kT, decay = ({} for _ in range(9))
    for u in units:
        d, j = u
        sl = slice(2 * j * LANES, (2 * j + 2) * LANES)
        kq_p[u] = kq_refs[d][:, sl]
        kqf[u] = kq_p[u].astype(F32)
        vkf[u] = vk_refs[d][:, sl].astype(F32)
        gccol[u], gcrow[u], bcol[u], gl[u], kT[u], decay[u] = [], [], [], [], [], []
        for hh in range(2):
            h = 2 * j + hh
            gccol[u].append(jnp.broadcast_to(gc_all[d][:, h:h + 1], (C, LANES)))
            bcol[u].append(jnp.broadcast_to(beta_all[d][:, DN_HEADS + h:DN_HEADS + h + 1], (C, LANES)))
            gl[u].append(gl_all[d][:, h:h + 1])
            gcrow[u].append(gccol[u][hh].T)
            kT[u].append(kqf[u][:, hh * LANES:(hh + 1) * LANES].T[0:H, :])
            decay[u].append(jnp.where(incl[d], jnp.exp(jnp.minimum(gccol[u][hh] - gcrow[u][hh], 0.0)), 0.0))

    p1 = {u: _mm(kq_p[u], _bd2(rhs1(kT[u][0]), rhs1(kT[u][1]))) for u in units}
    a_p, qkd, x_p = {}, {}, {}
    for u in units:
        d = u[0]
        a_h, qk_h = [], []
        for hh in range(2):
            kk = p1[u][:, (2 * hh) * C:(2 * hh + 1) * C]
            qk = p1[u][:, (2 * hh + 1) * C:(2 * hh + 2) * C]
            a_h.append(jnp.where(strict[d], kk * decay[u][hh], 0.0) * bcol[u][hh])
            qk_h.append(qk * decay[u][hh])
        a_p[u] = jnp.concatenate(a_h, axis=1)
        qkd[u] = jnp.concatenate(qk_h, axis=1)
        x_p[u] = eye_p - jnp.where(blk2, a_p[u], 0.0)

    for msk in levels:
        g_p = {u: _mm(jnp.where(msk, a_p[u], 0.0), _bd2(*halves(x_p[u]))) for u in units}
        x_p = {u: x_p[u] - _mm(x_p[u], _bd2(*halves(g_p[u]))) for u in units}

    uw = {}
    for u in units:
        rhs2 = [vkf[u][:, hh * LANES:(hh + 1) * LANES] * bcol[u][hh]
                * jnp.where(lo_half, 1.0, jnp.exp(gccol[u][hh])) for hh in range(2)]
        uw[u] = _mm(x_p[u], _bd2(rhs2[0], rhs2[1]))
    ol = {u: _mm(qkd[u], _bd2(*halves(uw[u]))) for u in units}
    nw = {}
    for u in units:
        kdT = jnp.concatenate([kT[u][hh] * jnp.exp(gl[u][hh] - gcrow[u][hh][0:H, :]) for hh in range(2)], axis=0)
        nw[u] = _mm(kdT, uw[u])
    for u in units:
        d, j = u
        sl = slice(2 * j * LANES, (2 * j + 2) * LANES)
        zmult = jnp.concatenate(
            [jnp.where(lo_half, jnp.exp(gl[u][hh] - gccol[u][hh]), jnp.exp(gccol[u][hh])) for hh in range(2)], axis=1)
        z_p = kqf[u] * zmult - ol[u]
        nw_h = (nw[u][0:H, 0:LANES], nw[u][H:2 * H, LANES:2 * LANES])
        s_old = [s_ref[d, 2 * j + hh] for hh in range(2)]
        sblk = [jnp.concatenate([jnp.zeros((H, LANES), F32), s_old[hh]], axis=0) for hh in range(2)]
        r = _mm(jnp.concatenate([_bd2(*nw_h), z_p], axis=0), _bd2(sblk[0], sblk[1]))
        o_refs[d][:, sl] = jnp.where(lane_p < H, r[2 * H:, :] + ol[u], 0.0)
        wks = (r[0:H, 0:LANES], r[H:2 * H, LANES:2 * LANES])
        for hh in range(2):
            s_new = jnp.exp(gl[u][hh]) * s_old[hh] + nw_h[hh] - wks[hh]
            s_ref[d, 2 * j + hh] = jnp.where(lane_s < H, s_new, 0.0)


def _delta_rule(kq, vk, dab, ea, bias, batch, seq):
    m = kq.shape[0]
    C = DN_CHUNK
    nc = seq // C
    width = kq.shape[1]
    fwd = lambda b, n: (b * nc + n, 0)
    bwd = lambda b, n: (b * nc + nc - 1 - n, 0)
    return pl.pallas_call(
        _delta_kernel, name="delta",
        grid=(batch, nc),
        in_specs=[
            pl.BlockSpec((C, width), fwd), pl.BlockSpec((C, width), bwd),
            pl.BlockSpec((C, width), fwd), pl.BlockSpec((C, width), bwd),
            pl.BlockSpec((1, C, 2 * DN_HEADS), lambda b, n: (0, b * nc + n, 0)),
            pl.BlockSpec((1, C, 2 * DN_HEADS), lambda b, n: (1, b * nc + nc - 1 - n, 0)),
            _full(ea.shape), _full(bias.shape),
        ],
        out_specs=(pl.BlockSpec((C, width), fwd), pl.BlockSpec((C, width), bwd)),
        out_shape=(jax.ShapeDtypeStruct((m, width), F32), jax.ShapeDtypeStruct((m, width), F32)),
        scratch_shapes=[pltpu.VMEM((2, DN_HEADS, DN_DIM, LANES), F32)],
        compiler_params=_cparams(("parallel", "arbitrary")),
    )(kq, kq, vk, vk, dab, dab, ea, bias)


def _merge_kernel(ot_ref, of_ref, ob_ref, dz_ref, sg_ref, x_ref, wa_ref, wb_ref, wo_ref, bdo_ref, onw_ref, n2_ref,
                  wr_hi_ref, wr_lo_ref, x1_ref, xn_ref, lg_ref):
    tm = x_ref.shape[0]
    att = ot_ref[...].reshape(ATT_Q_W, tm).astype(F32).T
    ya = _mm(att, wa_ref[...])
    o = of_ref[...] + ob_ref[...]
    osq = o * o
    bdo = bdo_ref[...]
    w = bdo.shape[0]
    ss = jnp.concatenate([_mm(osq[:, c * w:(c + 1) * w], bdo) for c in range(o.shape[1] // w)], axis=1)
    dz = dz_ref[...].astype(F32)
    dn = o * lax.rsqrt(ss * (1.0 / DN_DIM) + EPS) * onw_ref[...] * (dz * jax.nn.sigmoid(dz))
    yb = _mm(dn, wb_ref[...])
    sg = sg_ref[...].astype(F32)
    mix = sg[:, :D_MODEL] * ya + sg[:, D_MODEL:] * yb
    x1 = x_ref[...] + _mm(mix, wo_ref[...])
    x1_ref[...] = x1
    ms = jnp.mean(x1 * x1, axis=-1, keepdims=True)
    xn = x1 * lax.rsqrt(ms + EPS) * n2_ref[...]
    xn_hi = xn.astype(BF16)
    xn_lo = (xn - xn_hi.astype(F32)).astype(BF16)
    xn_ref[...] = xn_hi
    whi = wr_hi_ref[...]
    lg_ref[...] = (jnp.dot(xn_hi, whi, preferred_element_type=F32)
                   + jnp.dot(xn_lo, whi, preferred_element_type=F32)
                   + jnp.dot(xn_hi, wr_lo_ref[...], preferred_element_type=F32))


def _merge(ot, o_f, o_b, dz, sg, x2, wa, wb, wo, bdo, onw, n2, wr_hi, wr_lo):
    m = x2.shape[0]
    tm = 256
    row = lambda i: (i, 0)
    return pl.pallas_call(
        _merge_kernel, name="merge", grid=(m // tm,),
        in_specs=[
            pl.BlockSpec((ATT_HEADS, HEAD_DIM, tm), lambda i: (0, 0, i)),
            pl.BlockSpec((tm, o_f.shape[1]), row),
            pl.BlockSpec((tm, o_b.shape[1]), row),
            pl.BlockSpec((tm, dz.shape[1]), row),
            pl.BlockSpec((tm, sg.shape[1]), row),
            pl.BlockSpec((tm, D_MODEL), row),
            _full(wa.shape), _full(wb.shape), _full(wo.shape), _full(bdo.shape), _full(onw.shape),
            _full(n2.shape), _full(wr_hi.shape), _full(wr_lo.shape),
        ],
        out_specs=(pl.BlockSpec((tm, D_MODEL), row), pl.BlockSpec((tm, D_MODEL), row),
                   pl.BlockSpec((tm, LANES), row)),
        out_shape=(jax.ShapeDtypeStruct((m, D_MODEL), F32), jax.ShapeDtypeStruct((m, D_MODEL), BF16),
                   jax.ShapeDtypeStruct((m, LANES), F32)),
        compiler_params=_cparams(("parallel",)),
    )(ot, o_f, o_b, dz, sg, x2, wa, wb, wo, bdo, onw, n2, wr_hi, wr_lo)


def _route_kernel(lg_ref, gate_ref):
    lg = lg_ref[...]
    lane_i = lax.broadcasted_iota(jnp.int32, lg.shape, 1)
    lane = lane_i.astype(F32)
    group_of = ((lane_i - ROUTER_OFF) // EXPERTS_PER_GROUP).astype(F32)
    neg = -jnp.inf

    def first_argmax(v):
        mx = jnp.max(v, axis=-1, keepdims=True)
        idx = jnp.min(jnp.where(v == mx, lane, float(LANES)), axis=-1, keepdims=True)
        return mx, idx

    gl = jnp.where(lane_i < N_GROUPS, lg, neg)
    gmax, gidx = first_argmax(gl)
    gval = 1.0 / jnp.sum(jnp.exp(gl - gmax), axis=-1, keepdims=True)
    is_exp = (lane_i >= ROUTER_OFF) & (lane_i < ROUTER_OFF + N_EXPERTS)
    sel = is_exp & (group_of == gidx)
    el = jnp.where(sel, lg, neg)
    m1, i1 = first_argmax(el)
    el2 = jnp.where(lane == i1, neg, el)
    m2, i2 = first_argmax(el2)
    r = jnp.exp(m2 - m1)
    w1 = gval / (1.0 + r)
    w2 = gval * r / (1.0 + r)
    gate_ref[...] = jnp.where(lane == i1, w1, 0.0) + jnp.where(lane == i2, w2, 0.0)


def _route(logits):
    m = logits.shape[0]
    tm = 1024 if m % 1024 == 0 else 256
    spec = pl.BlockSpec((tm, LANES), lambda i: (i, 0))
    return pl.pallas_call(
        _route_kernel, name="route", grid=(m // tm,), in_specs=[spec], out_specs=spec,
        out_shape=jax.ShapeDtypeStruct((m, LANES), F32),
        compiler_params=_cparams(("parallel",)),
    )(logits)


def _moe_kernel(xn_ref, x1_ref, gate_ref, wg_ref, wu_ref, wd_ref, fw_ref, y_ref, acc_ref):
    e = pl.program_id(1)

    @pl.when(e == 0)
    def _():
        acc_ref[...] = jnp.zeros_like(acc_ref)

    xn = xn_ref[...]
    hg = jnp.dot(xn, wg_ref[0], preferred_element_type=F32)
    hu = jnp.dot(xn, wu_ref[0], preferred_element_type=F32)
    gcol = pltpu.roll(gate_ref[...], LANES - ROUTER_OFF - e, 1)[:, 0:1]
    hid = hg * jax.nn.sigmoid(hg) * hu
    acc_ref[...] += gcol * _mm(hid, wd_ref[0])

    @pl.when(e == N_EXPERTS - 1)
    def _():
        x = x1_ref[...] + acc_ref[...]
        ms = jnp.mean(x * x, axis=-1, keepdims=True)
        y_ref[...] = x * lax.rsqrt(ms + EPS) * fw_ref[...]


def _moe_final(xn, x1, gates, wg, wu, wd, fw):
    m = xn.shape[0]
    tm = 1024 if m % 1024 == 0 else 256
    row = lambda i, e: (i, 0)
    return pl.pallas_call(
        _moe_kernel, name="moe", grid=(m // tm, N_EXPERTS),
        in_specs=[
            pl.BlockSpec((tm, D_MODEL), row), pl.BlockSpec((tm, D_MODEL), row), pl.BlockSpec((tm, LANES), row),
            pl.BlockSpec((1, D_MODEL, EXPERT_FF), lambda i, e: (e, 0, 0)),
            pl.BlockSpec((1, D_MODEL, EXPERT_FF), lambda i, e: (e, 0, 0)),
            pl.BlockSpec((1, EXPERT_FF, D_MODEL), lambda i, e: (e, 0, 0)),
            pl.BlockSpec((1, D_MODEL), lambda i, e: (0, 0)),
        ],
        out_specs=pl.BlockSpec((tm, D_MODEL), row),
        out_shape=jax.ShapeDtypeStruct((m, D_MODEL), F32),
        scratch_shapes=[pltpu.VMEM((tm, D_MODEL), F32)],
        compiler_params=_cparams(("parallel", "arbitrary")),
    )(xn, x1, gates, wg, wu, wd, fw)


def _block_ones(n, blk):
    idx = np.arange(n)
    return jnp.asarray((idx[:, None] // blk == idx[None, :] // blk), dtype=BF16)


def _prepare(norm1_w, w_in, att_q_norm, att_k_norm, dn_conv_w, dn_a_log, dn_dt_bias, dn_out_norm,
             w_branch_att, w_branch_dn, w_out, norm2_w, moe_group_router, moe_expert_router,
             moe_w_gate, moe_w_up, moe_w_down, final_norm_w):
    w_in = w_in[0]
    o_q, o_k, o_v = 0, ATT_Q_W, ATT_Q_W + ATT_KV_W
    o_dq = o_v + ATT_KV_W
    o_dk, o_dv, o_dz = o_dq + DN_W, o_dq + 2 * DN_W, o_dq + 3 * DN_W
    o_da = o_dz + DN_W
    o_db = o_da + 2 * DN_HEADS
    o_ga = o_db + 2 * DN_HEADS
    o_gb = o_ga + D_MODEL

    deint = np.concatenate([np.arange(0, HEAD_DIM, 2), np.arange(1, HEAD_DIM, 2)])
    q_cols = np.concatenate([o_q + h * HEAD_DIM + deint for h in range(ATT_HEADS)])
    k_cols = np.concatenate([o_k + h * HEAD_DIM + deint for h in range(ATT_KV_HEADS)])
    v_cols = np.arange(o_v, o_v + ATT_KV_W)
    watt = w_in[:, np.concatenate([q_cols, k_cols, v_cols])].astype(BF16)
    qnw = jnp.tile(att_q_norm[0][deint], ATT_HEADS)[None, :]
    knw = jnp.tile(att_k_norm[0][deint], ATT_KV_HEADS)[None, :]

    hd = np.arange(DN_DIM)
    kq_cols = np.concatenate([np.concatenate([o_dk + h * DN_DIM + hd, o_dq + h * DN_DIM + hd]) for h in range(DN_HEADS)])
    vk_cols = np.concatenate([np.concatenate([o_dv + h * DN_DIM + hd, o_dk + h * DN_DIM + hd]) for h in range(DN_HEADS)])
    dn_cols = np.concatenate([kq_cols, vk_cols])
    wdn = w_in[:, dn_cols].astype(BF16)
    cw = jnp.concatenate([dn_conv_w[0][:, dn_cols - o_dq], jnp.zeros((8 - DN_CONV, dn_cols.size), F32)], axis=0)
    is_q = np.concatenate([np.tile(np.concatenate([np.zeros(DN_DIM), np.ones(DN_DIM)]), DN_HEADS), np.zeros(2 * DN_W)])
    is_v = np.concatenate([np.zeros(2 * DN_W), np.tile(np.concatenate([np.ones(DN_DIM), np.zeros(DN_DIM)]), DN_HEADS)])
    donorm = jnp.asarray(1.0 - is_v, F32)[None, :]
    scale = jnp.asarray(np.where(is_q > 0, DN_DIM ** -0.5, 1.0), F32)[None, :]

    pad_cols = np.concatenate([np.concatenate([h * DN_DIM + hd, np.full(DN_DIM, -1)]) for h in range(DN_HEADS)])
    valid = jnp.asarray(pad_cols >= 0)
    take = np.maximum(pad_cols, 0)
    wdz = jnp.where(valid[None, :], w_in[:, o_dz + take], 0.0).astype(BF16)
    onw = jnp.where(valid, jnp.tile(dn_out_norm[0], 2 * DN_HEADS), 0.0)[None, :]
    wb = jnp.where(valid[:, None], w_branch_dn[0][take, :], 0.0).astype(BF16)

    wab = jnp.zeros((D_MODEL, 2 * LANES), F32)
    for dirn in range(2):
        wab = wab.at[:, dirn * LANES:dirn * LANES + DN_HEADS].set(
            w_in[:, o_da + dirn * DN_HEADS:o_da + (dirn + 1) * DN_HEADS])
        wab = wab.at[:, dirn * LANES + DN_HEADS:dirn * LANES + 2 * DN_HEADS].set(
            w_in[:, o_db + dirn * DN_HEADS:o_db + (dirn + 1) * DN_HEADS])
    wab = wab.astype(BF16)
    wg = w_in[:, o_ga:o_gb + D_MODEL].astype(BF16)

    zeros8 = jnp.zeros((2, DN_HEADS), F32)
    ea = jnp.concatenate([jnp.exp(dn_a_log[0]), zeros8], axis=1)[:, None, :]
    bias = jnp.concatenate([dn_dt_bias[0], zeros8], axis=1)[:, None, :]

    idx = np.arange(2 * LANES)
    ok = (idx % LANES) < DN_DIM
    bdo = jnp.asarray((idx[:, None] // LANES == idx[None, :] // LANES) & ok[:, None] & ok[None, :], dtype=BF16)

    wr = jnp.concatenate([moe_group_router[0], moe_expert_router[0],
                          jnp.zeros((D_MODEL, LANES - N_GROUPS - N_EXPERTS), F32)], axis=1)
    wr_hi = wr.astype(BF16)
    wr_lo = (wr - wr_hi.astype(F32)).astype(BF16)

    return dict(
        n1=norm1_w[0][None, :], watt=watt, wdn=wdn, wdz=wdz, wab=wab, wg=wg,
        bd_att=_block_ones(ATT_Q_W, HEAD_DIM), qnw=qnw, knw=knw,
        cw=cw, bd_dn=_block_ones(2 * LANES, DN_DIM), donorm=donorm, scale=scale, ea=ea, bias=bias,
        wa=w_branch_att[0].astype(BF16), wb=wb, wo=w_out[0].astype(BF16), bdo=bdo, onw=onw,
        n2=norm2_w[0][None, :], wr_hi=wr_hi, wr_lo=wr_lo,
        wge=moe_w_gate[0].astype(BF16), wue=moe_w_up[0].astype(BF16), wde=moe_w_down[0].astype(BF16),
        fw=final_norm_w[None, :],
    )


def _rope_tables(seq):
    t = np.arange(seq)
    axis_dim = HEAD_DIM // 2
    inv = ROPE_THETA ** (-np.arange(0, axis_dim, 2, dtype=np.float32) / axis_dim)
    r = (t // GRID_W).astype(np.float32)
    c = (t % GRID_W).astype(np.float32)
    ang = np.concatenate([r[:, None] * inv, c[:, None] * inv], axis=-1).astype(np.float32)
    ang = jnp.asarray(ang)
    cos, sin = jnp.cos(ang), jnp.sin(ang)
    cos_t = jnp.tile(jnp.concatenate([cos, cos], axis=1), (1, LANES // HEAD_DIM))
    sin_t = jnp.tile(jnp.concatenate([-sin, sin], axis=1), (1, LANES // HEAD_DIM))
    return cos_t, sin_t


def _trunk(x, p):
    batch, seq, _ = x.shape
    x2 = x.reshape(batch * seq, D_MODEL)
    cos_t, sin_t = _rope_tables(seq)
    qt, kn, vt, dnpre, dz, dab, sg, kmx_tiles = _inproj(
        x2, seq, p["n1"], p["watt"], p["wdn"], p["wdz"], p["wab"], p["wg"], p["bd_att"], p["qnw"], p["knw"],
        cos_t, sin_t)
    kmx = jnp.max(kmx_tiles.reshape(batch, -1, 8, ATT_KV_HEADS, HEAD_DIM), axis=(1, 2, 4))
    kmx = jnp.broadcast_to(kmx[:, :, None, None], (batch, ATT_KV_HEADS, 8, LANES))
    ot = _attention(qt, kn, vt, kmx, batch, seq)
    kq, vk = _dnprep(dnpre, seq, p["cw"], p["bd_dn"], p["donorm"], p["scale"])
    o_f, o_b = _delta_rule(kq, vk, dab, p["ea"], p["bias"], batch, seq)
    x1, xn, logits = _merge(ot, o_f, o_b, dz, sg, x2, p["wa"], p["wb"], p["wo"], p["bdo"], p["onw"], p["n2"],
                            p["wr_hi"], p["wr_lo"])
    gates = _route(logits)
    y = _moe_final(xn, x1, gates, p["wge"], p["wue"], p["wde"], p["fw"])
    return y.reshape(batch, seq, D_MODEL)


def kernel(x_prompt, x_sample, norm1_w, w_in, att_q_norm, att_k_norm, dn_conv_w, dn_a_log, dn_dt_bias, dn_out_norm, w_branch_att, w_branch_dn, w_out, norm2_w, moe_group_router, moe_expert_router, moe_w_gate, moe_w_up, moe_w_down, final_norm_w):
    p = _prepare(norm1_w, w_in, att_q_norm, att_k_norm, dn_conv_w, dn_a_log, dn_dt_bias, dn_out_norm,
                 w_branch_att, w_branch_dn, w_out, norm2_w, moe_group_router, moe_expert_router,
                 moe_w_gate, moe_w_up, moe_w_down, final_norm_w)
    return (_trunk(x_prompt, p), _trunk(x_sample, p))
```

```python
import functools
import math

import numpy as np
import jax
import jax.numpy as jnp
from jax import lax
from jax.experimental import pallas as pl
from jax.experimental.pallas import tpu as pltpu
from jax.experimental.pallas import tpu_sc as plsc

F32 = jnp.float32
BF16 = jnp.bfloat16

D_MODEL = 1024
GRID_W = 64
EPS = 1e-6
ATT_HEADS = 8
ATT_KV_HEADS = 2
ATT_GROUP = ATT_HEADS // ATT_KV_HEADS
HEAD_DIM = 64
ROPE_THETA = 10000.0
DN_HEADS = 8
DN_DIM = 64
DN_CONV = 5
N_GROUPS = 4
EXPERTS_PER_GROUP = 8
N_EXPERTS = N_GROUPS * EXPERTS_PER_GROUP
EXPERT_FF = 256

ATT_Q_W = ATT_HEADS * HEAD_DIM
ATT_KV_W = ATT_KV_HEADS * HEAD_DIM
DN_W = DN_HEADS * DN_DIM
LANES = 128
DN_CHUNK = 128
ROUTER_OFF = N_GROUPS
PACK_W = D_MODEL // 4
SC_WINDOW = 128
MOE_TILE = 256
VMEM_LIMIT = 52 * 1024 * 1024
LOG2E = math.log2(math.e)
SCORE_BOUND_MAX = 50.0
SCORE_BOUND_SLACK = 1.05


def _mm(a, b):
    return jnp.dot(a.astype(BF16), b.astype(BF16), preferred_element_type=F32)


def _cparams(sem):
    return pltpu.CompilerParams(dimension_semantics=sem, vmem_limit_bytes=VMEM_LIMIT)


def _full(shape):
    nd = len(shape)
    return pl.BlockSpec(shape, lambda *_: (0,) * nd)


def _rope(x, cos, sin_signed):
    n = x.shape[1]
    lane = lax.broadcasted_iota(jnp.int32, x.shape, 1)
    first = (lane % HEAD_DIM) < (HEAD_DIM // 2)
    partner = jnp.where(first, pltpu.roll(x, n - HEAD_DIM // 2, 1), pltpu.roll(x, HEAD_DIM // 2, 1))
    return x * cos + partner * sin_signed


def _inproj_kernel(x_ref, n1_ref, watt_ref, wdn_ref, wdz_ref, wab_ref, wg_ref, bd_ref, qnw_ref, knw_ref,
                   cos_ref, sin_ref, qt_ref, k_ref, vt_ref, dn_ref, dz_ref, dab_ref, sg_ref, kmx_ref):
    tm = x_ref.shape[0]
    x = x_ref[...]
    ms = jnp.mean(x * x, axis=-1, keepdims=True)
    xn = (x * lax.rsqrt(ms + EPS) * n1_ref[...]).astype(BF16)

    att = jnp.dot(xn, watt_ref[...], preferred_element_type=F32)
    aq = att[:, :ATT_Q_W]
    ak = att[:, ATT_Q_W:ATT_Q_W + ATT_KV_W]
    av = att[:, ATT_Q_W + ATT_KV_W:]
    bd = bd_ref[...]
    qss = _mm(aq * aq, bd)
    kss = _mm(ak * ak, bd[:ATT_KV_W, :ATT_KV_W])
    cos = cos_ref[...]
    sin = sin_ref[...]
    cos4 = jnp.concatenate([cos] * (ATT_Q_W // LANES), axis=1)
    sin4 = jnp.concatenate([sin] * (ATT_Q_W // LANES), axis=1)
    q = aq * lax.rsqrt(qss * (1.0 / HEAD_DIM) + EPS) * qnw_ref[...]
    k = ak * lax.rsqrt(kss * (1.0 / HEAD_DIM) + EPS) * knw_ref[...]
    q = _rope(q, cos4, sin4) * (HEAD_DIM ** -0.5 * LOG2E)
    kmx_ref[0] = jnp.broadcast_to(jnp.max(_mm(k * k, bd[:ATT_KV_W, :ATT_KV_W]), axis=0, keepdims=True),
                                  (8, ATT_KV_W))
    k = _rope(k, cos, sin)
    qt_ref[...] = q.T.reshape(ATT_HEADS, HEAD_DIM, tm).astype(BF16)
    k_ref[0] = k[:, :HEAD_DIM].astype(BF16)
    k_ref[1] = k[:, HEAD_DIM:].astype(BF16)
    vt_ref[...] = av.T.reshape(ATT_KV_HEADS, HEAD_DIM, tm).astype(BF16)

    dn_ref[...] = jnp.dot(xn, wdn_ref[...], preferred_element_type=F32).astype(BF16)
    dz_ref[...] = jnp.dot(xn, wdz_ref[...], preferred_element_type=F32).astype(BF16)
    ab = jnp.dot(xn, wab_ref[...], preferred_element_type=F32)
    dab_ref[0] = ab[:, 0:2 * DN_HEADS]
    dab_ref[1] = ab[:, LANES:LANES + 2 * DN_HEADS]
    sg_ref[...] = jax.nn.sigmoid(jnp.dot(xn, wg_ref[...], preferred_element_type=F32)).astype(BF16)


def _inproj(x2, seq, n1, watt, wdn, wdz, wab, wg, bd, qnw, knw, cos_t, sin_t):
    m = x2.shape[0]
    tm = min(256, seq)
    per_seq = seq // tm
    row = lambda i: (i, 0)
    out_shape = (
        jax.ShapeDtypeStruct((ATT_HEADS, HEAD_DIM, m), BF16),
        jax.ShapeDtypeStruct((ATT_KV_HEADS, m, HEAD_DIM), BF16),
        jax.ShapeDtypeStruct((ATT_KV_HEADS, HEAD_DIM, m), BF16),
        jax.ShapeDtypeStruct((m, 4 * DN_W), BF16),
        jax.ShapeDtypeStruct((m, 2 * DN_W), BF16),
        jax.ShapeDtypeStruct((2, m, 2 * DN_HEADS), F32),
        jax.ShapeDtypeStruct((m, 2 * D_MODEL), BF16),
        jax.ShapeDtypeStruct((m // tm, 8, ATT_KV_W), F32),
    )
    in_specs = [
        pl.BlockSpec((tm, D_MODEL), row),
        _full(n1.shape), _full(watt.shape), _full(wdn.shape), _full(wdz.shape), _full(wab.shape),
        _full(wg.shape), _full(bd.shape), _full(qnw.shape), _full(knw.shape),
        pl.BlockSpec((tm, LANES), lambda i: (i % per_seq, 0)),
        pl.BlockSpec((tm, LANES), lambda i: (i % per_seq, 0)),
    ]
    out_specs = (
        pl.BlockSpec((ATT_HEADS, HEAD_DIM, tm), lambda i: (0, 0, i)),
        pl.BlockSpec((ATT_KV_HEADS, tm, HEAD_DIM), lambda i: (0, i, 0)),
        pl.BlockSpec((ATT_KV_HEADS, HEAD_DIM, tm), lambda i: (0, 0, i)),
        pl.BlockSpec((tm, 4 * DN_W), row),
        pl.BlockSpec((tm, 2 * DN_W), row),
        pl.BlockSpec((2, tm, 2 * DN_HEADS), lambda i: (0, i, 0)),
        pl.BlockSpec((tm, 2 * D_MODEL), row),
        pl.BlockSpec((1, 8, ATT_KV_W), lambda i: (i, 0, 0)),
    )
    return pl.pallas_call(
        _inproj_kernel, name="inproj", grid=(m // tm,), in_specs=in_specs, out_specs=out_specs, out_shape=out_shape,
        compiler_params=_cparams(("parallel",)),
    )(x2, n1, watt, wdn, wdz, wab, wg, bd, qnw, knw, cos_t, sin_t)


def _attn_kernel(qt_ref, k_ref, vt_ref, kmx_ref, ot_ref, *, tk, nk):
    tq = qt_ref.shape[2]
    q = jnp.concatenate([qt_ref[r] for r in range(ATT_GROUP)], axis=1)
    n = q.shape[1]
    qf = q.astype(F32)
    bound = jnp.sqrt(jnp.sum(qf * qf, axis=0, keepdims=True) * kmx_ref[0, 0][0:1, 0:1]) * SCORE_BOUND_SLACK
    small = jnp.max(bound) <= SCORE_BOUND_MAX
    ones_rows = 16

    def blocks(kb):
        off = pl.multiple_of(kb * tk, tk)
        return k_ref[0, pl.ds(off, tk), :], vt_ref[0, :, pl.ds(off, tk)]

    def write(out):
        for r in range(ATT_GROUP):
            ot_ref[r] = out[:, r * tq:(r + 1) * tq].astype(BF16)

    @pl.when(small)
    def _():
        def body(kb, acc):
            kblk, vblk = blocks(kb)
            s = jnp.dot(kblk, q, preferred_element_type=F32)
            p = jnp.exp2(s - bound).astype(BF16)
            vaug = jnp.concatenate([vblk, jnp.ones((ones_rows, tk), BF16)], axis=0)
            return acc + jnp.dot(vaug, p, preferred_element_type=F32)

        acc = lax.fori_loop(0, nk, body, jnp.zeros((HEAD_DIM + ones_rows, n), F32))
        write(acc[0:HEAD_DIM] / acc[HEAD_DIM:HEAD_DIM + 1])

    @pl.when(jnp.logical_not(small))
    def _():
        def body(kb, carry):
            m_run, l_run, acc = carry
            kblk, vblk = blocks(kb)
            s = jnp.dot(kblk, q, preferred_element_type=F32)
            m_new = jnp.maximum(m_run, jnp.max(s, axis=0, keepdims=True))
            p = jnp.exp2(s - m_new)
            alpha = jnp.exp2(m_run - m_new)
            l_new = alpha * l_run + jnp.sum(p, axis=0, keepdims=True)
            acc_new = alpha * acc + jnp.dot(vblk, p.astype(BF16), preferred_element_type=F32)
            return m_new, l_new, acc_new

        init = (jnp.full((1, n), -jnp.inf, F32), jnp.zeros((1, n), F32), jnp.zeros((HEAD_DIM, n), F32))
        _, l_fin, acc = lax.fori_loop(0, nk, body, init)
        write(acc / l_fin)


def _attention(qt, kn, vt, kmx, batch, seq):
    m = qt.shape[2]
    tq = min(512, seq)
    tk = min(512, seq)
    nq = seq // tq
    qspec = pl.BlockSpec((ATT_GROUP, HEAD_DIM, tq), lambda b, g, i: (g, 0, b * nq + i))
    return pl.pallas_call(
        functools.partial(_attn_kernel, tk=tk, nk=seq // tk), name="attention",
        grid=(batch, ATT_KV_HEADS, nq),
        in_specs=[
            qspec,
            pl.BlockSpec((1, seq, HEAD_DIM), lambda b, g, i: (g, b, 0)),
            pl.BlockSpec((1, HEAD_DIM, seq), lambda b, g, i: (g, 0, b)),
            pl.BlockSpec((1, 1, 8, LANES), lambda b, g, i: (b, g, 0, 0)),
        ],
        out_specs=qspec,
        out_shape=jax.ShapeDtypeStruct((ATT_HEADS, HEAD_DIM, m), BF16),
        compiler_params=_cparams(("parallel", "parallel", "parallel")),
    )(qt, kn, vt, kmx)


def _dnprep_kernel(cur_ref, prev_ref, next_ref, cw_ref, bd_ref, donorm_ref, scale_ref, kq_ref, vk_ref, buf_ref,
                   *, seq):
    tm = cur_ref.shape[0]
    halo = prev_ref.shape[0]
    i = pl.program_id(0)
    pos = (i * tm) % seq
    has_prev = pos != 0
    has_next = pos + tm != seq
    buf_ref[0:halo, :] = jnp.where(has_prev, prev_ref[...].astype(F32), 0.0)
    buf_ref[halo:halo + tm, :] = cur_ref[...].astype(F32)
    buf_ref[halo + tm:, :] = jnp.where(has_next, next_ref[...].astype(F32), 0.0)
    cw = cw_ref[...]
    pad = DN_CONV // 2
    y = None
    for j in range(DN_CONV):
        term = buf_ref[halo - pad + j:halo - pad + j + tm, :] * cw[j:j + 1, :]
        y = term if y is None else y + term
    y = y * jax.nn.sigmoid(y)
    bd = bd_ref[...]
    w = bd.shape[0]
    ysq = y * y
    ss = jnp.concatenate([_mm(ysq[:, c * w:(c + 1) * w], bd) for c in range(y.shape[1] // w)], axis=1)
    factor = jnp.where(donorm_ref[...] > 0.5, lax.rsqrt(ss + EPS), 1.0) * scale_ref[...]
    y = y * factor
    half = y.shape[1] // 2
    kq_ref[...] = y[:, :half].astype(BF16)
    vk_ref[...] = y[:, half:].astype(BF16)


def _dnprep(dnpre, seq, cw, bd, donorm, scale):
    m, width = dnpre.shape
    tm = min(512, seq)
    halo = 16
    hb = tm // halo
    last = m // halo - 1
    return pl.pallas_call(
        functools.partial(_dnprep_kernel, seq=seq), name="dnprep",
        grid=(m // tm,),
        in_specs=[
            pl.BlockSpec((tm, width), lambda i: (i, 0)),
            pl.BlockSpec((halo, width), lambda i: (jnp.maximum(i * hb - 1, 0), 0)),
            pl.BlockSpec((halo, width), lambda i: (jnp.minimum((i + 1) * hb, last), 0)),
            _full(cw.shape), _full(bd.shape), _full(donorm.shape), _full(scale.shape),
        ],
        out_specs=(pl.BlockSpec((tm, width // 2), lambda i: (i, 0)),
                   pl.BlockSpec((tm, width // 2), lambda i: (i, 0))),
        out_shape=(jax.ShapeDtypeStruct((m, width // 2), BF16), jax.ShapeDtypeStruct((m, width // 2), BF16)),
        scratch_shapes=[pltpu.VMEM((tm + 2 * halo, width), F32)],
        compiler_params=_cparams(("parallel",)),
    )(dnpre, dnpre, dnpre, cw, bd, donorm, scale)


def _bd2(a, b):
    return jnp.concatenate([jnp.concatenate([a, jnp.zeros_like(b)], axis=1),
                            jnp.concatenate([jnp.zeros_like(a), b], axis=1)], axis=0)


def _softplus(x):
    return jnp.maximum(x, 0.0) + jnp.log1p(jnp.exp(-jnp.abs(x)))


def _delta_kernel(kqf_ref, kqb_ref, vkf_ref, vkb_ref, dabf_ref, dabb_ref, ea_ref, bias_ref, of_ref, ob_ref, s_ref):
    C = DN_CHUNK
    H = DN_DIM
    n = pl.program_id(1)

    @pl.when(n == 0)
    def _():
        s_ref[...] = jnp.zeros_like(s_ref)

    row = lax.broadcasted_iota(jnp.int32, (C, C), 0)
    col = lax.broadcasted_iota(jnp.int32, (C, C), 1)
    incl = (row >= col, row <= col)
    strict = (row > col, row < col)
    rowp = lax.broadcasted_iota(jnp.int32, (C, 2 * C), 0)
    colp = lax.broadcasted_iota(jnp.int32, (C, 2 * C), 1) % C
    lane = lax.broadcasted_iota(jnp.int32, (C, LANES), 1)
    lo_half = lane < H
    lane_s = lax.broadcasted_iota(jnp.int32, (H, LANES), 1)
    lane_p = lax.broadcasted_iota(jnp.int32, (C, 2 * LANES), 1) % LANES
    eye_p = jnp.where(rowp == colp, 1.0, 0.0)
    blk2 = rowp // 2 == colp // 2
    levels = []
    b = 2
    while b < C:
        levels.append((rowp // (2 * b) == colp // (2 * b)) & (rowp // b != colp // b))
        b *= 2

    kq_refs = (kqf_ref, kqb_ref)
    vk_refs = (vkf_ref, vkb_ref)
    dab_refs = (dabf_ref, dabb_ref)
    o_refs = (of_ref, ob_ref)

    gc_all, gl_all, beta_all = [], [], []
    for d in range(2):
        ab = dab_refs[d][0]
        g = -ea_ref[d] * _softplus(ab + bias_ref[d])
        beta_all.append(jax.nn.sigmoid(ab))
        tri = jnp.where(incl[d], 1.0, 0.0).astype(BF16)
        g_hi = g.astype(BF16)
        g_lo = (g - g_hi.astype(F32)).astype(BF16)
        gc_all.append(jnp.dot(tri, g_hi, preferred_element_type=F32)
                      + jnp.dot(tri, g_lo, preferred_element_type=F32))
        gl_all.append(jnp.sum(g, axis=0, keepdims=True))

    units = [(d, j) for d in range(2) for j in range(DN_HEADS // 2)]

    def halves(x):
        return x[:, :x.shape[1] // 2], x[:, x.shape[1] // 2:]

    def rhs1(t):
        z = jnp.zeros_like(t)
        return jnp.concatenate([jnp.concatenate([t, z], axis=1), jnp.concatenate([z, t], axis=1)], axis=0)

    kq_p, kqf, vkf, gccol, gcrow, bcol, gl, kT, decay = ({} for _ in range(9))
    for u in units:
        d, j = u
        sl = slice(2 * j * LANES, (2 * j + 2) * LANES)
        kq_p[u] = kq_refs[d][:, sl]
        kqf[u] = kq_p[u].astype(F32)
        vkf[u] = vk_refs[d][:, sl].astype(F32)
        gccol[u], gcrow[u], bcol[u], gl[u], kT[u], decay[u] = [], [], [], [], [], []
        for hh in range(2):
            h = 2 * j + hh
            gccol[u].append(jnp.broadcast_to(gc_all[d][:, h:h + 1], (C, LANES)))
            bcol[u].append(jnp.broadcast_to(beta_all[d][:, DN_HEADS + h:DN_HEADS + h + 1], (C, LANES)))
            gl[u].append(gl_all[d][:, h:h + 1])
            gcrow[u].append(gccol[u][hh].T)
            kT[u].append(kqf[u][:, hh * LANES:(hh + 1) * LANES].T[0:H, :])
            decay[u].append(jnp.where(incl[d], jnp.exp(jnp.minimum(gccol[u][hh] - gcrow[u][hh], 0.0)), 0.0))

    p1 = {u: _mm(kq_p[u], _bd2(rhs1(kT[u][0]), rhs1(kT[u][1]))) for u in units}
    a_p, qkd, x_p = {}, {}, {}
    for u in units:
        d = u[0]
        a_h, qk_h = [], []
        for hh in range(2):
            kk = p1[u][:, (2 * hh) * C:(2 * hh + 1) * C]
            qk = p1[u][:, (2 * hh + 1) * C:(2 * hh + 2) * C]
            a_h.append(jnp.where(strict[d], kk * decay[u][hh], 0.0) * bcol[u][hh])
            qk_h.append(qk * decay[u][hh])
        a_p[u] = jnp.concatenate(a_h, axis=1)
        qkd[u] = jnp.concatenate(qk_h, axis=1)
        x_p[u] = eye_p - jnp.where(blk2, a_p[u], 0.0)

    for msk in levels:
        g_p = {u: _mm(jnp.where(msk, a_p[u], 0.0), _bd2(*halves(x_p[u]))) for u in units}
        x_p = {u: x_p[u] - _mm(x_p[u], _bd2(*halves(g_p[u]))) for u in units}

    uw = {}
    for u in units:
        rhs2 = [vkf[u][:, hh * LANES:(hh + 1) * LANES] * bcol[u][hh]
                * jnp.where(lo_half, 1.0, jnp.exp(gccol[u][hh])) for hh in range(2)]
        uw[u] = _mm(x_p[u], _bd2(rhs2[0], rhs2[1]))
    ol = {u: _mm(qkd[u], _bd2(*halves(uw[u]))) for u in units}
    nw = {}
    for u in units:
        kdT = jnp.concatenate([kT[u][hh] * jnp.exp(gl[u][hh] - gcrow[u][hh][0:H, :]) for hh in range(2)], axis=0)
        nw[u] = _mm(kdT, uw[u])
    for u in units:
        d, j = u
        sl = slice(2 * j * LANES, (2 * j + 2) * LANES)
        zmult = jnp.concatenate(
            [jnp.where(lo_half, jnp.exp(gl[u][hh] - gccol[u][hh]), jnp.exp(gccol[u][hh])) for hh in range(2)], axis=1)
        z_p = kqf[u] * zmult - ol[u]
        nw_h = (nw[u][0:H, 0:LANES], nw[u][H:2 * H, LANES:2 * LANES])
        s_old = [s_ref[d, 2 * j + hh] for hh in range(2)]
        sblk = [jnp.concatenate([jnp.zeros((H, LANES), F32), s_old[hh]], axis=0) for hh in range(2)]
        r = _mm(jnp.concatenate([_bd2(*nw_h), z_p], axis=0), _bd2(sblk[0], sblk[1]))
        o_refs[d][:, sl] = jnp.where(lane_p < H, r[2 * H:, :] + ol[u], 0.0)
        wks = (r[0:H, 0:LANES], r[H:2 * H, LANES:2 * LANES])
        for hh in range(2):
            s_new = jnp.exp(gl[u][hh]) * s_old[hh] + nw_h[hh] - wks[hh]
            s_ref[d, 2 * j + hh] = jnp.where(lane_s < H, s_new, 0.0)


def _delta_rule(kq, vk, dab, ea, bias, batch, seq):
    m = kq.shape[0]
    C = DN_CHUNK
    nc = seq // C
    width = kq.shape[1]
    fwd = lambda b, n: (b * nc + n, 0)
    bwd = lambda b, n: (b * nc + nc - 1 - n, 0)
    return pl.pallas_call(
        _delta_kernel, name="delta",
        grid=(batch, nc),
        in_specs=[
            pl.BlockSpec((C, width), fwd), pl.BlockSpec((C, width), bwd),
            pl.BlockSpec((C, width), fwd), pl.BlockSpec((C, width), bwd),
            pl.BlockSpec((1, C, 2 * DN_HEADS), lambda b, n: (0, b * nc + n, 0)),
            pl.BlockSpec((1, C, 2 * DN_HEADS), lambda b, n: (1, b * nc + nc - 1 - n, 0)),
            _full(ea.shape), _full(bias.shape),
        ],
        out_specs=(pl.BlockSpec((C, width), fwd), pl.BlockSpec((C, width), bwd)),
        out_shape=(jax.ShapeDtypeStruct((m, width), F32), jax.ShapeDtypeStruct((m, width), F32)),
        scratch_shapes=[pltpu.VMEM((2, DN_HEADS, DN_DIM, LANES), F32)],
        compiler_params=_cparams(("parallel", "arbitrary")),
    )(kq, kq, vk, vk, dab, dab, ea, bias)


def _merge_kernel(ot_ref, of_ref, ob_ref, dz_ref, sg_ref, x_ref, wa_ref, wb_ref, wo_ref, bdo_ref, onw_ref, n2_ref,
                  wr_hi_ref, wr_lo_ref, x1_ref, xp_ref, lg_ref):
    tm = x_ref.shape[0]
    att = ot_ref[...].reshape(ATT_Q_W, tm).astype(F32).T
    ya = _mm(att, wa_ref[...])
    o = of_ref[...] + ob_ref[...]
    osq = o * o
    bdo = bdo_ref[...]
    w = bdo.shape[0]
    ss = jnp.concatenate([_mm(osq[:, c * w:(c + 1) * w], bdo) for c in range(o.shape[1] // w)], axis=1)
    dz = dz_ref[...].astype(F32)
    dn = o * lax.rsqrt(ss * (1.0 / DN_DIM) + EPS) * onw_ref[...] * (dz * jax.nn.sigmoid(dz))
    yb = _mm(dn, wb_ref[...])
    sg = sg_ref[...].astype(F32)
    mix = sg[:, :D_MODEL] * ya + sg[:, D_MODEL:] * yb
    x1 = x_ref[...] + _mm(mix, wo_ref[...])
    x1_ref[...] = x1
    ms = jnp.mean(x1 * x1, axis=-1, keepdims=True)
    xn = x1 * lax.rsqrt(ms + EPS) * n2_ref[...]
    xn_hi = xn.astype(BF16)
    xn_lo = (xn - xn_hi.astype(F32)).astype(BF16)
    xp_ref[0], xp_ref[1] = _pack_rows(xn)
    whi = wr_hi_ref[...]
    lg_ref[...] = (jnp.dot(xn_hi, whi, preferred_element_type=F32)
                   + jnp.dot(xn_lo, whi, preferred_element_type=F32)
                   + jnp.dot(xn_hi, wr_lo_ref[...], preferred_element_type=F32))


def _merge(ot, o_f, o_b, dz, sg, x2, wa, wb, wo, bdo, onw, n2, wr_hi, wr_lo):
    m = x2.shape[0]
    tm = 256
    row = lambda i: (i, 0)
    return pl.pallas_call(
        _merge_kernel, name="merge", grid=(m // tm,),
        in_specs=[
            pl.BlockSpec((ATT_HEADS, HEAD_DIM, tm), lambda i: (0, 0, i)),
            pl.BlockSpec((tm, o_f.shape[1]), row),
            pl.BlockSpec((tm, o_b.shape[1]), row),
            pl.BlockSpec((tm, dz.shape[1]), row),
            pl.BlockSpec((tm, sg.shape[1]), row),
            pl.BlockSpec((tm, D_MODEL), row),
            _full(wa.shape), _full(wb.shape), _full(wo.shape), _full(bdo.shape), _full(onw.shape),
            _full(n2.shape), _full(wr_hi.shape), _full(wr_lo.shape),
        ],
        out_specs=(pl.BlockSpec((tm, D_MODEL), row), pl.BlockSpec((2, tm, PACK_W), lambda i: (0, i, 0)),
                   pl.BlockSpec((tm, LANES), row)),
        out_shape=(jax.ShapeDtypeStruct((m, D_MODEL), F32), jax.ShapeDtypeStruct((2, m, PACK_W), jnp.uint32),
                   jax.ShapeDtypeStruct((m, LANES), F32)),
        compiler_params=_cparams(("parallel",)),
    )(ot, o_f, o_b, dz, sg, x2, wa, wb, wo, bdo, onw, n2, wr_hi, wr_lo)


def _route_kernel(lg_ref, info_ref):
    lg = lg_ref[...]
    lane_i = lax.broadcasted_iota(jnp.int32, lg.shape, 1)
    lane = lane_i.astype(F32)
    group_of = ((lane_i - ROUTER_OFF) // EXPERTS_PER_GROUP).astype(F32)
    neg = -jnp.inf

    def first_argmax(v):
        mx = jnp.max(v, axis=-1, keepdims=True)
        idx = jnp.min(jnp.where(v == mx, lane, float(LANES)), axis=-1, keepdims=True)
        return mx, idx

    gl = jnp.where(lane_i < N_GROUPS, lg, neg)
    gmax, gidx = first_argmax(gl)
    gval = 1.0 / jnp.sum(jnp.exp(gl - gmax), axis=-1, keepdims=True)
    is_exp = (lane_i >= ROUTER_OFF) & (lane_i < ROUTER_OFF + N_EXPERTS)
    sel = is_exp & (group_of == gidx)
    el = jnp.where(sel, lg, neg)
    m1, i1 = first_argmax(el)
    el2 = jnp.where(lane == i1, neg, el)
    m2, i2 = first_argmax(el2)
    r = jnp.exp(m2 - m1)
    w1 = gval / (1.0 + r)
    w2 = gval * r / (1.0 + r)
    info_ref[...] = (jnp.where(lane_i == 0, i1 - ROUTER_OFF, 0.0) + jnp.where(lane_i == 1, i2 - ROUTER_OFF, 0.0)
                     + jnp.where(lane_i == 2, w1, 0.0) + jnp.where(lane_i == 3, w2, 0.0))


def _route(logits):
    m = logits.shape[0]
    tm = 1024 if m % 1024 == 0 else 256
    spec = pl.BlockSpec((tm, LANES), lambda i: (i, 0))
    return pl.pallas_call(
        _route_kernel, name="route", grid=(m // tm,), in_specs=[spec], out_specs=spec,
        out_shape=jax.ShapeDtypeStruct((m, LANES), F32),
        compiler_params=_cparams(("parallel",)),
    )(logits)


def _pack_rows(x):
    bits = pltpu.bitcast(x.astype(BF16).astype(F32), jnp.uint32)
    half = x.shape[1] // 2
    word = (bits[:, :half] >> 16) | (bits[:, half:] & jnp.uint32(0xFFFF0000))
    return word[:, :PACK_W], word[:, PACK_W:]


def _unpack_rows(w0, w1):
    lo = [pltpu.bitcast(w << 16, F32) for w in (w0, w1)]
    hi = [pltpu.bitcast(w & jnp.uint32(0xFFFF0000), F32) for w in (w0, w1)]
    return jnp.concatenate(lo + hi, axis=1)


def _sc_mesh():
    return plsc.VectorSubcoreMesh(core_axis_name="c", subcore_axis_name="s")


def _sc_scatter_rows(src, idx, n_out, reps):
    s, width = src.shape
    nblk = s // SC_WINDOW

    @pl.kernel(out_type=jax.ShapeDtypeStruct((n_out, width), src.dtype), mesh=_sc_mesh(), scratch_types=[])
    def scatter_kernel(x_hbm, i_hbm, o_hbm):
        def body(x_vmem, i_vmem):
            pltpu.sync_copy(x_vmem, o_hbm.at[i_vmem.at[0]])

        pltpu.emit_pipeline(
            body, grid=(reps * nblk,),
            in_specs=[pl.BlockSpec((SC_WINDOW, width), index_map=lambda i: (i % nblk, 0)),
                      pl.BlockSpec((1, SC_WINDOW), index_map=lambda i: (0, i))],
            out_specs=[], core_axis_name=("c", "s"), dimension_semantics=(pltpu.PARALLEL,),
        )(x_hbm, i_hbm)

    return scatter_kernel(src, idx.reshape(1, reps * s))


def _sc_gather_rows(table, idx):
    k = idx.shape[0]
    width = table.shape[1]

    @pl.kernel(out_type=jax.ShapeDtypeStruct((k, width), table.dtype), mesh=_sc_mesh())
    def gather_kernel(x_hbm, i_hbm, o_hbm):
        def body(i_vmem, o_vmem):
            pltpu.sync_copy(x_hbm.at[i_vmem.at[0]], o_vmem)

        pltpu.emit_pipeline(
            body, grid=(k // SC_WINDOW,),
            in_specs=[pl.BlockSpec((1, SC_WINDOW), index_map=lambda i: (0, i))],
            out_specs=[pl.BlockSpec((SC_WINDOW, width), index_map=lambda i: (i, 0))],
            core_axis_name=("c", "s"), dimension_semantics=(pltpu.PARALLEL,),
        )(i_hbm, o_hbm)

    return gather_kernel(table, idx.reshape(1, k))


def _routing_tables(info):
    m = info.shape[0]
    e_flat = info[:, 0:2].astype(jnp.int32).reshape(-1)
    onehot = (e_flat[:, None] == jnp.arange(N_EXPERTS, dtype=jnp.int32)[None, :]).astype(jnp.int32)
    csum = jnp.cumsum(onehot, axis=0)
    rank = jnp.sum(csum * onehot, axis=1) - 1
    tiles_e = (csum[-1] + MOE_TILE - 1) // MOE_TILE
    tile_end = jnp.cumsum(tiles_e)
    slot_start = (tile_end - tiles_e) * MOE_TILE
    pos = jnp.sum(onehot * slot_start[None, :], axis=1) + rank
    n_tiles = 2 * m // MOE_TILE + N_EXPERTS
    tile_expert = jnp.sum(jnp.arange(n_tiles, dtype=jnp.int32)[:, None] >= tile_end[None, :], axis=1)
    tile_expert = jnp.minimum(tile_expert, N_EXPERTS - 1).astype(jnp.int32)
    return pos.reshape(m, 2), tile_expert, tile_end[-1:].astype(jnp.int32)


def _experts_kernel(te_ref, nu_ref, xs_ref, wg_ref, wu_ref, wd_ref, ys_ref):
    del te_ref

    @pl.when(pl.program_id(0) < nu_ref[0])
    def _():
        x = _unpack_rows(xs_ref[0], xs_ref[1]).astype(BF16)
        hg = jnp.dot(x, wg_ref[0], preferred_element_type=F32)
        hu = jnp.dot(x, wu_ref[0], preferred_element_type=F32)
        y = _mm(hg * jax.nn.sigmoid(hg) * hu, wd_ref[0])
        ys_ref[0], ys_ref[1] = _pack_rows(y)


def _experts(xs, tile_expert, n_used, wg, wu, wd):
    n_slots = xs.shape[1]
    slots = pl.BlockSpec((2, MOE_TILE, PACK_W), lambda i, te, nu: (0, i, 0))
    return pl.pallas_call(
        _experts_kernel, name="experts",
        grid_spec=pltpu.PrefetchScalarGridSpec(
            num_scalar_prefetch=2, grid=(n_slots // MOE_TILE,),
            in_specs=[
                slots,
                pl.BlockSpec((1, D_MODEL, EXPERT_FF), lambda i, te, nu: (te[i], 0, 0)),
                pl.BlockSpec((1, D_MODEL, EXPERT_FF), lambda i, te, nu: (te[i], 0, 0)),
                pl.BlockSpec((1, EXPERT_FF, D_MODEL), lambda i, te, nu: (te[i], 0, 0)),
            ],
            out_specs=slots,
        ),
        out_shape=jax.ShapeDtypeStruct(xs.shape, jnp.uint32),
        compiler_params=_cparams(("arbitrary",)),
    )(tile_expert, n_used, xs, wg, wu, wd)


def _combine_kernel(x1_ref, info_ref, yg_ref, fw_ref, y_ref):
    info = info_ref[...]
    x = (x1_ref[...] + info[:, 2:3] * _unpack_rows(yg_ref[0, 0], yg_ref[0, 1])
         + info[:, 3:4] * _unpack_rows(yg_ref[1, 0], yg_ref[1, 1]))
    ms = jnp.mean(x * x, axis=-1, keepdims=True)
    y_ref[...] = x * lax.rsqrt(ms + EPS) * fw_ref[...]


def _combine(x1, info, yg, fw):
    m = x1.shape[0]
    tm = 512 if m % 512 == 0 else 256
    row = lambda i: (i, 0)
    return pl.pallas_call(
        _combine_kernel, name="combine", grid=(m // tm,),
        in_specs=[pl.BlockSpec((tm, D_MODEL), row), pl.BlockSpec((tm, LANES), row),
                  pl.BlockSpec((2, 2, tm, PACK_W), lambda i: (0, 0, i, 0)), _full(fw.shape)],
        out_specs=pl.BlockSpec((tm, D_MODEL), row),
        out_shape=jax.ShapeDtypeStruct((m, D_MODEL), F32),
        compiler_params=_cparams(("parallel",)),
    )(x1, info, yg, fw)


def _moe_final(xp, x1, info, wg, wu, wd, fw):
    m = x1.shape[0]
    pos, tile_expert, n_used = _routing_tables(info)
    n_slots = 2 * m + N_EXPERTS * MOE_TILE
    idx = jnp.concatenate([h * n_slots + pos[:, k] for k in range(2) for h in range(2)])
    xs = _sc_scatter_rows(xp.reshape(2 * m, PACK_W), idx, 2 * n_slots, 2)
    ys = _experts(xs.reshape(2, n_slots, PACK_W), tile_expert, n_used, wg, wu, wd)
    yg = _sc_gather_rows(ys.reshape(2 * n_slots, PACK_W), idx)
    return _combine(x1, info, yg.reshape(2, 2, m, PACK_W), fw)


def _block_ones(n, blk):
    idx = np.arange(n)
    return jnp.asarray((idx[:, None] // blk == idx[None, :] // blk), dtype=BF16)


def _prepare(norm1_w, w_in, att_q_norm, att_k_norm, dn_conv_w, dn_a_log, dn_dt_bias, dn_out_norm,
             w_branch_att, w_branch_dn, w_out, norm2_w, moe_group_router, moe_expert_router,
             moe_w_gate, moe_w_up, moe_w_down, final_norm_w):
    w_in = w_in[0]
    o_q, o_k, o_v = 0, ATT_Q_W, ATT_Q_W + ATT_KV_W
    o_dq = o_v + ATT_KV_W
    o_dk, o_dv, o_dz = o_dq + DN_W, o_dq + 2 * DN_W, o_dq + 3 * DN_W
    o_da = o_dz + DN_W
    o_db = o_da + 2 * DN_HEADS
    o_ga = o_db + 2 * DN_HEADS
    o_gb = o_ga + D_MODEL

    deint = np.concatenate([np.arange(0, HEAD_DIM, 2), np.arange(1, HEAD_DIM, 2)])
    q_cols = np.concatenate([o_q + h * HEAD_DIM + deint for h in range(ATT_HEADS)])
    k_cols = np.concatenate([o_k + h * HEAD_DIM + deint for h in range(ATT_KV_HEADS)])
    v_cols = np.arange(o_v, o_v + ATT_KV_W)
    watt = w_in[:, np.concatenate([q_cols, k_cols, v_cols])].astype(BF16)
    qnw = jnp.tile(att_q_norm[0][deint], ATT_HEADS)[None, :]
    knw = jnp.tile(att_k_norm[0][deint], ATT_KV_HEADS)[None, :]

    hd = np.arange(DN_DIM)
    kq_cols = np.concatenate([np.concatenate([o_dk + h * DN_DIM + hd, o_dq + h * DN_DIM + hd]) for h in range(DN_HEADS)])
    vk_cols = np.concatenate([np.concatenate([o_dv + h * DN_DIM + hd, o_dk + h * DN_DIM + hd]) for h in range(DN_HEADS)])
    dn_cols = np.concatenate([kq_cols, vk_cols])
    wdn = w_in[:, dn_cols].astype(BF16)
    cw = jnp.concatenate([dn_conv_w[0][:, dn_cols - o_dq], jnp.zeros((8 - DN_CONV, dn_cols.size), F32)], axis=0)
    is_q = np.concatenate([np.tile(np.concatenate([np.zeros(DN_DIM), np.ones(DN_DIM)]), DN_HEADS), np.zeros(2 * DN_W)])
    is_v = np.concatenate([np.zeros(2 * DN_W), np.tile(np.concatenate([np.ones(DN_DIM), np.zeros(DN_DIM)]), DN_HEADS)])
    donorm = jnp.asarray(1.0 - is_v, F32)[None, :]
    scale = jnp.asarray(np.where(is_q > 0, DN_DIM ** -0.5, 1.0), F32)[None, :]

    pad_cols = np.concatenate([np.concatenate([h * DN_DIM + hd, np.full(DN_DIM, -1)]) for h in range(DN_HEADS)])
    valid = jnp.asarray(pad_cols >= 0)
    take = np.maximum(pad_cols, 0)
    wdz = jnp.where(valid[None, :], w_in[:, o_dz + take], 0.0).astype(BF16)
    onw = jnp.where(valid, jnp.tile(dn_out_norm[0], 2 * DN_HEADS), 0.0)[None, :]
    wb = jnp.where(valid[:, None], w_branch_dn[0][take, :], 0.0).astype(BF16)

    wab = jnp.zeros((D_MODEL, 2 * LANES), F32)
    for dirn in range(2):
        wab = wab.at[:, dirn * LANES:dirn * LANES + DN_HEADS].set(
            w_in[:, o_da + dirn * DN_HEADS:o_da + (dirn + 1) * DN_HEADS])
        wab = wab.at[:, dirn * LANES + DN_HEADS:dirn * LANES + 2 * DN_HEADS].set(
            w_in[:, o_db + dirn * DN_HEADS:o_db + (dirn + 1) * DN_HEADS])
    wab = wab.astype(BF16)
    wg = w_in[:, o_ga:o_gb + D_MODEL].astype(BF16)

    zeros8 = jnp.zeros((2, DN_HEADS), F32)
    ea = jnp.concatenate([jnp.exp(dn_a_log[0]), zeros8], axis=1)[:, None, :]
    bias = jnp.concatenate([dn_dt_bias[0], zeros8], axis=1)[:, None, :]

    idx = np.arange(2 * LANES)
    ok = (idx % LANES) < DN_DIM
    bdo = jnp.asarray((idx[:, None] // LANES == idx[None, :] // LANES) & ok[:, None] & ok[None, :], dtype=BF16)

    wr = jnp.concatenate([moe_group_router[0], moe_expert_router[0],
                          jnp.zeros((D_MODEL, LANES - N_GROUPS - N_EXPERTS), F32)], axis=1)
    wr_hi = wr.astype(BF16)
    wr_lo = (wr - wr_hi.astype(F32)).astype(BF16)

    return dict(
        n1=norm1_w[0][None, :], watt=watt, wdn=wdn, wdz=wdz, wab=wab, wg=wg,
        bd_att=_block_ones(ATT_Q_W, HEAD_DIM), qnw=qnw, knw=knw,
        cw=cw, bd_dn=_block_ones(2 * LANES, DN_DIM), donorm=donorm, scale=scale, ea=ea, bias=bias,
        wa=w_branch_att[0].astype(BF16), wb=wb, wo=w_out[0].astype(BF16), bdo=bdo, onw=onw,
        n2=norm2_w[0][None, :], wr_hi=wr_hi, wr_lo=wr_lo,
        wge=moe_w_gate[0].astype(BF16), wue=moe_w_up[0].astype(BF16), wde=moe_w_down[0].astype(BF16),
        fw=final_norm_w[None, :],
    )


def _rope_tables(seq):
    t = np.arange(seq)
    axis_dim = HEAD_DIM // 2
    inv = ROPE_THETA ** (-np.arange(0, axis_dim, 2, dtype=np.float32) / axis_dim)
    r = (t // GRID_W).astype(np.float32)
    c = (t % GRID_W).astype(np.float32)
    ang = np.concatenate([r[:, None] * inv, c[:, None] * inv], axis=-1).astype(np.float32)
    ang = jnp.asarray(ang)
    cos, sin = jnp.cos(ang), jnp.sin(ang)
    cos_t = jnp.tile(jnp.concatenate([cos, cos], axis=1), (1, LANES // HEAD_DIM))
    sin_t = jnp.tile(jnp.concatenate([-sin, sin], axis=1), (1, LANES // HEAD_DIM))
    return cos_t, sin_t


def _trunk(x, p):
    batch, seq, _ = x.shape
    x2 = x.reshape(batch * seq, D_MODEL)
    cos_t, sin_t = _rope_tables(seq)
    qt, kn, vt, dnpre, dz, dab, sg, kmx_tiles = _inproj(
        x2, seq, p["n1"], p["watt"], p["wdn"], p["wdz"], p["wab"], p["wg"], p["bd_att"], p["qnw"], p["knw"],
        cos_t, sin_t)
    kmx = jnp.max(kmx_tiles.reshape(batch, -1, 8, ATT_KV_HEADS, HEAD_DIM), axis=(1, 2, 4))
    kmx = jnp.broadcast_to(kmx[:, :, None, None], (batch, ATT_KV_HEADS, 8, LANES))
    ot = _attention(qt, kn, vt, kmx, batch, seq)
    kq, vk = _dnprep(dnpre, seq, p["cw"], p["bd_dn"], p["donorm"], p["scale"])
    o_f, o_b = _delta_rule(kq, vk, dab, p["ea"], p["bias"], batch, seq)
    x1, xp, logits = _merge(ot, o_f, o_b, dz, sg, x2, p["wa"], p["wb"], p["wo"], p["bdo"], p["onw"], p["n2"],
                            p["wr_hi"], p["wr_lo"])
    info = _route(logits)
    y = _moe_final(xp, x1, info, p["wge"], p["wue"], p["wde"], p["fw"])
    return y.reshape(batch, seq, D_MODEL)


def kernel(x_prompt, x_sample, norm1_w, w_in, att_q_norm, att_k_norm, dn_conv_w, dn_a_log, dn_dt_bias, dn_out_norm, w_branch_att, w_branch_dn, w_out, norm2_w, moe_group_router, moe_expert_router, moe_w_gate, moe_w_up, moe_w_down, final_norm_w):
    p = _prepare(norm1_w, w_in, att_q_norm, att_k_norm, dn_conv_w, dn_a_log, dn_dt_bias, dn_out_norm,
                 w_branch_att, w_branch_dn, w_out, norm2_w, moe_group_router, moe_expert_router,
                 moe_w_gate, moe_w_up, moe_w_down, final_norm_w)
    return (_trunk(x_prompt, p), _trunk(x_sample, p))
```

```python
import functools
import math

import numpy as np
import jax
import jax.numpy as jnp
from jax import lax
from jax.experimental import pallas as pl
from jax.experimental.pallas import tpu as pltpu
from jax.experimental.pallas import tpu_sc as plsc

F32 = jnp.float32
BF16 = jnp.bfloat16

D_MODEL = 1024
GRID_W = 64
EPS = 1e-6
ATT_HEADS = 8
ATT_KV_HEADS = 2
ATT_GROUP = ATT_HEADS // ATT_KV_HEADS
HEAD_DIM = 64
ROPE_THETA = 10000.0
DN_HEADS = 8
DN_DIM = 64
DN_CONV = 5
N_GROUPS = 4
EXPERTS_PER_GROUP = 8
N_EXPERTS = N_GROUPS * EXPERTS_PER_GROUP
EXPERT_FF = 256

ATT_Q_W = ATT_HEADS * HEAD_DIM
ATT_KV_W = ATT_KV_HEADS * HEAD_DIM
DN_W = DN_HEADS * DN_DIM
LANES = 128
DN_CHUNK = 128
ROUTER_OFF = N_GROUPS
PACK_W = D_MODEL // 4
SC_WINDOW = 128
MOE_TILE = 512
ATT_TQ = 1024
ATT_TK = 1024
VMEM_LIMIT = 52 * 1024 * 1024
LOG2E = math.log2(math.e)
SCORE_BOUND_MAX = 50.0
SCORE_BOUND_SLACK = 1.05


def _mm(a, b):
    return jnp.dot(a.astype(BF16), b.astype(BF16), preferred_element_type=F32)


def _cparams(sem):
    return pltpu.CompilerParams(dimension_semantics=sem, vmem_limit_bytes=VMEM_LIMIT)


def _full(shape):
    nd = len(shape)
    return pl.BlockSpec(shape, lambda *_: (0,) * nd, pipeline_mode=pl.Buffered(1))


def _rope(x, cos, sin_signed):
    n = x.shape[1]
    lane = lax.broadcasted_iota(jnp.int32, x.shape, 1)
    first = (lane % HEAD_DIM) < (HEAD_DIM // 2)
    partner = jnp.where(first, pltpu.roll(x, n - HEAD_DIM // 2, 1), pltpu.roll(x, HEAD_DIM // 2, 1))
    return x * cos + partner * sin_signed


def _inproj_kernel(x_ref, n1_ref, watt_ref, wdn_ref, wdz_ref, wab_ref, wg_ref, bd_ref, qnw_ref, knw_ref,
                   cos_ref, sin_ref, qt_ref, k_ref, vt_ref, dn_ref, dz_ref, dab_ref, sg_ref, kmx_ref):
    tm = x_ref.shape[0]
    x = x_ref[...]
    ms = jnp.mean(x * x, axis=-1, keepdims=True)
    xn = (x * lax.rsqrt(ms + EPS) * n1_ref[...]).astype(BF16)

    att = jnp.dot(xn, watt_ref[...], preferred_element_type=F32)
    aq = att[:, :ATT_Q_W]
    ak = att[:, ATT_Q_W:ATT_Q_W + ATT_KV_W]
    av = att[:, ATT_Q_W + ATT_KV_W:]
    bd = bd_ref[...]
    qss = _mm(aq * aq, bd)
    kss = _mm(ak * ak, bd[:ATT_KV_W, :ATT_KV_W])
    cos = cos_ref[...]
    sin = sin_ref[...]
    cos4 = jnp.concatenate([cos] * (ATT_Q_W // LANES), axis=1)
    sin4 = jnp.concatenate([sin] * (ATT_Q_W // LANES), axis=1)
    q = aq * lax.rsqrt(qss * (1.0 / HEAD_DIM) + EPS) * qnw_ref[...]
    k = ak * lax.rsqrt(kss * (1.0 / HEAD_DIM) + EPS) * knw_ref[...]
    q = _rope(q, cos4, sin4) * (HEAD_DIM ** -0.5 * LOG2E)
    kmx_ref[0] = jnp.broadcast_to(jnp.max(_mm(k * k, bd[:ATT_KV_W, :ATT_KV_W]), axis=0, keepdims=True),
                                  (8, ATT_KV_W))
    k = _rope(k, cos, sin)
    qt_ref[...] = q.T.reshape(ATT_HEADS, HEAD_DIM, tm).astype(BF16)
    k_ref[0] = k[:, :HEAD_DIM].astype(BF16)
    k_ref[1] = k[:, HEAD_DIM:].astype(BF16)
    vt_ref[...] = av.T.reshape(ATT_KV_HEADS, HEAD_DIM, tm).astype(BF16)

    dn_ref[...] = jnp.dot(xn, wdn_ref[...], preferred_element_type=F32).astype(BF16)
    dz_ref[...] = jnp.dot(xn, wdz_ref[...], preferred_element_type=F32).astype(BF16)
    ab = jnp.dot(xn, wab_ref[...], preferred_element_type=F32)
    dab_ref[0] = ab[:, 0:2 * DN_HEADS]
    dab_ref[1] = ab[:, LANES:LANES + 2 * DN_HEADS]
    sg_ref[...] = jax.nn.sigmoid(jnp.dot(xn, wg_ref[...], preferred_element_type=F32)).astype(BF16)


def _inproj(x2, seq, n1, watt, wdn, wdz, wab, wg, bd, qnw, knw, cos_t, sin_t):
    m = x2.shape[0]
    tm = min(512, seq)
    per_seq = seq // tm
    row = lambda i: (i, 0)
    out_shape = (
        jax.ShapeDtypeStruct((ATT_HEADS, HEAD_DIM, m), BF16),
        jax.ShapeDtypeStruct((ATT_KV_HEADS, m, HEAD_DIM), BF16),
        jax.ShapeDtypeStruct((ATT_KV_HEADS, HEAD_DIM, m), BF16),
        jax.ShapeDtypeStruct((m, 4 * DN_W), BF16),
        jax.ShapeDtypeStruct((m, DN_W), BF16),
        jax.ShapeDtypeStruct((2, m, 2 * DN_HEADS), F32),
        jax.ShapeDtypeStruct((m, 2 * D_MODEL), BF16),
        jax.ShapeDtypeStruct((m // tm, 8, ATT_KV_W), F32),
    )
    in_specs = [
        pl.BlockSpec((tm, D_MODEL), row),
        _full(n1.shape), _full(watt.shape), _full(wdn.shape), _full(wdz.shape), _full(wab.shape),
        _full(wg.shape), _full(bd.shape), _full(qnw.shape), _full(knw.shape),
        pl.BlockSpec((tm, LANES), lambda i: (i % per_seq, 0)),
        pl.BlockSpec((tm, LANES), lambda i: (i % per_seq, 0)),
    ]
    out_specs = (
        pl.BlockSpec((ATT_HEADS, HEAD_DIM, tm), lambda i: (0, 0, i)),
        pl.BlockSpec((ATT_KV_HEADS, tm, HEAD_DIM), lambda i: (0, i, 0)),
        pl.BlockSpec((ATT_KV_HEADS, HEAD_DIM, tm), lambda i: (0, 0, i)),
        pl.BlockSpec((tm, 4 * DN_W), row),
        pl.BlockSpec((tm, DN_W), row),
        pl.BlockSpec((2, tm, 2 * DN_HEADS), lambda i: (0, i, 0)),
        pl.BlockSpec((tm, 2 * D_MODEL), row),
        pl.BlockSpec((1, 8, ATT_KV_W), lambda i: (i, 0, 0)),
    )
    return pl.pallas_call(
        _inproj_kernel, name="inproj", grid=(m // tm,), in_specs=in_specs, out_specs=out_specs, out_shape=out_shape,
        compiler_params=_cparams(("parallel",)),
    )(x2, n1, watt, wdn, wdz, wab, wg, bd, qnw, knw, cos_t, sin_t)


def _attn_kernel(qt_ref, k_ref, vt_ref, kmx_ref, ot_ref, *, tk, nk):
    tq = qt_ref.shape[2]
    q = jnp.concatenate([qt_ref[r] for r in range(ATT_GROUP)], axis=1)
    n = q.shape[1]
    qf = q.astype(F32)
    bound = jnp.sqrt(jnp.sum(qf * qf, axis=0, keepdims=True) * kmx_ref[0, 0][0:1, 0:1]) * SCORE_BOUND_SLACK
    small = jnp.max(bound) <= SCORE_BOUND_MAX
    ones_rows = 16

    def blocks(kb):
        off = pl.multiple_of(kb * tk, tk)
        return k_ref[0, pl.ds(off, tk), :], vt_ref[0, :, pl.ds(off, tk)]

    def write(out):
        for r in range(ATT_GROUP):
            ot_ref[r] = out[:, r * tq:(r + 1) * tq].astype(BF16)

    @pl.when(small)
    def _():
        def body(kb, acc):
            kblk, vblk = blocks(kb)
            s = jnp.dot(kblk, q, preferred_element_type=F32)
            p = jnp.exp2(s - bound).astype(BF16)
            vaug = jnp.concatenate([vblk, jnp.ones((ones_rows, tk), BF16)], axis=0)
            return acc + jnp.dot(vaug, p, preferred_element_type=F32)

        acc = lax.fori_loop(0, nk, body, jnp.zeros((HEAD_DIM + ones_rows, n), F32))
        write(acc[0:HEAD_DIM] / acc[HEAD_DIM:HEAD_DIM + 1])

    @pl.when(jnp.logical_not(small))
    def _():
        def body(kb, carry):
            m_run, l_run, acc = carry
            kblk, vblk = blocks(kb)
            s = jnp.dot(kblk, q, preferred_element_type=F32)
            m_new = jnp.maximum(m_run, jnp.max(s, axis=0, keepdims=True))
            p = jnp.exp2(s - m_new)
            alpha = jnp.exp2(m_run - m_new)
            l_new = alpha * l_run + jnp.sum(p, axis=0, keepdims=True)
            acc_new = alpha * acc + jnp.dot(vblk, p.astype(BF16), preferred_element_type=F32)
            return m_new, l_new, acc_new

        init = (jnp.full((1, n), -jnp.inf, F32), jnp.zeros((1, n), F32), jnp.zeros((HEAD_DIM, n), F32))
        _, l_fin, acc = lax.fori_loop(0, nk, body, init)
        write(acc / l_fin)


def _attention(qt, kn, vt, kmx, batch, seq):
    m = qt.shape[2]
    tq = min(ATT_TQ, seq)
    tk = min(ATT_TK, seq)
    nq = seq // tq
    qspec = pl.BlockSpec((ATT_GROUP, HEAD_DIM, tq), lambda b, g, i: (g, 0, b * nq + i))
    return pl.pallas_call(
        functools.partial(_attn_kernel, tk=tk, nk=seq // tk), name="attention",
        grid=(batch, ATT_KV_HEADS, nq),
        in_specs=[
            qspec,
            pl.BlockSpec((1, seq, HEAD_DIM), lambda b, g, i: (g, b, 0)),
            pl.BlockSpec((1, HEAD_DIM, seq), lambda b, g, i: (g, 0, b)),
            pl.BlockSpec((1, 1, 8, LANES), lambda b, g, i: (b, g, 0, 0)),
        ],
        out_specs=qspec,
        out_shape=jax.ShapeDtypeStruct((ATT_HEADS, HEAD_DIM, m), BF16),
        compiler_params=_cparams(("parallel", "parallel", "parallel")),
    )(qt, kn, vt, kmx)


def _dnprep_kernel(cur_ref, prev_ref, next_ref, cw_ref, shift_ref, bd_ref, donorm_ref, scale_ref, kq_ref, vk_ref,
                   buf_ref, *, seq):
    tm = cur_ref.shape[0]
    halo = prev_ref.shape[0]
    blk = shift_ref.shape[1] - 2 * halo
    i = pl.program_id(0)
    pos = (i * tm) % seq
    has_prev = pos != 0
    has_next = pos + tm != seq
    buf_ref[0:halo, :] = jnp.where(has_prev, prev_ref[...], jnp.zeros_like(prev_ref))
    buf_ref[halo:halo + tm, :] = cur_ref[...]
    buf_ref[halo + tm:, :] = jnp.where(has_next, next_ref[...], jnp.zeros_like(next_ref))
    cw = cw_ref[...]
    pad = DN_CONV // 2
    shift = shift_ref[...]
    rows = []
    for r in range(tm // blk):
        win = buf_ref[r * blk:(r + 1) * blk + 2 * halo, :]
        shifted = jnp.dot(shift, win, preferred_element_type=F32)
        acc = win[halo:halo + blk, :].astype(F32) * cw[pad:pad + 1, :]
        for n, j in enumerate(t for t in range(DN_CONV) if t != pad):
            acc = acc + shifted[n * blk:(n + 1) * blk, :] * cw[j:j + 1, :]
        rows.append(acc)
    y = jnp.concatenate(rows, axis=0) if len(rows) > 1 else rows[0]
    y = y * jax.nn.sigmoid(y)
    bd = bd_ref[...]
    w = bd.shape[0]
    ysq = y * y
    ss = jnp.concatenate([_mm(ysq[:, c * w:(c + 1) * w], bd) for c in range(y.shape[1] // w)], axis=1)
    factor = jnp.where(donorm_ref[...] > 0.5, lax.rsqrt(ss + EPS), 1.0) * scale_ref[...]
    y = y * factor
    half = y.shape[1] // 2
    kq_ref[...] = y[:, :half].astype(BF16)
    vk_ref[...] = y[:, half:].astype(BF16)


def _dnprep(dnpre, seq, cw, bd, donorm, scale):
    m, width = dnpre.shape
    tm = min(512, seq)
    halo = 16
    hb = tm // halo
    last = m // halo - 1
    blk = min(128, tm)
    pad = DN_CONV // 2
    t = np.arange(blk)
    shift = np.zeros((DN_CONV - 1, blk, blk + 2 * halo), np.float32)
    for n, j in enumerate(x for x in range(DN_CONV) if x != pad):
        shift[n, t, halo + t + j - pad] = 1.0
    shift = jnp.asarray(shift.reshape((DN_CONV - 1) * blk, blk + 2 * halo), BF16)
    return pl.pallas_call(
        functools.partial(_dnprep_kernel, seq=seq), name="dnprep",
        grid=(m // tm,),
        in_specs=[
            pl.BlockSpec((tm, width), lambda i: (i, 0)),
            pl.BlockSpec((halo, width), lambda i: (jnp.maximum(i * hb - 1, 0), 0)),
            pl.BlockSpec((halo, width), lambda i: (jnp.minimum((i + 1) * hb, last), 0)),
            _full(cw.shape), _full(shift.shape), _full(bd.shape), _full(donorm.shape), _full(scale.shape),
        ],
        out_specs=(pl.BlockSpec((tm, width // 2), lambda i: (i, 0)),
                   pl.BlockSpec((tm, width // 2), lambda i: (i, 0))),
        out_shape=(jax.ShapeDtypeStruct((m, width // 2), BF16), jax.ShapeDtypeStruct((m, width // 2), BF16)),
        scratch_shapes=[pltpu.VMEM((tm + 2 * halo, width), BF16)],
        compiler_params=_cparams(("parallel",)),
    )(dnpre, dnpre, dnpre, cw, shift, bd, donorm, scale)


def _bd2(a, b):
    return jnp.concatenate([jnp.concatenate([a, jnp.zeros_like(b)], axis=1),
                            jnp.concatenate([jnp.zeros_like(a), b], axis=1)], axis=0)


def _softplus(x):
    return jnp.maximum(x, 0.0) + jnp.log1p(jnp.exp(-jnp.abs(x)))


def _delta_kernel(kqf_ref, kqb_ref, vkf_ref, vkb_ref, dabf_ref, dabb_ref, ea_ref, bias_ref, of_ref, ob_ref, s_ref):
    C = DN_CHUNK
    H = DN_DIM
    n = pl.program_id(1)

    @pl.when(n == 0)
    def _():
        s_ref[...] = jnp.zeros_like(s_ref)

    row = lax.broadcasted_iota(jnp.int32, (C, C), 0)
    col = lax.broadcasted_iota(jnp.int32, (C, C), 1)
    incl = (row >= col, row <= col)
    strict = (row > col, row < col)
    rowp = lax.broadcasted_iota(jnp.int32, (C, 2 * C), 0)
    colp = lax.broadcasted_iota(jnp.int32, (C, 2 * C), 1) % C
    lane = lax.broadcasted_iota(jnp.int32, (C, LANES), 1)
    lo_half = lane < H
    lane_s = lax.broadcasted_iota(jnp.int32, (H, LANES), 1)
    eye_p =jnp.where(rowp == colp, 1.0, 0.0)
    blk2 = rowp // 2 == colp // 2
    levels = []
    b = 2
    while b < C:
        levels.append((rowp // (2 * b) == colp // (2 * b)) & (rowp // b != colp // b))
        b *= 2

    kq_refs = (kqf_ref, kqb_ref)
    vk_refs = (vkf_ref, vkb_ref)
    dab_refs = (dabf_ref, dabb_ref)
    o_refs = (of_ref, ob_ref)

    gc_all, gl_all, beta_all = [], [], []
    for d in range(2):
        ab = dab_refs[d][0]
        g = -ea_ref[d] * _softplus(ab + bias_ref[d])
        beta_all.append(jax.nn.sigmoid(ab))
        tri = jnp.where(incl[d], 1.0, 0.0).astype(BF16)
        g_hi = g.astype(BF16)
        g_lo = (g - g_hi.astype(F32)).astype(BF16)
        gc_all.append(jnp.dot(tri, g_hi, preferred_element_type=F32)
                      + jnp.dot(tri, g_lo, preferred_element_type=F32))
        gl_all.append(jnp.sum(g, axis=0, keepdims=True))

    units = [(d, j) for d in range(2) for j in range(DN_HEADS // 2)]

    def halves(x):
        return x[:, :x.shape[1] // 2], x[:, x.shape[1] // 2:]

    def rhs1(t):
        z = jnp.zeros_like(t)
        return jnp.concatenate([jnp.concatenate([t, z], axis=1), jnp.concatenate([z, t], axis=1)], axis=0)

    kq_p, kqf, vkf, gccol, gcrow, bcol, gl, kT, decay = ({} for _ in range(9))
    for u in units:
        d, j = u
        sl = slice(2 * j * LANES, (2 * j + 2) * LANES)
        kq_p[u] = kq_refs[d][:, sl]
        kqf[u] = kq_p[u].astype(F32)
        vkf[u] = vk_refs[d][:, sl].astype(F32)
        gccol[u], gcrow[u], bcol[u], gl[u], kT[u], decay[u] = [], [], [], [], [], []
        for hh in range(2):
            h = 2 * j + hh
            gccol[u].append(jnp.broadcast_to(gc_all[d][:, h:h + 1], (C, LANES)))
            bcol[u].append(jnp.broadcast_to(beta_all[d][:, DN_HEADS + h:DN_HEADS + h + 1], (C, LANES)))
            gl[u].append(gl_all[d][:, h:h + 1])
            gcrow[u].append(gccol[u][hh].T)
            kT[u].append(kqf[u][:, hh * LANES:(hh + 1) * LANES].T[0:H, :])
            decay[u].append(jnp.where(incl[d], jnp.exp(jnp.minimum(gccol[u][hh] - gcrow[u][hh], 0.0)), 0.0))

    p1 = {u: _mm(kq_p[u], _bd2(rhs1(kT[u][0]), rhs1(kT[u][1]))) for u in units}
    a_p, qkd, x_p = {}, {}, {}
    for u in units:
        d = u[0]
        a_h, qk_h = [], []
        for hh in range(2):
            kk = p1[u][:, (2 * hh) * C:(2 * hh + 1) * C]
            qk = p1[u][:, (2 * hh + 1) * C:(2 * hh + 2) * C]
            a_h.append(jnp.where(strict[d], kk * decay[u][hh], 0.0) * bcol[u][hh])
            qk_h.append(qk * decay[u][hh])
        a_p[u] = jnp.concatenate(a_h, axis=1)
        qkd[u] = jnp.concatenate(qk_h, axis=1)
        x_p[u] = eye_p - jnp.where(blk2, a_p[u], 0.0)

    for msk in levels:
        g_p = {u: _mm(jnp.where(msk, a_p[u], 0.0), _bd2(*halves(x_p[u]))) for u in units}
        x_p = {u: x_p[u] - _mm(x_p[u], _bd2(*halves(g_p[u]))) for u in units}

    uw = {}
    for u in units:
        rhs2 = [vkf[u][:, hh * LANES:(hh + 1) * LANES] * bcol[u][hh]
                * jnp.where(lo_half, 1.0, jnp.exp(gccol[u][hh])) for hh in range(2)]
        uw[u] = _mm(x_p[u], _bd2(rhs2[0], rhs2[1]))
    ol = {u: _mm(qkd[u], _bd2(*halves(uw[u]))) for u in units}
    nw = {}
    for u in units:
        kdT = jnp.concatenate([kT[u][hh] * jnp.exp(gl[u][hh] - gcrow[u][hh][0:H, :]) for hh in range(2)], axis=0)
        nw[u] = _mm(kdT, uw[u])
    for u in units:
        d, j = u
        sl = slice(2 * j * LANES, (2 * j + 2) * LANES)
        zmult = jnp.concatenate(
            [jnp.where(lo_half, jnp.exp(gl[u][hh] - gccol[u][hh]), jnp.exp(gccol[u][hh])) for hh in range(2)], axis=1)
        z_p = kqf[u] * zmult - ol[u]
        nw_h = (nw[u][0:H, 0:LANES], nw[u][H:2 * H, LANES:2 * LANES])
        s_old = [s_ref[d, 2 * j + hh] for hh in range(2)]
        sblk = [jnp.concatenate([jnp.zeros((H, LANES), F32), s_old[hh]], axis=0) for hh in range(2)]
        r = _mm(jnp.concatenate([_bd2(*nw_h), z_p], axis=0), _bd2(sblk[0], sblk[1]))
        o_pair = r[2 * H:, :] + ol[u]
        o_refs[d][:, j * LANES:(j + 1) * LANES] = jnp.where(
            lo_half, o_pair[:, :LANES], pltpu.roll(o_pair[:, LANES:], H, 1))
        wks = (r[0:H, 0:LANES], r[H:2 * H, LANES:2 * LANES])
        for hh in range(2):
            s_new = jnp.exp(gl[u][hh]) * s_old[hh] + nw_h[hh] - wks[hh]
            s_ref[d, 2 * j + hh] = jnp.where(lane_s < H, s_new, 0.0)


def _delta_rule(kq, vk, dab, ea, bias, batch, seq):
    m = kq.shape[0]
    C = DN_CHUNK
    nc = seq // C
    width = kq.shape[1]
    fwd = lambda b, n: (b * nc + n, 0)
    bwd = lambda b, n: (b * nc + nc - 1 - n, 0)
    return pl.pallas_call(
        _delta_kernel, name="delta",
        grid=(batch, nc),
        in_specs=[
            pl.BlockSpec((C, width), fwd), pl.BlockSpec((C, width), bwd),
            pl.BlockSpec((C, width), fwd), pl.BlockSpec((C, width), bwd),
            pl.BlockSpec((1, C, 2 * DN_HEADS), lambda b, n: (0, b * nc + n, 0)),
            pl.BlockSpec((1, C, 2 * DN_HEADS), lambda b, n: (1, b * nc + nc - 1 - n, 0)),
            _full(ea.shape), _full(bias.shape),
        ],
        out_specs=(pl.BlockSpec((C, DN_W), fwd), pl.BlockSpec((C, DN_W), bwd)),
        out_shape=(jax.ShapeDtypeStruct((m, DN_W), F32), jax.ShapeDtypeStruct((m, DN_W), F32)),
        scratch_shapes=[pltpu.VMEM((2, DN_HEADS, DN_DIM, LANES), F32)],
        compiler_params=_cparams(("parallel", "arbitrary")),
    )(kq, kq, vk, vk, dab, dab, ea, bias)


def _merge_kernel(ot_ref, of_ref, ob_ref, dz_ref, sg_ref, x_ref, wa_ref, wb_ref, wo_ref, bdo_ref, onw_ref, n2_ref,
                  wr_hi_ref, wr_lo_ref, x1_ref, xp_ref, lg_ref):
    tm = x_ref.shape[0]
    att = ot_ref[...].reshape(ATT_Q_W, tm).astype(F32).T
    ya = _mm(att, wa_ref[...])
    o = of_ref[...] + ob_ref[...]
    ss = _mm(o * o, bdo_ref[...])
    dz = dz_ref[...].astype(F32)
    dn = o * lax.rsqrt(ss * (1.0 / DN_DIM) + EPS) * onw_ref[...] * (dz * jax.nn.sigmoid(dz))
    yb = _mm(dn, wb_ref[...])
    sg = sg_ref[...].astype(F32)
    mix = sg[:, :D_MODEL] * ya + sg[:, D_MODEL:] * yb
    x1 = x_ref[...] + _mm(mix, wo_ref[...])
    x1_ref[...] = x1
    ms = jnp.mean(x1 * x1, axis=-1, keepdims=True)
    xn = x1 * lax.rsqrt(ms + EPS) * n2_ref[...]
    xn_hi = xn.astype(BF16)
    xn_lo = (xn - xn_hi.astype(F32)).astype(BF16)
    xp_ref[0], xp_ref[1] = _pack_rows(xn)
    whi = wr_hi_ref[...]
    lg_ref[...] = (jnp.dot(xn_hi, whi, preferred_element_type=F32)
                   + jnp.dot(xn_lo, whi, preferred_element_type=F32)
                   + jnp.dot(xn_hi, wr_lo_ref[...], preferred_element_type=F32))


def _merge(ot, o_f, o_b, dz, sg, x2, wa, wb, wo, bdo, onw, n2, wr_hi, wr_lo):
    m = x2.shape[0]
    tm = 256
    row = lambda i: (i, 0)
    return pl.pallas_call(
        _merge_kernel, name="merge", grid=(m // tm,),
        in_specs=[
            pl.BlockSpec((ATT_HEADS, HEAD_DIM, tm), lambda i: (0, 0, i)),
            pl.BlockSpec((tm, o_f.shape[1]), row),
            pl.BlockSpec((tm, o_b.shape[1]), row),
            pl.BlockSpec((tm, dz.shape[1]), row),
            pl.BlockSpec((tm, sg.shape[1]), row),
            pl.BlockSpec((tm, D_MODEL), row),
            _full(wa.shape), _full(wb.shape), _full(wo.shape), _full(bdo.shape), _full(onw.shape),
            _full(n2.shape), _full(wr_hi.shape), _full(wr_lo.shape),
        ],
        out_specs=(pl.BlockSpec((tm, D_MODEL), row), pl.BlockSpec((2, tm, PACK_W), lambda i: (0, i, 0)),
                   pl.BlockSpec((tm, LANES), row)),
        out_shape=(jax.ShapeDtypeStruct((m, D_MODEL), F32), jax.ShapeDtypeStruct((2, m, PACK_W), jnp.uint32),
                   jax.ShapeDtypeStruct((m, LANES), F32)),
        compiler_params=_cparams(("parallel",)),
    )(ot, o_f, o_b, dz, sg, x2, wa, wb, wo, bdo, onw, n2, wr_hi, wr_lo)


def _route_kernel(lg_ref, info_ref, cnt_ref):
    @pl.when(pl.program_id(0) == 0)
    def _():
        cnt_ref[...] = jnp.zeros_like(cnt_ref)

    lg = lg_ref[...]
    lane_i = lax.broadcasted_iota(jnp.int32, lg.shape, 1)
    lane = lane_i.astype(F32)
    group_of = ((lane_i - ROUTER_OFF) // EXPERTS_PER_GROUP).astype(F32)
    neg = -jnp.inf

    def first_argmax(v):
        mx = jnp.max(v, axis=-1, keepdims=True)
        idx = jnp.min(jnp.where(v == mx, lane, float(LANES)), axis=-1, keepdims=True)
        return mx, idx

    gl = jnp.where(lane_i < N_GROUPS, lg, neg)
    gmax, gidx = first_argmax(gl)
    gval = 1.0 / jnp.sum(jnp.exp(gl - gmax), axis=-1, keepdims=True)
    is_exp = (lane_i >= ROUTER_OFF) & (lane_i < ROUTER_OFF + N_EXPERTS)
    sel = is_exp & (group_of == gidx)
    el = jnp.where(sel, lg, neg)
    m1, i1 = first_argmax(el)
    el2 = jnp.where(lane == i1, neg, el)
    m2, i2 = first_argmax(el2)
    r = jnp.exp(m2 - m1)
    w1 = gval / (1.0 + r)
    w2 = gval * r / (1.0 + r)
    tm = lg.shape[0]
    chosen = jnp.where((lane == i1) | (lane == i2), 1.0, 0.0)
    earlier = (lax.broadcasted_iota(jnp.int32, (tm, tm), 0) > lax.broadcasted_iota(jnp.int32, (tm, tm), 1))
    before = _mm(jnp.where(earlier, 1.0, 0.0), chosen) + cnt_ref[0:1, :]
    rank1 = jnp.sum(jnp.where(lane == i1, before, 0.0), axis=-1, keepdims=True)
    rank2 = jnp.sum(jnp.where(lane == i2, before, 0.0), axis=-1, keepdims=True)
    cnt_ref[...] = cnt_ref[...] + jnp.sum(chosen, axis=0, keepdims=True)
    info_ref[...] = (jnp.where(lane_i == 0, i1 - ROUTER_OFF, 0.0) + jnp.where(lane_i == 1, i2 - ROUTER_OFF, 0.0)
                     + jnp.where(lane_i == 2, w1, 0.0) + jnp.where(lane_i == 3, w2, 0.0)
                     + jnp.where(lane_i == 4, rank1, 0.0) + jnp.where(lane_i == 5, rank2, 0.0))


def _route(logits):
    m = logits.shape[0]
    tm = 1024 if m % 1024 == 0 else 256
    spec = pl.BlockSpec((tm, LANES), lambda i: (i, 0))
    return pl.pallas_call(
        _route_kernel, name="route", grid=(m // tm,), in_specs=[spec],
        out_specs=(spec, pl.BlockSpec((8, LANES), lambda i: (0, 0))),
        out_shape=(jax.ShapeDtypeStruct((m, LANES), F32), jax.ShapeDtypeStruct((8, LANES), F32)),
        compiler_params=_cparams(("arbitrary",)),
    )(logits)


def _pack_rows(x):
    bits = pltpu.bitcast(x.astype(BF16).astype(F32), jnp.uint32)
    half = x.shape[1] // 2
    word = (bits[:, :half] >> 16) | (bits[:, half:] & jnp.uint32(0xFFFF0000))
    return word[:, :PACK_W], word[:, PACK_W:]


def _unpack_rows(w0, w1):
    lo = [pltpu.bitcast(w << 16, F32) for w in (w0, w1)]
    hi = [pltpu.bitcast(w & jnp.uint32(0xFFFF0000), F32) for w in (w0, w1)]
    return jnp.concatenate(lo + hi, axis=1)


def _sc_mesh():
    return plsc.VectorSubcoreMesh(core_axis_name="c", subcore_axis_name="s")


def _sc_scatter_rows(src, idx, n_out, reps):
    s, width = src.shape
    nblk = s // SC_WINDOW

    @pl.kernel(out_type=jax.ShapeDtypeStruct((n_out, width), src.dtype), mesh=_sc_mesh(), scratch_types=[])
    def scatter_kernel(x_hbm, i_hbm, o_hbm):
        def body(x_vmem, i_vmem):
            pltpu.sync_copy(x_vmem, o_hbm.at[i_vmem.at[0]])

        pltpu.emit_pipeline(
            body, grid=(reps * nblk,),
            in_specs=[pl.BlockSpec((SC_WINDOW, width), index_map=lambda i: (i % nblk, 0)),
                      pl.BlockSpec((1, SC_WINDOW), index_map=lambda i: (0, i))],
            out_specs=[], core_axis_name=("c", "s"), dimension_semantics=(pltpu.PARALLEL,),
        )(x_hbm, i_hbm)

    return scatter_kernel(src, idx.reshape(1, reps * s))


def _sc_gather_rows(table, idx):
    k = idx.shape[0]
    width = table.shape[1]

    @pl.kernel(out_type=jax.ShapeDtypeStruct((k, width), table.dtype), mesh=_sc_mesh())
    def gather_kernel(x_hbm, i_hbm, o_hbm):
        def body(i_vmem, o_vmem):
            pltpu.sync_copy(x_hbm.at[i_vmem.at[0]], o_vmem)

        pltpu.emit_pipeline(
            body, grid=(k // SC_WINDOW,),
            in_specs=[pl.BlockSpec((1, SC_WINDOW), index_map=lambda i: (0, i))],
            out_specs=[pl.BlockSpec((SC_WINDOW, width), index_map=lambda i: (i, 0))],
            core_axis_name=("c", "s"), dimension_semantics=(pltpu.PARALLEL,),
        )(i_hbm, o_hbm)

    return gather_kernel(table, idx.reshape(1, k))


def _routing_tables(info, counts):
    m = info.shape[0]
    e = info[:, 0:2].astype(jnp.int32)
    rank = info[:, 4:6].astype(jnp.int32)
    cnt = counts[0, ROUTER_OFF:ROUTER_OFF + N_EXPERTS].astype(jnp.int32)
    tiles_e = (cnt + MOE_TILE - 1) // MOE_TILE
    tile_end = jnp.cumsum(tiles_e)
    slot_start = (tile_end - tiles_e) * MOE_TILE
    pos = jnp.take(slot_start, e, axis=0) + rank
    n_tiles = 2 * m // MOE_TILE + N_EXPERTS
    tile_expert = jnp.sum(jnp.arange(n_tiles, dtype=jnp.int32)[:, None] >= tile_end[None, :], axis=1)
    tile_expert = jnp.minimum(tile_expert, N_EXPERTS - 1).astype(jnp.int32)
    return pos, tile_expert, tile_end[-1:].astype(jnp.int32)


def _experts_kernel(te_ref, nu_ref, xs_ref, wg_ref, wu_ref, wd_ref, ys_ref):
    del te_ref

    @pl.when(pl.program_id(0) < nu_ref[0])
    def _():
        x = _unpack_rows(xs_ref[0], xs_ref[1]).astype(BF16)
        hg = _mm(x, wg_ref[0])
        hu = _mm(x, wu_ref[0])
        y = _mm(hg * jax.nn.sigmoid(hg) * hu, wd_ref[0])
        ys_ref[0], ys_ref[1] = _pack_rows(y)


def _experts(xs, tile_expert, n_used, wg, wu, wd):
    n_slots = xs.shape[1]
    slots = pl.BlockSpec((2, MOE_TILE, PACK_W), lambda i, te, nu: (0, i, 0))
    return pl.pallas_call(
        _experts_kernel, name="experts",
        grid_spec=pltpu.PrefetchScalarGridSpec(
            num_scalar_prefetch=2, grid=(n_slots // MOE_TILE,),
            in_specs=[
                slots,
                pl.BlockSpec((1, D_MODEL, EXPERT_FF), lambda i, te, nu: (te[i], 0, 0)),
                pl.BlockSpec((1, D_MODEL, EXPERT_FF), lambda i, te, nu: (te[i], 0, 0)),
                pl.BlockSpec((1, EXPERT_FF, D_MODEL), lambda i, te, nu: (te[i], 0, 0)),
            ],
            out_specs=slots,
        ),
        out_shape=jax.ShapeDtypeStruct(xs.shape, jnp.uint32),
        compiler_params=_cparams(("arbitrary",)),
    )(tile_expert, n_used, xs, wg, wu, wd)


def _combine_kernel(x1_ref, info_ref, yg_ref, fw_ref, y_ref):
    info = info_ref[...]
    x = (x1_ref[...] + info[:, 2:3] * _unpack_rows(yg_ref[0, 0], yg_ref[0, 1])
         + info[:, 3:4] * _unpack_rows(yg_ref[1, 0], yg_ref[1, 1]))
    ms = jnp.mean(x * x, axis=-1, keepdims=True)
    y_ref[...] = x * lax.rsqrt(ms + EPS) * fw_ref[...]


def _combine(x1, info, yg, fw):
    m = x1.shape[0]
    tm = 512 if m % 512 == 0 else 256
    row = lambda i: (i, 0)
    return pl.pallas_call(
        _combine_kernel, name="combine", grid=(m // tm,),
        in_specs=[pl.BlockSpec((tm, D_MODEL), row), pl.BlockSpec((tm, LANES), row),
                  pl.BlockSpec((2, 2, tm, PACK_W), lambda i: (0, 0, i, 0)), _full(fw.shape)],
        out_specs=pl.BlockSpec((tm, D_MODEL), row),
        out_shape=jax.ShapeDtypeStruct((m, D_MODEL), F32),
        compiler_params=_cparams(("parallel",)),
    )(x1, info, yg, fw)


def _moe_final(xp, x1, info, counts, wg, wu, wd, fw):
    m = x1.shape[0]
    pos, tile_expert, n_used = _routing_tables(info, counts)
    n_slots = 2 * m + N_EXPERTS * MOE_TILE
    idx = jnp.concatenate([h * n_slots + pos[:, k] for k in range(2) for h in range(2)])
    xs = _sc_scatter_rows(xp.reshape(2 * m, PACK_W), idx, 2 * n_slots, 2)
    ys = _experts(xs.reshape(2, n_slots, PACK_W), tile_expert, n_used, wg, wu, wd)
    yg = _sc_gather_rows(ys.reshape(2 * n_slots, PACK_W), idx)
    return _combine(x1, info, yg.reshape(2, 2, m, PACK_W), fw)


def _block_ones(n, blk):
    idx = np.arange(n)
    return jnp.asarray((idx[:, None] // blk == idx[None, :] // blk), dtype=BF16)


def _prepare(norm1_w, w_in, att_q_norm, att_k_norm, dn_conv_w, dn_a_log, dn_dt_bias, dn_out_norm,
             w_branch_att, w_branch_dn, w_out, norm2_w, moe_group_router, moe_expert_router,
             moe_w_gate, moe_w_up, moe_w_down, final_norm_w):
    w_in = w_in[0]
    o_q, o_k, o_v = 0, ATT_Q_W, ATT_Q_W + ATT_KV_W
    o_dq = o_v + ATT_KV_W
    o_dk, o_dv, o_dz = o_dq + DN_W, o_dq + 2 * DN_W, o_dq + 3 * DN_W
    o_da = o_dz + DN_W
    o_db = o_da + 2 * DN_HEADS
    o_ga = o_db + 2 * DN_HEADS
    o_gb = o_ga + D_MODEL

    deint = np.concatenate([np.arange(0, HEAD_DIM, 2), np.arange(1, HEAD_DIM, 2)])
    q_cols = np.concatenate([o_q + h * HEAD_DIM + deint for h in range(ATT_HEADS)])
    k_cols = np.concatenate([o_k + h * HEAD_DIM + deint for h in range(ATT_KV_HEADS)])
    v_cols = np.arange(o_v, o_v + ATT_KV_W)
    watt = w_in[:, np.concatenate([q_cols, k_cols, v_cols])].astype(BF16)
    qnw = jnp.tile(att_q_norm[0][deint], ATT_HEADS)[None, :]
    knw = jnp.tile(att_k_norm[0][deint], ATT_KV_HEADS)[None, :]

    hd = np.arange(DN_DIM)
    kq_cols = np.concatenate([np.concatenate([o_dk + h * DN_DIM + hd, o_dq + h * DN_DIM + hd]) for h in range(DN_HEADS)])
    vk_cols = np.concatenate([np.concatenate([o_dv + h * DN_DIM + hd, o_dk + h * DN_DIM + hd]) for h in range(DN_HEADS)])
    dn_cols = np.concatenate([kq_cols, vk_cols])
    wdn = w_in[:, dn_cols].astype(BF16)
    cw = jnp.concatenate([dn_conv_w[0][:, dn_cols - o_dq], jnp.zeros((8 - DN_CONV, dn_cols.size), F32)], axis=0)
    is_q = np.concatenate([np.tile(np.concatenate([np.zeros(DN_DIM), np.ones(DN_DIM)]), DN_HEADS), np.zeros(2 * DN_W)])
    is_v = np.concatenate([np.zeros(2 * DN_W), np.tile(np.concatenate([np.ones(DN_DIM), np.zeros(DN_DIM)]), DN_HEADS)])
    donorm = jnp.asarray(1.0 - is_v, F32)[None, :]
    scale = jnp.asarray(np.where(is_q > 0, DN_DIM ** -0.5, 1.0), F32)[None, :]

    wdz = w_in[:, o_dz:o_dz + DN_W].astype(BF16)
    onw = jnp.tile(dn_out_norm[0], DN_HEADS)[None, :]
    wb = w_branch_dn[0].astype(BF16)

    wab = jnp.zeros((D_MODEL, 2 * LANES), F32)
    for dirn in range(2):
        wab = wab.at[:, dirn * LANES:dirn * LANES + DN_HEADS].set(
            w_in[:, o_da + dirn * DN_HEADS:o_da + (dirn + 1) * DN_HEADS])
        wab = wab.at[:, dirn * LANES + DN_HEADS:dirn * LANES + 2 * DN_HEADS].set(
            w_in[:, o_db + dirn * DN_HEADS:o_db + (dirn + 1) * DN_HEADS])
    wab = wab.astype(BF16)
    wg = w_in[:, o_ga:o_gb + D_MODEL].astype(BF16)

    zeros8 = jnp.zeros((2, DN_HEADS), F32)
    ea = jnp.concatenate([jnp.exp(dn_a_log[0]), zeros8], axis=1)[:, None, :]
    bias = jnp.concatenate([dn_dt_bias[0], zeros8], axis=1)[:, None, :]


    wr = jnp.concatenate([moe_group_router[0], moe_expert_router[0],
                          jnp.zeros((D_MODEL, LANES - N_GROUPS - N_EXPERTS), F32)], axis=1)
    wr_hi = wr.astype(BF16)
    wr_lo = (wr - wr_hi.astype(F32)).astype(BF16)

    return dict(
        n1=norm1_w[0][None, :], watt=watt, wdn=wdn, wdz=wdz, wab=wab, wg=wg,
        bd_att=_block_ones(ATT_Q_W, HEAD_DIM), qnw=qnw, knw=knw,
        cw=cw, bd_dn=_block_ones(2 * LANES, DN_DIM), donorm=donorm, scale=scale, ea=ea, bias=bias,
        wa=w_branch_att[0].astype(BF16), wb=wb, wo=w_out[0].astype(BF16), bdo=_block_ones(DN_W, DN_DIM), onw=onw,
        n2=norm2_w[0][None, :], wr_hi=wr_hi, wr_lo=wr_lo,
        wge=moe_w_gate[0], wue=moe_w_up[0], wde=moe_w_down[0],
        fw=final_norm_w[None, :],
    )


def _rope_tables(seq):
    t = np.arange(seq)
    axis_dim = HEAD_DIM // 2
    inv = ROPE_THETA ** (-np.arange(0, axis_dim, 2, dtype=np.float32) / axis_dim)
    r = (t // GRID_W).astype(np.float32)
    c = (t % GRID_W).astype(np.float32)
    ang = np.concatenate([r[:, None] * inv, c[:, None] * inv], axis=-1).astype(np.float32)
    ang = jnp.asarray(ang)
    cos, sin = jnp.cos(ang), jnp.sin(ang)
    cos_t = jnp.tile(jnp.concatenate([cos, cos], axis=1), (1, LANES // HEAD_DIM))
    sin_t = jnp.tile(jnp.concatenate([-sin, sin], axis=1), (1, LANES // HEAD_DIM))
    return cos_t, sin_t


def _trunk(x, p):
    batch, seq, _ = x.shape
    x2 = x.reshape(batch * seq, D_MODEL)
    cos_t, sin_t = _rope_tables(seq)
    qt, kn, vt, dnpre, dz, dab, sg, kmx_tiles = _inproj(
        x2, seq, p["n1"], p["watt"], p["wdn"], p["wdz"], p["wab"], p["wg"], p["bd_att"], p["qnw"], p["knw"],
        cos_t, sin_t)
    kmx = jnp.max(kmx_tiles.reshape(batch, -1, 8, ATT_KV_HEADS, HEAD_DIM), axis=(1, 2, 4))
    kmx = jnp.broadcast_to(kmx[:, :, None, None], (batch, ATT_KV_HEADS, 8, LANES))
    ot = _attention(qt, kn, vt, kmx, batch, seq)
    kq, vk = _dnprep(dnpre, seq, p["cw"], p["bd_dn"], p["donorm"], p["scale"])
    o_f, o_b = _delta_rule(kq, vk, dab, p["ea"], p["bias"], batch, seq)
    x1, xp, logits = _merge(ot, o_f, o_b, dz, sg, x2, p["wa"], p["wb"], p["wo"], p["bdo"], p["onw"], p["n2"],
                            p["wr_hi"], p["wr_lo"])
    info, counts = _route(logits)
    y = _moe_final(xp, x1, info, counts, p["wge"], p["wue"], p["wde"], p["fw"])
    return y.reshape(batch, seq, D_MODEL)


def kernel(x_prompt, x_sample, norm1_w, w_in, att_q_norm, att_k_norm, dn_conv_w, dn_a_log, dn_dt_bias, dn_out_norm, w_branch_att, w_branch_dn, w_out, norm2_w, moe_group_router, moe_expert_router, moe_w_gate, moe_w_up, moe_w_down, final_norm_w):
    p = _prepare(norm1_w, w_in, att_q_norm, att_k_norm, dn_conv_w, dn_a_log, dn_dt_bias, dn_out_norm,
                 w_branch_att, w_branch_dn, w_out, norm2_w, moe_group_router, moe_expert_router,
                 moe_w_gate, moe_w_up, moe_w_down, final_norm_w)
    return (_trunk(x_prompt, p), _trunk(x_sample, p))
```

```python
import functools
import math

import numpy as np
import jax
import jax.numpy as jnp
from jax import lax
from jax.experimental import pallas as pl
from jax.experimental.pallas import tpu as pltpu
from jax.experimental.pallas import tpu_sc as plsc

F32 = jnp.float32
BF16 = jnp.bfloat16

D_MODEL = 1024
GRID_W = 64
EPS = 1e-6
ATT_HEADS = 8
ATT_KV_HEADS = 2
ATT_GROUP = ATT_HEADS // ATT_KV_HEADS
HEAD_DIM = 64
ROPE_THETA = 10000.0
DN_HEADS = 8
DN_DIM = 64
DN_CONV = 5
N_GROUPS = 4
EXPERTS_PER_GROUP = 8
N_EXPERTS = N_GROUPS * EXPERTS_PER_GROUP
EXPERT_FF = 256

ATT_Q_W = ATT_HEADS * HEAD_DIM
ATT_KV_W = ATT_KV_HEADS * HEAD_DIM
DN_W = DN_HEADS * DN_DIM
LANES = 128
DN_CHUNK = 128
ROUTER_OFF = N_GROUPS
PACK_W = D_MODEL // 4
SC_WINDOW = 128
MOE_TILE = 512
ATT_TQ = 1024
ATT_TK = 1024
VMEM_LIMIT = 52 * 1024 * 1024
LOG2E = math.log2(math.e)
SCORE_BOUND_MAX = 50.0
SCORE_BOUND_SLACK = 1.05


def _mm(a, b):
    return jnp.dot(a.astype(BF16), b.astype(BF16), preferred_element_type=F32)


def _cparams(sem):
    return pltpu.CompilerParams(dimension_semantics=sem, vmem_limit_bytes=VMEM_LIMIT)


def _full(shape):
    nd = len(shape)
    return pl.BlockSpec(shape, lambda *_: (0,) * nd, pipeline_mode=pl.Buffered(1))


def _rope(x, cos, sin_signed):
    n = x.shape[1]
    lane = lax.broadcasted_iota(jnp.int32, x.shape, 1)
    first = (lane % HEAD_DIM) < (HEAD_DIM // 2)
    partner = jnp.where(first, pltpu.roll(x, n - HEAD_DIM // 2, 1), pltpu.roll(x, HEAD_DIM // 2, 1))
    return x * cos + partner * sin_signed


def _inproj_kernel(x_ref, n1_ref, watt_ref, wdn_ref, wdz_ref, wab_ref, wg_ref, bd_ref, qnw_ref, knw_ref,
                   cos_ref, sin_ref, qt_ref, k_ref, vt_ref, dn_ref, dz_ref, dab_ref, sg_ref, kmx_ref):
    tm = x_ref.shape[0]
    x = x_ref[...]
    ms = jnp.mean(x * x, axis=-1, keepdims=True)
    xn = (x * lax.rsqrt(ms + EPS) * n1_ref[...]).astype(BF16)

    att = jnp.dot(xn, watt_ref[...], preferred_element_type=F32)
    aq = att[:, :ATT_Q_W]
    ak = att[:, ATT_Q_W:ATT_Q_W + ATT_KV_W]
    av = att[:, ATT_Q_W + ATT_KV_W:]
    bd = bd_ref[...]
    qss = _mm(aq * aq, bd)
    kss = _mm(ak * ak, bd[:ATT_KV_W, :ATT_KV_W])
    cos = cos_ref[...]
    sin = sin_ref[...]
    cos4 = jnp.concatenate([cos] * (ATT_Q_W // LANES), axis=1)
    sin4 = jnp.concatenate([sin] * (ATT_Q_W // LANES), axis=1)
    q = aq * lax.rsqrt(qss * (1.0 / HEAD_DIM) + EPS) * qnw_ref[...]
    k = ak * lax.rsqrt(kss * (1.0 / HEAD_DIM) + EPS) * knw_ref[...]
    q = _rope(q, cos4, sin4) * (HEAD_DIM ** -0.5 * LOG2E)
    kmx_ref[0] = jnp.broadcast_to(jnp.max(_mm(k * k, bd[:ATT_KV_W, :ATT_KV_W]), axis=0, keepdims=True),
                                  (8, ATT_KV_W))
    k = _rope(k, cos, sin)
    qt_ref[...] = q.T.reshape(ATT_HEADS, HEAD_DIM, tm).astype(BF16)
    k_ref[0] = k[:, :HEAD_DIM].astype(BF16)
    k_ref[1] = k[:, HEAD_DIM:].astype(BF16)
    vt_ref[...] = av.T.reshape(ATT_KV_HEADS, HEAD_DIM, tm).astype(BF16)

    dn_ref[...] = jnp.dot(xn, wdn_ref[...], preferred_element_type=F32).astype(BF16)
    dz_ref[...] = jnp.dot(xn, wdz_ref[...], preferred_element_type=F32).astype(BF16)
    ab = jnp.dot(xn, wab_ref[...], preferred_element_type=F32)
    dab_ref[0] = ab[:, 0:2 * DN_HEADS]
    dab_ref[1] = ab[:, LANES:LANES + 2 * DN_HEADS]
    sg_ref[...] = jax.nn.sigmoid(jnp.dot(xn, wg_ref[...], preferred_element_type=F32)).astype(BF16)


def _inproj(x2, seq, n1, watt, wdn, wdz, wab, wg, bd, qnw, knw, cos_t, sin_t):
    m = x2.shape[0]
    tm = min(512, seq)
    per_seq = seq // tm
    row = lambda i: (i, 0)
    out_shape = (
        jax.ShapeDtypeStruct((ATT_HEADS, HEAD_DIM, m), BF16),
        jax.ShapeDtypeStruct((ATT_KV_HEADS, m, HEAD_DIM), BF16),
        jax.ShapeDtypeStruct((ATT_KV_HEADS, HEAD_DIM, m), BF16),
        jax.ShapeDtypeStruct((m, 4 * DN_W), BF16),
        jax.ShapeDtypeStruct((m, DN_W), BF16),
        jax.ShapeDtypeStruct((2, m, 2 * DN_HEADS), F32),
        jax.ShapeDtypeStruct((m, 2 * D_MODEL), BF16),
        jax.ShapeDtypeStruct((m // tm, 8, ATT_KV_W), F32),
    )
    in_specs = [
        pl.BlockSpec((tm, D_MODEL), row),
        _full(n1.shape), _full(watt.shape), _full(wdn.shape), _full(wdz.shape), _full(wab.shape),
        _full(wg.shape), _full(bd.shape), _full(qnw.shape), _full(knw.shape),
        pl.BlockSpec((tm, LANES), lambda i: (i % per_seq, 0)),
        pl.BlockSpec((tm, LANES), lambda i: (i % per_seq, 0)),
    ]
    out_specs = (
        pl.BlockSpec((ATT_HEADS, HEAD_DIM, tm), lambda i: (0, 0, i)),
        pl.BlockSpec((ATT_KV_HEADS, tm, HEAD_DIM), lambda i: (0, i, 0)),
        pl.BlockSpec((ATT_KV_HEADS, HEAD_DIM, tm), lambda i: (0, 0, i)),
        pl.BlockSpec((tm, 4 * DN_W), row),
        pl.BlockSpec((tm, DN_W), row),
        pl.BlockSpec((2, tm, 2 * DN_HEADS), lambda i: (0, i, 0)),
        pl.BlockSpec((tm, 2 * D_MODEL), row),
        pl.BlockSpec((1, 8, ATT_KV_W), lambda i: (i, 0, 0)),
    )
    return pl.pallas_call(
        _inproj_kernel, name="inproj", grid=(m // tm,), in_specs=in_specs, out_specs=out_specs, out_shape=out_shape,
        compiler_params=_cparams(("parallel",)),
    )(x2, n1, watt, wdn, wdz, wab, wg, bd, qnw, knw, cos_t, sin_t)


def _attn_kernel(qt_ref, k_ref, vt_ref, kmx_ref, ot_ref, p_ref, *, tk, nk):
    tq = qt_ref.shape[2]
    q = jnp.concatenate([qt_ref[r] for r in range(ATT_GROUP)], axis=1)
    n = q.shape[1]
    qf = q.astype(F32)
    bound = jnp.sqrt(jnp.sum(qf * qf, axis=0, keepdims=True) * kmx_ref[0, 0][0:1, 0:1]) * SCORE_BOUND_SLACK
    small = jnp.max(bound) <= SCORE_BOUND_MAX
    ones_rows = 16

    def blocks(kb):
        off = pl.multiple_of(kb * tk, tk)
        return k_ref[0, pl.ds(off, tk), :], vt_ref[0, :, pl.ds(off, tk)]

    def write(out):
        for r in range(ATT_GROUP):
            ot_ref[r] = out[:, r * tq:(r + 1) * tq].astype(BF16)

    @pl.when(small)
    def _():
        def probs(kb):
            s = jnp.dot(blocks(kb)[0], q, preferred_element_type=F32)
            return jnp.exp2(s - bound).astype(BF16)

        def weighted(kb, p):
            vaug = jnp.concatenate([blocks(kb)[1], jnp.ones((ones_rows, tk), BF16)], axis=0)
            return jnp.dot(vaug, p, preferred_element_type=F32)

        acc0 = jnp.zeros((HEAD_DIM + ones_rows, n), F32)
        p_ref[0] = probs(0)
        if nk == 1:
            acc = acc0 + weighted(0, p_ref[0])
        else:
            def pair(kb, acc, last):
                p_ref[1] = probs(kb + 1)
                acc = acc + weighted(kb, p_ref[0])
                if not last:
                    p_ref[0] = probs(kb + 2)
                return acc + weighted(kb + 1, p_ref[1])

            acc = lax.fori_loop(0, nk // 2 - 1, lambda j, a: pair(2 * j, a, False), acc0)
            acc = pair(nk - 2, acc, True)
        write(acc[0:HEAD_DIM] / acc[HEAD_DIM:HEAD_DIM + 1])

    @pl.when(jnp.logical_not(small))
    def _():
        def body(kb, carry):
            m_run, l_run, acc = carry
            kblk, vblk = blocks(kb)
            s = jnp.dot(kblk, q, preferred_element_type=F32)
            m_new = jnp.maximum(m_run, jnp.max(s, axis=0, keepdims=True))
            p = jnp.exp2(s - m_new)
            alpha = jnp.exp2(m_run - m_new)
            l_new = alpha * l_run + jnp.sum(p, axis=0, keepdims=True)
            acc_new = alpha * acc + jnp.dot(vblk, p.astype(BF16), preferred_element_type=F32)
            return m_new, l_new, acc_new

        init = (jnp.full((1, n), -jnp.inf, F32), jnp.zeros((1, n), F32), jnp.zeros((HEAD_DIM, n), F32))
        _, l_fin, acc = lax.fori_loop(0, nk, body, init)
        write(acc / l_fin)


def _attention(qt, kn, vt, kmx, batch, seq):
    m = qt.shape[2]
    tq = min(ATT_TQ, seq)
    tk = min(ATT_TK, seq)
    nq = seq // tq
    qspec = pl.BlockSpec((ATT_GROUP, HEAD_DIM, tq), lambda b, g, i: (g, 0, b * nq + i))
    return pl.pallas_call(
        functools.partial(_attn_kernel, tk=tk, nk=seq // tk), name="attention",
        grid=(batch, ATT_KV_HEADS, nq),
        in_specs=[
            qspec,
            pl.BlockSpec((1, seq, HEAD_DIM), lambda b, g, i: (g, b, 0)),
            pl.BlockSpec((1, HEAD_DIM, seq), lambda b, g, i: (g, 0, b)),
            pl.BlockSpec((1, 1, 8, LANES), lambda b, g, i: (b, g, 0, 0)),
        ],
        out_specs=qspec,
        out_shape=jax.ShapeDtypeStruct((ATT_HEADS, HEAD_DIM, m), BF16),
        scratch_shapes=[pltpu.VMEM((2, tk, ATT_GROUP * tq), BF16)],
        compiler_params=_cparams(("parallel", "parallel", "parallel")),
    )(qt, kn, vt, kmx)


def _dnprep_kernel(cur_ref, prev_ref, next_ref, cw_ref, shift_ref, bd_ref, donorm_ref, scale_ref, kq_ref, vk_ref,
                   buf_ref, *, seq):
    tm = cur_ref.shape[0]
    halo = prev_ref.shape[0]
    blk = shift_ref.shape[1] - 2 * halo
    i = pl.program_id(0)
    pos = (i * tm) % seq
    has_prev = pos != 0
    has_next = pos + tm != seq
    buf_ref[0:halo, :] = jnp.where(has_prev, prev_ref[...], jnp.zeros_like(prev_ref))
    buf_ref[halo:halo + tm, :] = cur_ref[...]
    buf_ref[halo + tm:, :] = jnp.where(has_next, next_ref[...], jnp.zeros_like(next_ref))
    cw = cw_ref[...]
    pad = DN_CONV // 2
    shift = shift_ref[...]
    rows = []
    for r in range(tm // blk):
        win = buf_ref[r * blk:(r + 1) * blk + 2 * halo, :]
        shifted = jnp.dot(shift, win, preferred_element_type=F32)
        acc = win[halo:halo + blk, :].astype(F32) * cw[pad:pad + 1, :]
        for n, j in enumerate(t for t in range(DN_CONV) if t != pad):
            acc = acc + shifted[n * blk:(n + 1) * blk, :] * cw[j:j + 1, :]
        rows.append(acc)
    y = jnp.concatenate(rows, axis=0) if len(rows) > 1 else rows[0]
    y = y * jax.nn.sigmoid(y)
    bd = bd_ref[...]
    w = bd.shape[0]
    ysq = y * y
    ss = jnp.concatenate([_mm(ysq[:, c * w:(c + 1) * w], bd) for c in range(y.shape[1] // w)], axis=1)
    factor = jnp.where(donorm_ref[...] > 0.5, lax.rsqrt(ss + EPS), 1.0) * scale_ref[...]
    y = y * factor
    half = y.shape[1] // 2
    kq_ref[...] = y[:, :half].astype(BF16)
    vk_ref[...] = y[:, half:].astype(BF16)


def _dnprep(dnpre, seq, cw, bd, donorm, scale):
    m, width = dnpre.shape
    tm = min(512, seq)
    halo = 16
    hb = tm // halo
    last = m // halo - 1
    blk = min(128, tm)
    pad = DN_CONV // 2
    t = np.arange(blk)
    shift = np.zeros((DN_CONV - 1, blk, blk + 2 * halo), np.float32)
    for n, j in enumerate(x for x in range(DN_CONV) if x != pad):
        shift[n, t, halo + t + j - pad] = 1.0
    shift = jnp.asarray(shift.reshape((DN_CONV - 1) * blk, blk + 2 * halo), BF16)
    return pl.pallas_call(
        functools.partial(_dnprep_kernel, seq=seq), name="dnprep",
        grid=(m // tm,),
        in_specs=[
            pl.BlockSpec((tm, width), lambda i: (i, 0)),
            pl.BlockSpec((halo, width), lambda i: (jnp.maximum(i * hb - 1, 0), 0)),
            pl.BlockSpec((halo, width), lambda i: (jnp.minimum((i + 1) * hb, last), 0)),
            _full(cw.shape), _full(shift.shape), _full(bd.shape), _full(donorm.shape), _full(scale.shape),
        ],
        out_specs=(pl.BlockSpec((tm, width // 2), lambda i: (i, 0)),
                   pl.BlockSpec((tm, width // 2), lambda i: (i, 0))),
        out_shape=(jax.ShapeDtypeStruct((m, width // 2), BF16), jax.ShapeDtypeStruct((m, width // 2), BF16)),
        scratch_shapes=[pltpu.VMEM((tm + 2 * halo, width), BF16)],
        compiler_params=_cparams(("parallel",)),
    )(dnpre, dnpre, dnpre, cw, shift, bd, donorm, scale)


def _bd2(a, b):
    return jnp.concatenate([jnp.concatenate([a, jnp.zeros_like(b)], axis=1),
                            jnp.concatenate([jnp.zeros_like(a), b], axis=1)], axis=0)


def _softplus(x):
    return jnp.maximum(x, 0.0) + jnp.log1p(jnp.exp(-jnp.abs(x)))


def _delta_kernel(kqf_ref, kqb_ref, vkf_ref, vkb_ref, dabf_ref, dabb_ref, ea_ref, bias_ref, of_ref, ob_ref, s_ref):
    C = DN_CHUNK
    H = DN_DIM
    n = pl.program_id(1)

    @pl.when(n == 0)
    def _():
        s_ref[...] = jnp.zeros_like(s_ref)

    row = lax.broadcasted_iota(jnp.int32, (C, C), 0)
    col = lax.broadcasted_iota(jnp.int32, (C, C), 1)
    incl = (row >= col, row <= col)
    strict = (row > col, row < col)
    rowp = lax.broadcasted_iota(jnp.int32, (C, 2 * C), 0)
    colp = lax.broadcasted_iota(jnp.int32, (C, 2 * C), 1) % C
    lane = lax.broadcasted_iota(jnp.int32, (C, LANES), 1)
    lo_half = lane < H
    lane_s = lax.broadcasted_iota(jnp.int32, (H, LANES), 1)
    eye_p =jnp.where(rowp == colp, 1.0, 0.0)
    blk2 = rowp // 2 == colp // 2
    levels = []
    b = 2
    while b < C:
        levels.append((rowp // (2 * b) == colp // (2 * b)) & (rowp // b != colp // b))
        b *= 2

    kq_refs = (kqf_ref, kqb_ref)
    vk_refs = (vkf_ref, vkb_ref)
    dab_refs = (dabf_ref, dabb_ref)
    o_refs = (of_ref, ob_ref)

    gc_all, gl_all, beta_all = [], [], []
    for d in range(2):
        ab = dab_refs[d][0]
        g = -ea_ref[d] * _softplus(ab + bias_ref[d])
        beta_all.append(jax.nn.sigmoid(ab))
        tri = jnp.where(incl[d], 1.0, 0.0).astype(BF16)
        g_hi = g.astype(BF16)
        g_lo = (g - g_hi.astype(F32)).astype(BF16)
        gc_all.append(jnp.dot(tri, g_hi, preferred_element_type=F32)
                      + jnp.dot(tri, g_lo, preferred_element_type=F32))
        gl_all.append(jnp.sum(g, axis=0, keepdims=True))

    units = [(d, j) for d in range(2) for j in range(DN_HEADS // 2)]

    def halves(x):
        return x[:, :x.shape[1] // 2], x[:, x.shape[1] // 2:]

    def rhs1(t):
        z = jnp.zeros_like(t)
        return jnp.concatenate([jnp.concatenate([t, z], axis=1), jnp.concatenate([z, t], axis=1)], axis=0)

    kq_p, kqf, vkf, gccol, gcrow, bcol, gl, kT, decay = ({} for _ in range(9))
    for u in units:
        d, j = u
        sl = slice(2 * j * LANES, (2 * j + 2) * LANES)
        kq_p[u] = kq_refs[d][:, sl]
        kqf[u] = kq_p[u].astype(F32)
        vkf[u] = vk_refs[d][:, sl].astype(F32)
        gccol[u], gcrow[u], bcol[u], gl[u], kT[u], decay[u] = [], [], [], [], [], []
        for hh in range(2):
            h = 2 * j + hh
            gccol[u].append(jnp.broadcast_to(gc_all[d][:, h:h + 1], (C, LANES)))
            bcol[u].append(jnp.broadcast_to(beta_all[d][:, DN_HEADS + h:DN_HEADS + h + 1], (C, LANES)))
            gl[u].append(gl_all[d][:, h:h + 1])
            gcrow[u].append(gccol[u][hh].T)
            kT[u].append(kqf[u][:, hh * LANES:(hh + 1) * LANES].T[0:H, :])
            decay[u].append(jnp.where(incl[d], jnp.exp(jnp.minimum(gccol[u][hh] - gcrow[u][hh], 0.0)), 0.0))

    p1 = {u: _mm(kq_p[u], _bd2(rhs1(kT[u][0]), rhs1(kT[u][1]))) for u in units}
    a_p, qkd, x_p = {}, {}, {}
    for u in units:
        d = u[0]
        a_h, qk_h = [], []
        for hh in range(2):
            kk = p1[u][:, (2 * hh) * C:(2 * hh + 1) * C]
            qk = p1[u][:, (2 * hh + 1) * C:(2 * hh + 2) * C]
            a_h.append(jnp.where(strict[d], kk * decay[u][hh], 0.0) * bcol[u][hh])
            qk_h.append(qk * decay[u][hh])
        a_p[u] = jnp.concatenate(a_h, axis=1)
        qkd[u] = jnp.concatenate(qk_h, axis=1)
        x_p[u] = eye_p - jnp.where(blk2, a_p[u], 0.0)

    for msk in levels:
        g_p = {u: _mm(jnp.where(msk, a_p[u], 0.0), _bd2(*halves(x_p[u]))) for u in units}
        x_p = {u: x_p[u] - _mm(x_p[u], _bd2(*halves(g_p[u]))) for u in units}

    uw = {}
    for u in units:
        rhs2 = [vkf[u][:, hh * LANES:(hh + 1) * LANES] * bcol[u][hh]
                * jnp.where(lo_half, 1.0, jnp.exp(gccol[u][hh])) for hh in range(2)]
        uw[u] = _mm(x_p[u], _bd2(rhs2[0], rhs2[1]))
    ol = {u: _mm(qkd[u], _bd2(*halves(uw[u]))) for u in units}
    nw = {}
    for u in units:
        kdT = jnp.concatenate([kT[u][hh] * jnp.exp(gl[u][hh] - gcrow[u][hh][0:H, :]) for hh in range(2)], axis=0)
        nw[u] = _mm(kdT, uw[u])
    for u in units:
        d, j = u
        sl = slice(2 * j * LANES, (2 * j + 2) * LANES)
        zmult = jnp.concatenate(
            [jnp.where(lo_half, jnp.exp(gl[u][hh] - gccol[u][hh]), jnp.exp(gccol[u][hh])) for hh in range(2)], axis=1)
        z_p = kqf[u] * zmult - ol[u]
        nw_h = (nw[u][0:H, 0:LANES], nw[u][H:2 * H, LANES:2 * LANES])
        s_old = [s_ref[d, 2 * j + hh] for hh in range(2)]
        sblk = [jnp.concatenate([jnp.zeros((H, LANES), F32), s_old[hh]], axis=0) for hh in range(2)]
        r = _mm(jnp.concatenate([_bd2(*nw_h), z_p], axis=0), _bd2(sblk[0], sblk[1]))
        o_pair = r[2 * H:, :] + ol[u]
        o_refs[d][:, j * LANES:(j + 1) * LANES] = jnp.where(
            lo_half, o_pair[:, :LANES], pltpu.roll(o_pair[:, LANES:], H, 1))
        wks = (r[0:H, 0:LANES], r[H:2 * H, LANES:2 * LANES])
        for hh in range(2):
            s_new = jnp.exp(gl[u][hh]) * s_old[hh] + nw_h[hh] - wks[hh]
            s_ref[d, 2 * j + hh] = jnp.where(lane_s < H, s_new, 0.0)


def _delta_rule(kq, vk, dab, ea, bias, batch, seq):
    m = kq.shape[0]
    C = DN_CHUNK
    nc = seq // C
    width = kq.shape[1]
    fwd = lambda b, n: (b * nc + n, 0)
    bwd = lambda b, n: (b * nc + nc - 1 - n, 0)
    return pl.pallas_call(
        _delta_kernel, name="delta",
        grid=(batch, nc),
        in_specs=[
            pl.BlockSpec((C, width), fwd), pl.BlockSpec((C, width), bwd),
            pl.BlockSpec((C, width), fwd), pl.BlockSpec((C, width), bwd),
            pl.BlockSpec((1, C, 2 * DN_HEADS), lambda b, n: (0, b * nc + n, 0)),
            pl.BlockSpec((1, C, 2 * DN_HEADS), lambda b, n: (1, b * nc + nc - 1 - n, 0)),
            _full(ea.shape), _full(bias.shape),
        ],
        out_specs=(pl.BlockSpec((C, DN_W), fwd), pl.BlockSpec((C, DN_W), bwd)),
        out_shape=(jax.ShapeDtypeStruct((m, DN_W), F32), jax.ShapeDtypeStruct((m, DN_W), F32)),
        scratch_shapes=[pltpu.VMEM((2, DN_HEADS, DN_DIM, LANES), F32)],
        compiler_params=_cparams(("parallel", "arbitrary")),
    )(kq, kq, vk, vk, dab, dab, ea, bias)


def _merge_kernel(ot_ref, of_ref, ob_ref, dz_ref, sg_ref, x_ref, wa_ref, wb_ref, wo_ref, bdo_ref, onw_ref, n2_ref,
                  wr_hi_ref, wr_lo_ref, x1_ref, xp_ref, lg_ref):
    tm = x_ref.shape[0]
    att = ot_ref[...].reshape(ATT_Q_W, tm).astype(F32).T
    ya = _mm(att, wa_ref[...])
    o = of_ref[...] + ob_ref[...]
    ss = _mm(o * o, bdo_ref[...])
    dz = dz_ref[...].astype(F32)
    dn = o * lax.rsqrt(ss * (1.0 / DN_DIM) + EPS) * onw_ref[...] * (dz * jax.nn.sigmoid(dz))
    yb = _mm(dn, wb_ref[...])
    sg = sg_ref[...].astype(F32)
    mix = sg[:, :D_MODEL] * ya + sg[:, D_MODEL:] * yb
    x1 = x_ref[...] + _mm(mix, wo_ref[...])
    x1_ref[...] = x1
    ms = jnp.mean(x1 * x1, axis=-1, keepdims=True)
    xn = x1 * lax.rsqrt(ms + EPS) * n2_ref[...]
    xn_hi = xn.astype(BF16)
    xn_lo = (xn - xn_hi.astype(F32)).astype(BF16)
    xp_ref[0], xp_ref[1] = _pack_rows(xn)
    whi = wr_hi_ref[...]
    lg_ref[...] = (jnp.dot(xn_hi, whi, preferred_element_type=F32)
                   + jnp.dot(xn_lo, whi, preferred_element_type=F32)
                   + jnp.dot(xn_hi, wr_lo_ref[...], preferred_element_type=F32))


def _merge(ot, o_f, o_b, dz, sg, x2, wa, wb, wo, bdo, onw, n2, wr_hi, wr_lo):
    m = x2.shape[0]
    tm = 512 if m % 512 == 0 else 256
    row = lambda i: (i, 0)
    return pl.pallas_call(
        _merge_kernel, name="merge", grid=(m // tm,),
        in_specs=[
            pl.BlockSpec((ATT_HEADS, HEAD_DIM, tm), lambda i: (0, 0, i)),
            pl.BlockSpec((tm, o_f.shape[1]), row),
            pl.BlockSpec((tm, o_b.shape[1]), row),
            pl.BlockSpec((tm, dz.shape[1]), row),
            pl.BlockSpec((tm, sg.shape[1]), row),
            pl.BlockSpec((tm, D_MODEL), row),
            _full(wa.shape), _full(wb.shape), _full(wo.shape), _full(bdo.shape), _full(onw.shape),
            _full(n2.shape), _full(wr_hi.shape), _full(wr_lo.shape),
        ],
        out_specs=(pl.BlockSpec((tm, D_MODEL), row), pl.BlockSpec((2, tm, PACK_W), lambda i: (0, i, 0)),
                   pl.BlockSpec((tm, LANES), row)),
        out_shape=(jax.ShapeDtypeStruct((m, D_MODEL), F32), jax.ShapeDtypeStruct((2, m, PACK_W), jnp.uint32),
                   jax.ShapeDtypeStruct((m, LANES), F32)),
        compiler_params=_cparams(("parallel",)),
    )(ot, o_f, o_b, dz, sg, x2, wa, wb, wo, bdo, onw, n2, wr_hi, wr_lo)


def _route_kernel(lg_ref, info_ref, cnt_ref):
    @pl.when(pl.program_id(0) == 0)
    def _():
        cnt_ref[...] = jnp.zeros_like(cnt_ref)

    lg = lg_ref[...]
    lane_i = lax.broadcasted_iota(jnp.int32, lg.shape, 1)
    lane = lane_i.astype(F32)
    group_of = ((lane_i - ROUTER_OFF) // EXPERTS_PER_GROUP).astype(F32)
    neg = -jnp.inf

    def first_argmax(v):
        mx = jnp.max(v, axis=-1, keepdims=True)
        idx = jnp.min(jnp.where(v == mx, lane, float(LANES)), axis=-1, keepdims=True)
        return mx, idx

    gl = jnp.where(lane_i < N_GROUPS, lg, neg)
    gmax, gidx = first_argmax(gl)
    gval = 1.0 / jnp.sum(jnp.exp(gl - gmax), axis=-1, keepdims=True)
    is_exp = (lane_i >= ROUTER_OFF) & (lane_i < ROUTER_OFF + N_EXPERTS)
    sel = is_exp & (group_of == gidx)
    el = jnp.where(sel, lg, neg)
    m1, i1 = first_argmax(el)
    el2 = jnp.where(lane == i1, neg, el)
    m2, i2 = first_argmax(el2)
    r = jnp.exp(m2 - m1)
    w1 = gval / (1.0 + r)
    w2 = gval * r / (1.0 + r)
    tm = lg.shape[0]
    chosen = jnp.where((lane == i1) | (lane == i2), 1.0, 0.0)
    earlier = (lax.broadcasted_iota(jnp.int32, (tm, tm), 0) > lax.broadcasted_iota(jnp.int32, (tm, tm), 1))
    before = _mm(jnp.where(earlier, 1.0, 0.0), chosen) + cnt_ref[0:1, :]
    rank1 = jnp.sum(jnp.where(lane == i1, before, 0.0), axis=-1, keepdims=True)
    rank2 = jnp.sum(jnp.where(lane == i2, before, 0.0), axis=-1, keepdims=True)
    cnt_ref[...] = cnt_ref[...] + jnp.sum(chosen, axis=0, keepdims=True)
    info_ref[...] = (jnp.where(lane_i == 0, i1 - ROUTER_OFF, 0.0) + jnp.where(lane_i == 1, i2 - ROUTER_OFF, 0.0)
                     + jnp.where(lane_i == 2, w1, 0.0) + jnp.where(lane_i == 3, w2, 0.0)
                     + jnp.where(lane_i == 4, rank1, 0.0) + jnp.where(lane_i == 5, rank2, 0.0))


def _route(logits):
    m = logits.shape[0]
    tm = 1024 if m % 1024 == 0 else 256
    spec = pl.BlockSpec((tm, LANES), lambda i: (i, 0))
    return pl.pallas_call(
        _route_kernel, name="route", grid=(m // tm,), in_specs=[spec],
        out_specs=(spec, pl.BlockSpec((8, LANES), lambda i: (0, 0))),
        out_shape=(jax.ShapeDtypeStruct((m, LANES), F32), jax.ShapeDtypeStruct((8, LANES), F32)),
        compiler_params=_cparams(("arbitrary",)),
    )(logits)


def _pack_rows(x):
    bits = pltpu.bitcast(x.astype(BF16).astype(F32), jnp.uint32)
    half = x.shape[1] // 2
    word = (bits[:, :half] >> 16) | (bits[:, half:] & jnp.uint32(0xFFFF0000))
    return word[:, :PACK_W], word[:, PACK_W:]


def _unpack_rows(w0, w1):
    lo = [pltpu.bitcast(w << 16, F32) for w in (w0, w1)]
    hi = [pltpu.bitcast(w & jnp.uint32(0xFFFF0000), F32) for w in (w0, w1)]
    return jnp.concatenate(lo + hi, axis=1)


def _sc_mesh():
    return plsc.VectorSubcoreMesh(core_axis_name="c", subcore_axis_name="s")


def _sc_scatter_rows(src, idx, n_out, reps):
    s, width = src.shape
    nblk = s // SC_WINDOW

    @pl.kernel(out_type=jax.ShapeDtypeStruct((n_out, width), src.dtype), mesh=_sc_mesh(), scratch_types=[])
    def scatter_kernel(x_hbm, i_hbm, o_hbm):
        def body(x_vmem, i_vmem):
            pltpu.sync_copy(x_vmem, o_hbm.at[i_vmem.at[0]])

        pltpu.emit_pipeline(
            body, grid=(reps * nblk,),
            in_specs=[pl.BlockSpec((SC_WINDOW, width), index_map=lambda i: (i % nblk, 0)),
                      pl.BlockSpec((1, SC_WINDOW), index_map=lambda i: (0, i))],
            out_specs=[], core_axis_name=("c", "s"), dimension_semantics=(pltpu.PARALLEL,),
        )(x_hbm, i_hbm)

    return scatter_kernel(src, idx.reshape(1, reps * s))


def _sc_gather_rows(table, idx):
    k = idx.shape[0]
    width = table.shape[1]

    @pl.kernel(out_type=jax.ShapeDtypeStruct((k, width), table.dtype), mesh=_sc_mesh())
    def gather_kernel(x_hbm, i_hbm, o_hbm):
        def body(i_vmem, o_vmem):
            pltpu.sync_copy(x_hbm.at[i_vmem.at[0]], o_vmem)

        pltpu.emit_pipeline(
            body, grid=(k // SC_WINDOW,),
            in_specs=[pl.BlockSpec((1, SC_WINDOW), index_map=lambda i: (0, i))],
            out_specs=[pl.BlockSpec((SC_WINDOW, width), index_map=lambda i: (i, 0))],
            core_axis_name=("c", "s"), dimension_semantics=(pltpu.PARALLEL,),
        )(i_hbm, o_hbm)

    return gather_kernel(table, idx.reshape(1, k))


def _routing_tables(info, counts):
    m = info.shape[0]
    e = info[:, 0:2].astype(jnp.int32)
    rank = info[:, 4:6].astype(jnp.int32)
    cnt = counts[0, ROUTER_OFF:ROUTER_OFF + N_EXPERTS].astype(jnp.int32)
    tiles_e = (cnt + MOE_TILE - 1) // MOE_TILE
    tile_end = jnp.cumsum(tiles_e)
    slot_start = (tile_end - tiles_e) * MOE_TILE
    expert_ids = jnp.arange(N_EXPERTS, dtype=jnp.int32)
    pos = jnp.sum(jnp.where(e[:, :, None] == expert_ids, slot_start, 0), axis=-1) + rank
    n_tiles = 2 * m // MOE_TILE + N_EXPERTS
    tile_ids = jnp.arange(n_tiles, dtype=jnp.int32)
    tile_expert = jnp.sum(tile_ids[:, None] >= tile_end[None, :], axis=1)
    tile_expert = jnp.minimum(tile_expert, N_EXPERTS - 1).astype(jnp.int32)
    n_used = tile_end[-1:].astype(jnp.int32)
    has = tiles_e > 0
    run_index = jnp.cumsum(has.astype(jnp.int32)) - 1
    later = (expert_ids[None, :] > expert_ids[:, None]) & has[None, :]
    next_e = jnp.min(jnp.where(later, expert_ids[None, :], N_EXPERTS), axis=1)
    next_e = jnp.where(next_e < N_EXPERTS, next_e, -1)
    first = (tile_ids == jnp.take(tile_end - tiles_e, tile_expert)) & (tile_ids < n_used[0])
    sched = (tile_expert, n_used, jnp.take(next_e, tile_expert).astype(jnp.int32), first.astype(jnp.int32),
             (jnp.take(run_index, tile_expert) % 2).astype(jnp.int32))
    return pos, sched


def _experts_kernel(te_ref, nu_ref, nxt_ref, first_ref, slot_ref, xs_ref, wg_hbm, wu_hbm, wd_hbm, ys_ref,
                    wg_buf, wu_buf, wd_buf, sem):
    i = pl.program_id(0)

    def weight_copies(e, s):
        return (pltpu.make_async_copy(wg_hbm.at[e], wg_buf.at[s], sem.at[0, s]),
                pltpu.make_async_copy(wu_hbm.at[e], wu_buf.at[s], sem.at[1, s]),
                pltpu.make_async_copy(wd_hbm.at[e], wd_buf.at[s], sem.at[2, s]))

    @pl.when(i == 0)
    def _():
        for c in weight_copies(te_ref[0], slot_ref[0]):
            c.start()

    s = slot_ref[i]

    @pl.when(first_ref[i] == 1)
    def _():
        for c in weight_copies(te_ref[i], s):
            c.wait()

        @pl.when(nxt_ref[i] >= 0)
        def _():
            for c in weight_copies(nxt_ref[i], 1 - s):
                c.start()

    @pl.when(i < nu_ref[0])
    def _():
        x = _unpack_rows(xs_ref[0], xs_ref[1]).astype(BF16)
        hg = _mm(x, wg_buf[s])
        hu = _mm(x, wu_buf[s])
        y = _mm(hg * jax.nn.sigmoid(hg) * hu, wd_buf[s])
        ys_ref[0], ys_ref[1] = _pack_rows(y)


def _experts(xs, sched, wg, wu, wd):
    n_slots = xs.shape[1]
    slots = pl.BlockSpec((2, MOE_TILE, PACK_W), lambda i, *_: (0, i, 0))
    hbm = pl.BlockSpec(memory_space=pl.ANY)
    return pl.pallas_call(
        _experts_kernel, name="experts",
        grid_spec=pltpu.PrefetchScalarGridSpec(
            num_scalar_prefetch=len(sched), grid=(n_slots // MOE_TILE,),
            in_specs=[slots, hbm, hbm, hbm],
            out_specs=slots,
            scratch_shapes=[pltpu.VMEM((2,) + wg.shape[1:], wg.dtype), pltpu.VMEM((2,) + wu.shape[1:], wu.dtype),
                            pltpu.VMEM((2,) + wd.shape[1:], wd.dtype), pltpu.SemaphoreType.DMA((3, 2))],
        ),
        out_shape=jax.ShapeDtypeStruct(xs.shape, jnp.uint32),
        compiler_params=_cparams(("arbitrary",)),
    )(*sched, xs, wg, wu, wd)


def _combine_kernel(x1_ref, info_ref, yg_ref, fw_ref, y_ref):
    info = info_ref[...]
    x = (x1_ref[...] + info[:, 2:3] * _unpack_rows(yg_ref[0, 0], yg_ref[0, 1])
         + info[:, 3:4] * _unpack_rows(yg_ref[1, 0], yg_ref[1, 1]))
    ms = jnp.mean(x * x, axis=-1, keepdims=True)
    y_ref[...] = x * lax.rsqrt(ms + EPS) * fw_ref[...]


def _combine(x1, info, yg, fw):
    m = x1.shape[0]
    tm = 512 if m % 512 == 0 else 256
    row = lambda i: (i, 0)
    return pl.pallas_call(
        _combine_kernel, name="combine", grid=(m // tm,),
        in_specs=[pl.BlockSpec((tm, D_MODEL), row), pl.BlockSpec((tm, LANES), row),
                  pl.BlockSpec((2, 2, tm, PACK_W), lambda i: (0, 0, i, 0)), _full(fw.shape)],
        out_specs=pl.BlockSpec((tm, D_MODEL), row),
        out_shape=jax.ShapeDtypeStruct((m, D_MODEL), F32),
        compiler_params=_cparams(("parallel",)),
    )(x1, info, yg, fw)


def _moe_final(xp, x1, info, counts, wg, wu, wd, fw):
    m = x1.shape[0]
    pos, sched = _routing_tables(info, counts)
    n_slots = 2 * m + N_EXPERTS * MOE_TILE
    idx = jnp.concatenate([h * n_slots + pos[:, k] for k in range(2) for h in range(2)])
    xs = _sc_scatter_rows(xp.reshape(2 * m, PACK_W), idx, 2 * n_slots, 2)
    ys = _experts(xs.reshape(2, n_slots, PACK_W), sched, wg, wu, wd)
    yg = _sc_gather_rows(ys.reshape(2 * n_slots, PACK_W), idx)
    return _combine(x1, info, yg.reshape(2, 2, m, PACK_W), fw)


def _block_ones(n, blk):
    idx = np.arange(n)
    return jnp.asarray((idx[:, None] // blk == idx[None, :] // blk), dtype=BF16)


def _prepare(norm1_w, w_in, att_q_norm, att_k_norm, dn_conv_w, dn_a_log, dn_dt_bias, dn_out_norm,
             w_branch_att, w_branch_dn, w_out, norm2_w, moe_group_router, moe_expert_router,
             moe_w_gate, moe_w_up, moe_w_down, final_norm_w):
    w_in = w_in[0]
    o_q, o_k, o_v = 0, ATT_Q_W, ATT_Q_W + ATT_KV_W
    o_dq = o_v + ATT_KV_W
    o_dk, o_dv, o_dz = o_dq + DN_W, o_dq + 2 * DN_W, o_dq + 3 * DN_W
    o_da = o_dz + DN_W
    o_db = o_da + 2 * DN_HEADS
    o_ga = o_db + 2 * DN_HEADS
    o_gb = o_ga + D_MODEL

    deint = np.concatenate([np.arange(0, HEAD_DIM, 2), np.arange(1, HEAD_DIM, 2)])
    q_cols = np.concatenate([o_q + h * HEAD_DIM + deint for h in range(ATT_HEADS)])
    k_cols = np.concatenate([o_k + h * HEAD_DIM + deint for h in range(ATT_KV_HEADS)])
    v_cols = np.arange(o_v, o_v + ATT_KV_W)
    watt = w_in[:, np.concatenate([q_cols, k_cols, v_cols])].astype(BF16)
    qnw = jnp.tile(att_q_norm[0][deint], ATT_HEADS)[None, :]
    knw = jnp.tile(att_k_norm[0][deint], ATT_KV_HEADS)[None, :]

    hd = np.arange(DN_DIM)
    kq_cols = np.concatenate([np.concatenate([o_dk + h * DN_DIM + hd, o_dq + h * DN_DIM + hd]) for h in range(DN_HEADS)])
    vk_cols = np.concatenate([np.concatenate([o_dv + h * DN_DIM + hd, o_dk + h * DN_DIM + hd]) for h in range(DN_HEADS)])
    dn_cols = np.concatenate([kq_cols, vk_cols])
    wdn = w_in[:, dn_cols].astype(BF16)
    cw = jnp.concatenate([dn_conv_w[0][:, dn_cols - o_dq], jnp.zeros((8 - DN_CONV, dn_cols.size), F32)], axis=0)
    is_q = np.concatenate([np.tile(np.concatenate([np.zeros(DN_DIM), np.ones(DN_DIM)]), DN_HEADS), np.zeros(2 * DN_W)])
    is_v = np.concatenate([np.zeros(2 * DN_W), np.tile(np.concatenate([np.ones(DN_DIM), np.zeros(DN_DIM)]), DN_HEADS)])
    donorm = jnp.asarray(1.0 - is_v, F32)[None, :]
    scale = jnp.asarray(np.where(is_q > 0, DN_DIM ** -0.5, 1.0), F32)[None, :]

    wdz = w_in[:, o_dz:o_dz + DN_W].astype(BF16)
    onw = jnp.tile(dn_out_norm[0], DN_HEADS)[None, :]
    wb = w_branch_dn[0].astype(BF16)

    wab = jnp.zeros((D_MODEL, 2 * LANES), F32)
    for dirn in range(2):
        wab = wab.at[:, dirn * LANES:dirn * LANES + DN_HEADS].set(
            w_in[:, o_da + dirn * DN_HEADS:o_da + (dirn + 1) * DN_HEADS])
        wab = wab.at[:, dirn * LANES + DN_HEADS:dirn * LANES + 2 * DN_HEADS].set(
            w_in[:, o_db + dirn * DN_HEADS:o_db + (dirn + 1) * DN_HEADS])
    wab = wab.astype(BF16)
    wg = w_in[:, o_ga:o_gb + D_MODEL].astype(BF16)

    zeros8 = jnp.zeros((2, DN_HEADS), F32)
    ea = jnp.concatenate([jnp.exp(dn_a_log[0]), zeros8], axis=1)[:, None, :]
    bias = jnp.concatenate([dn_dt_bias[0], zeros8], axis=1)[:, None, :]


    wr = jnp.concatenate([moe_group_router[0], moe_expert_router[0],
                          jnp.zeros((D_MODEL, LANES - N_GROUPS - N_EXPERTS), F32)], axis=1)
    wr_hi = wr.astype(BF16)
    wr_lo = (wr - wr_hi.astype(F32)).astype(BF16)

    return dict(
        n1=norm1_w[0][None, :], watt=watt, wdn=wdn, wdz=wdz, wab=wab, wg=wg,
        bd_att=_block_ones(ATT_Q_W, HEAD_DIM), qnw=qnw, knw=knw,
        cw=cw, bd_dn=_block_ones(2 * LANES, DN_DIM), donorm=donorm, scale=scale, ea=ea, bias=bias,
        wa=w_branch_att[0].astype(BF16), wb=wb, wo=w_out[0].astype(BF16), bdo=_block_ones(DN_W, DN_DIM), onw=onw,
        n2=norm2_w[0][None, :], wr_hi=wr_hi, wr_lo=wr_lo,
        wge=moe_w_gate[0], wue=moe_w_up[0], wde=moe_w_down[0],
        fw=final_norm_w[None, :],
    )


def _rope_tables(seq):
    t = np.arange(seq)
    axis_dim = HEAD_DIM // 2
    inv = ROPE_THETA ** (-np.arange(0, axis_dim, 2, dtype=np.float32) / axis_dim)
    r = (t // GRID_W).astype(np.float32)
    c = (t % GRID_W).astype(np.float32)
    ang = np.concatenate([r[:, None] * inv, c[:, None] * inv], axis=-1).astype(np.float32)
    ang = jnp.asarray(ang)
    cos, sin = jnp.cos(ang), jnp.sin(ang)
    cos_t = jnp.tile(jnp.concatenate([cos, cos], axis=1), (1, LANES // HEAD_DIM))
    sin_t = jnp.tile(jnp.concatenate([-sin, sin], axis=1), (1, LANES // HEAD_DIM))
    return cos_t, sin_t


def _trunk(x, p):
    batch, seq, _ = x.shape
    x2 = x.reshape(batch * seq, D_MODEL)
    cos_t, sin_t = _rope_tables(seq)
    qt, kn, vt, dnpre, dz, dab, sg, kmx_tiles = _inproj(
        x2, seq, p["n1"], p["watt"], p["wdn"], p["wdz"], p["wab"], p["wg"], p["bd_att"], p["qnw"], p["knw"],
        cos_t, sin_t)
    kmx = jnp.max(kmx_tiles.reshape(batch, -1, 8, ATT_KV_HEADS, HEAD_DIM), axis=(1, 2, 4))
    kmx = jnp.broadcast_to(kmx[:, :, None, None], (batch, ATT_KV_HEADS, 8, LANES))
    ot = _attention(qt, kn, vt, kmx, batch, seq)
    kq, vk = _dnprep(dnpre, seq, p["cw"], p["bd_dn"], p["donorm"], p["scale"])
    o_f, o_b = _delta_rule(kq, vk, dab, p["ea"], p["bias"], batch, seq)
    x1, xp, logits = _merge(ot, o_f, o_b, dz, sg, x2, p["wa"], p["wb"], p["wo"], p["bdo"], p["onw"], p["n2"],
                            p["wr_hi"], p["wr_lo"])
    info, counts = _route(logits)
    y = _moe_final(xp, x1, info, counts, p["wge"], p["wue"], p["wde"], p["fw"])
    return y.reshape(batch, seq, D_MODEL)


def kernel(x_prompt, x_sample, norm1_w, w_in, att_q_norm, att_k_norm, dn_conv_w, dn_a_log, dn_dt_bias, dn_out_norm, w_branch_att, w_branch_dn, w_out, norm2_w, moe_group_router, moe_expert_router, moe_w_gate, moe_w_up, moe_w_down, final_norm_w):
    p = _prepare(norm1_w, w_in, att_q_norm, att_k_norm, dn_conv_w, dn_a_log, dn_dt_bias, dn_out_norm,
                 w_branch_att, w_branch_dn, w_out, norm2_w, moe_group_router, moe_expert_router,
                 moe_w_gate, moe_w_up, moe_w_down, final_norm_w)
    return (_trunk(x_prompt, p), _trunk(x_sample, p))
```

```python
import functools
import math

import numpy as np
import jax
import jax.numpy as jnp
from jax import lax
from jax.experimental import pallas as pl
from jax.experimental.pallas import tpu as pltpu
from jax.experimental.pallas import tpu_sc as plsc

F32 = jnp.float32
BF16 = jnp.bfloat16

D_MODEL = 1024
GRID_W = 64
EPS = 1e-6
ATT_HEADS = 8
ATT_KV_HEADS = 2
ATT_GROUP = ATT_HEADS // ATT_KV_HEADS
HEAD_DIM = 64
ROPE_THETA = 10000.0
DN_HEADS = 8
DN_DIM = 64
DN_CONV = 5
N_GROUPS = 4
EXPERTS_PER_GROUP = 8
N_EXPERTS = N_GROUPS * EXPERTS_PER_GROUP
EXPERT_FF = 256

ATT_Q_W = ATT_HEADS * HEAD_DIM
ATT_KV_W = ATT_KV_HEADS * HEAD_DIM
DN_W = DN_HEADS * DN_DIM
LANES = 128
DN_CHUNK = 128
ROUTER_OFF = N_GROUPS
PACK_W = D_MODEL // 4
SC_WINDOW = 128
MOE_TILE = 512
ATT_TQ = 1024
ATT_TK = 1024
VMEM_LIMIT = 52 * 1024 * 1024
LOG2E = math.log2(math.e)
SCORE_BOUND_MAX = 50.0
SCORE_BOUND_SLACK = 1.05


def _mm(a, b):
    return jnp.dot(a.astype(BF16), b.astype(BF16), preferred_element_type=F32)


def _cparams(sem):
    return pltpu.CompilerParams(dimension_semantics=sem, vmem_limit_bytes=VMEM_LIMIT)


def _full(shape):
    nd = len(shape)
    return pl.BlockSpec(shape, lambda *_: (0,) * nd, pipeline_mode=pl.Buffered(1))


def _rope(x, cos, sin_signed):
    n = x.shape[1]
    lane = lax.broadcasted_iota(jnp.int32, x.shape, 1)
    first = (lane % HEAD_DIM) < (HEAD_DIM // 2)
    partner = jnp.where(first, pltpu.roll(x, n - HEAD_DIM // 2, 1), pltpu.roll(x, HEAD_DIM // 2, 1))
    return x * cos + partner * sin_signed


def _inproj_kernel(x_ref, n1_ref, watt_ref, wdn_ref, wdz_ref, wab_ref, wg_ref, bd_ref, qnw_ref, knw_ref,
                   cos_ref, sin_ref, qt_ref, k_ref, vt_ref, dn_ref, dz_ref, dab_ref, sg_ref, kmx_ref):
    tm = x_ref.shape[0]
    x = x_ref[...]
    ms = jnp.mean(x * x, axis=-1, keepdims=True)
    xn = (x * lax.rsqrt(ms + EPS) * n1_ref[...]).astype(BF16)

    att = jnp.dot(xn, watt_ref[...], preferred_element_type=F32)
    aq = att[:, :ATT_Q_W]
    ak = att[:, ATT_Q_W:ATT_Q_W + ATT_KV_W]
    av = att[:, ATT_Q_W + ATT_KV_W:]
    bd = bd_ref[...]
    qss = _mm(aq * aq, bd)
    kss = _mm(ak * ak, bd[:ATT_KV_W, :ATT_KV_W])
    cos = cos_ref[...]
    sin = sin_ref[...]
    cos4 = jnp.concatenate([cos] * (ATT_Q_W // LANES), axis=1)
    sin4 = jnp.concatenate([sin] * (ATT_Q_W // LANES), axis=1)
    q = aq * lax.rsqrt(qss * (1.0 / HEAD_DIM) + EPS) * qnw_ref[...]
    k = ak * lax.rsqrt(kss * (1.0 / HEAD_DIM) + EPS) * knw_ref[...]
    q = _rope(q, cos4, sin4) * (HEAD_DIM ** -0.5 * LOG2E)
    kmx_ref[0] = jnp.broadcast_to(jnp.max(_mm(k * k, bd[:ATT_KV_W, :ATT_KV_W]), axis=0, keepdims=True),
                                  (8, ATT_KV_W))
    k = _rope(k, cos, sin)
    qt_ref[...] = q.T.reshape(ATT_HEADS, HEAD_DIM, tm).astype(BF16)
    k_ref[0] = k[:, :HEAD_DIM].astype(BF16)
    k_ref[1] = k[:, HEAD_DIM:].astype(BF16)
    vt_ref[...] = av.T.reshape(ATT_KV_HEADS, HEAD_DIM, tm).astype(BF16)

    dn_ref[...] = jnp.dot(xn, wdn_ref[...], preferred_element_type=F32).astype(BF16)
    dz_ref[...] = jnp.dot(xn, wdz_ref[...], preferred_element_type=F32).astype(BF16)
    ab = jnp.dot(xn, wab_ref[...], preferred_element_type=F32)
    dab_ref[0] = ab[:, 0:2 * DN_HEADS]
    dab_ref[1] = ab[:, LANES:LANES + 2 * DN_HEADS]
    sg_ref[...] = jax.nn.sigmoid(jnp.dot(xn, wg_ref[...], preferred_element_type=F32)).astype(BF16)


def _inproj(x2, seq, n1, watt, wdn, wdz, wab, wg, bd, qnw, knw, cos_t, sin_t):
    m = x2.shape[0]
    tm = min(512, seq)
    per_seq = seq // tm
    row = lambda i: (i, 0)
    out_shape = (
        jax.ShapeDtypeStruct((ATT_HEADS, HEAD_DIM, m), BF16),
        jax.ShapeDtypeStruct((ATT_KV_HEADS, m, HEAD_DIM), BF16),
        jax.ShapeDtypeStruct((ATT_KV_HEADS, HEAD_DIM, m), BF16),
        jax.ShapeDtypeStruct((m, 3 * DN_W), BF16),
        jax.ShapeDtypeStruct((m, DN_W), BF16),
        jax.ShapeDtypeStruct((2, m, 2 * DN_HEADS), F32),
        jax.ShapeDtypeStruct((m, 2 * D_MODEL), BF16),
        jax.ShapeDtypeStruct((m // tm, 8, ATT_KV_W), F32),
    )
    in_specs = [
        pl.BlockSpec((tm, D_MODEL), row),
        _full(n1.shape), _full(watt.shape), _full(wdn.shape), _full(wdz.shape), _full(wab.shape),
        _full(wg.shape), _full(bd.shape), _full(qnw.shape), _full(knw.shape),
        pl.BlockSpec((tm, LANES), lambda i: (i % per_seq, 0)),
        pl.BlockSpec((tm, LANES), lambda i: (i % per_seq, 0)),
    ]
    out_specs = (
        pl.BlockSpec((ATT_HEADS, HEAD_DIM, tm), lambda i: (0, 0, i)),
        pl.BlockSpec((ATT_KV_HEADS, tm, HEAD_DIM), lambda i: (0, i, 0)),
        pl.BlockSpec((ATT_KV_HEADS, HEAD_DIM, tm), lambda i: (0, 0, i)),
        pl.BlockSpec((tm, 3 * DN_W), row),
        pl.BlockSpec((tm, DN_W), row),
        pl.BlockSpec((2, tm, 2 * DN_HEADS), lambda i: (0, i, 0)),
        pl.BlockSpec((tm, 2 * D_MODEL), row),
        pl.BlockSpec((1, 8, ATT_KV_W), lambda i: (i, 0, 0)),
    )
    return pl.pallas_call(
        _inproj_kernel, name="inproj", grid=(m // tm,), in_specs=in_specs, out_specs=out_specs, out_shape=out_shape,
        compiler_params=_cparams(("parallel",)),
    )(x2, n1, watt, wdn, wdz, wab, wg, bd, qnw, knw, cos_t, sin_t)


def _attn_kernel(qt_ref, k_ref, vt_ref, kmx_ref, ot_ref, p_ref, *, tk, nk):
    tq = qt_ref.shape[2]
    q = jnp.concatenate([qt_ref[r] for r in range(ATT_GROUP)], axis=1)
    n = q.shape[1]
    qf = q.astype(F32)
    bound = jnp.sqrt(jnp.sum(qf * qf, axis=0, keepdims=True) * kmx_ref[0, 0][0:1, 0:1]) * SCORE_BOUND_SLACK
    small = jnp.max(bound) <= SCORE_BOUND_MAX
    ones_rows = 16

    def blocks(kb):
        off = pl.multiple_of(kb * tk, tk)
        return k_ref[0, pl.ds(off, tk), :], vt_ref[0, :, pl.ds(off, tk)]

    def write(out):
        for r in range(ATT_GROUP):
            ot_ref[r] = out[:, r * tq:(r + 1) * tq].astype(BF16)

    @pl.when(small)
    def _():
        def probs(kb):
            s = jnp.dot(blocks(kb)[0], q, preferred_element_type=F32)
            return jnp.exp2(s - bound).astype(BF16)

        def weighted(kb, p):
            vaug = jnp.concatenate([blocks(kb)[1], jnp.ones((ones_rows, tk), BF16)], axis=0)
            return jnp.dot(vaug, p, preferred_element_type=F32)

        acc0 = jnp.zeros((HEAD_DIM + ones_rows, n), F32)
        p_ref[0] = probs(0)
        if nk == 1:
            acc = acc0 + weighted(0, p_ref[0])
        else:
            def pair(kb, acc, last):
                p_ref[1] = probs(kb + 1)
                acc = acc + weighted(kb, p_ref[0])
                if not last:
                    p_ref[0] = probs(kb + 2)
                return acc + weighted(kb + 1, p_ref[1])

            acc = lax.fori_loop(0, nk // 2 - 1, lambda j, a: pair(2 * j, a, False), acc0)
            acc = pair(nk - 2, acc, True)
        write(acc[0:HEAD_DIM] / acc[HEAD_DIM:HEAD_DIM + 1])

    @pl.when(jnp.logical_not(small))
    def _():
        def body(kb, carry):
            m_run, l_run, acc = carry
            kblk, vblk = blocks(kb)
            s = jnp.dot(kblk, q, preferred_element_type=F32)
            m_new = jnp.maximum(m_run, jnp.max(s, axis=0, keepdims=True))
            p = jnp.exp2(s - m_new)
            alpha = jnp.exp2(m_run - m_new)
            l_new = alpha * l_run + jnp.sum(p, axis=0, keepdims=True)
            acc_new = alpha * acc + jnp.dot(vblk, p.astype(BF16), preferred_element_type=F32)
            return m_new, l_new, acc_new

        init = (jnp.full((1, n), -jnp.inf, F32), jnp.zeros((1, n), F32), jnp.zeros((HEAD_DIM, n), F32))
        _, l_fin, acc = lax.fori_loop(0, nk, body, init)
        write(acc / l_fin)


def _attention(qt, kn, vt, kmx, batch, seq):
    m = qt.shape[2]
    tq = min(ATT_TQ, seq)
    tk = min(ATT_TK, seq)
    nq = seq // tq
    qspec = pl.BlockSpec((ATT_GROUP, HEAD_DIM, tq), lambda b, g, i: (g, 0, b * nq + i))
    return pl.pallas_call(
        functools.partial(_attn_kernel, tk=tk, nk=seq // tk), name="attention",
        grid=(batch, ATT_KV_HEADS, nq),
        in_specs=[
            qspec,
            pl.BlockSpec((1, seq, HEAD_DIM), lambda b, g, i: (g, b, 0)),
            pl.BlockSpec((1, HEAD_DIM, seq), lambda b, g, i: (g, 0, b)),
            pl.BlockSpec((1, 1, 8, LANES), lambda b, g, i: (b, g, 0, 0)),
        ],
        out_specs=qspec,
        out_shape=jax.ShapeDtypeStruct((ATT_HEADS, HEAD_DIM, m), BF16),
        scratch_shapes=[pltpu.VMEM((2, tk, ATT_GROUP * tq), BF16)],
        compiler_params=_cparams(("parallel", "parallel", "parallel")),
    )(qt, kn, vt, kmx)


def _dnprep_kernel(cur_ref, prev_ref, next_ref, cw_ref, shift_ref, bd_ref, scale_ref, kq_ref, vk_ref,
                   buf_ref, *, seq):
    tm = cur_ref.shape[0]
    halo = prev_ref.shape[0]
    blk = shift_ref.shape[1] - 2 * halo
    i = pl.program_id(0)
    pos = (i * tm) % seq
    has_prev = pos != 0
    has_next = pos + tm != seq
    buf_ref[0:halo, :] = jnp.where(has_prev, prev_ref[...], jnp.zeros_like(prev_ref))
    buf_ref[halo:halo + tm, :] = cur_ref[...]
    buf_ref[halo + tm:, :] = jnp.where(has_next, next_ref[...], jnp.zeros_like(next_ref))
    cw = cw_ref[...]
    pad = DN_CONV // 2
    shift = shift_ref[...]
    rows = []
    for r in range(tm // blk):
        win = buf_ref[r * blk:(r + 1) * blk + 2 * halo, :]
        shifted = jnp.dot(shift, win, preferred_element_type=F32)
        acc = win[halo:halo + blk, :].astype(F32) * cw[pad:pad + 1, :]
        for n, j in enumerate(t for t in range(DN_CONV) if t != pad):
            acc = acc + shifted[n * blk:(n + 1) * blk, :] * cw[j:j + 1, :]
        rows.append(acc)
    y = jnp.concatenate(rows, axis=0) if len(rows) > 1 else rows[0]
    y = y * jax.nn.sigmoid(y)
    bd = bd_ref[...]
    w = bd.shape[0]
    qk = y[:, :2 * DN_W]
    qksq = qk * qk
    ss = jnp.concatenate([_mm(qksq[:, c * w:(c + 1) * w], bd) for c in range(2 * DN_W // w)], axis=1)
    qk = qk * lax.rsqrt(ss + EPS) * scale_ref[...]
    lo_half = lax.broadcasted_iota(jnp.int32, (tm, LANES), 1) < DN_DIM
    kq_cols, vk_cols = [], []
    for j in range(DN_HEADS // 2):
        sl = slice(j * LANES, (j + 1) * LANES)
        qb = qk[:, sl]
        kb = qk[:, DN_W + j * LANES:DN_W + (j + 1) * LANES]
        vb = y[:, 2 * DN_W + j * LANES:2 * DN_W + (j + 1) * LANES]
        qr, kr, vr = (pltpu.roll(t, DN_DIM, 1) for t in (qb, kb, vb))
        kq_cols += [jnp.where(lo_half, kb, qr), jnp.where(lo_half, kr, qb)]
        vk_cols += [jnp.where(lo_half, vb, kr), jnp.where(lo_half, vr, kb)]
    kq_ref[...] = jnp.concatenate(kq_cols, axis=1).astype(BF16)
    vk_ref[...] = jnp.concatenate(vk_cols, axis=1).astype(BF16)


def _dnprep(dnpre, seq, cw, bd, scale):
    m, width = dnpre.shape
    tm = min(512, seq)
    halo = 16
    hb = tm // halo
    last = m // halo - 1
    blk = min(128, tm)
    pad = DN_CONV // 2
    t = np.arange(blk)
    shift = np.zeros((DN_CONV - 1, blk, blk + 2 * halo), np.float32)
    for n, j in enumerate(x for x in range(DN_CONV) if x != pad):
        shift[n, t, halo + t + j - pad] = 1.0
    shift = jnp.asarray(shift.reshape((DN_CONV - 1) * blk, blk + 2 * halo), BF16)
    return pl.pallas_call(
        functools.partial(_dnprep_kernel, seq=seq), name="dnprep",
        grid=(m // tm,),
        in_specs=[
            pl.BlockSpec((tm, width), lambda i: (i, 0)),
            pl.BlockSpec((halo, width), lambda i: (jnp.maximum(i * hb - 1, 0), 0)),
            pl.BlockSpec((halo, width), lambda i: (jnp.minimum((i + 1) * hb, last), 0)),
            _full(cw.shape), _full(shift.shape), _full(bd.shape), _full(scale.shape),
        ],
        out_specs=(pl.BlockSpec((tm, 2 * DN_W), lambda i: (i, 0)),
                   pl.BlockSpec((tm, 2 * DN_W), lambda i: (i, 0))),
        out_shape=(jax.ShapeDtypeStruct((m, 2 * DN_W), BF16), jax.ShapeDtypeStruct((m, 2 * DN_W), BF16)),
        scratch_shapes=[pltpu.VMEM((tm + 2 * halo, width), BF16)],
        compiler_params=_cparams(("parallel",)),
    )(dnpre, dnpre, dnpre, cw, shift, bd, scale)


def _bd2(a, b):
    return jnp.concatenate([jnp.concatenate([a, jnp.zeros_like(b)], axis=1),
                            jnp.concatenate([jnp.zeros_like(a), b], axis=1)], axis=0)


def _softplus(x):
    return jnp.maximum(x, 0.0) + jnp.log1p(jnp.exp(-jnp.abs(x)))


def _delta_kernel(kqf_ref, kqb_ref, vkf_ref, vkb_ref, dabf_ref, dabb_ref, ea_ref, bias_ref, of_ref, ob_ref, s_ref):
    C = DN_CHUNK
    H = DN_DIM
    n = pl.program_id(1)

    @pl.when(n == 0)
    def _():
        s_ref[...] = jnp.zeros_like(s_ref)

    row = lax.broadcasted_iota(jnp.int32, (C, C), 0)
    col = lax.broadcasted_iota(jnp.int32, (C, C), 1)
    incl = (row >= col, row <= col)
    strict = (row > col, row < col)
    rowp = lax.broadcasted_iota(jnp.int32, (C, 2 * C), 0)
    colp = lax.broadcasted_iota(jnp.int32, (C, 2 * C), 1) % C
    lane = lax.broadcasted_iota(jnp.int32, (C, LANES), 1)
    lo_half = lane < H
    lane_s = lax.broadcasted_iota(jnp.int32, (H, LANES), 1)
    eye_p =jnp.where(rowp == colp, 1.0, 0.0)
    blk2 = rowp // 2 == colp // 2
    levels = []
    b = 2
    while b < C:
        levels.append((rowp // (2 * b) == colp // (2 * b)) & (rowp // b != colp // b))
        b *= 2

    kq_refs = (kqf_ref, kqb_ref)
    vk_refs = (vkf_ref, vkb_ref)
    dab_refs = (dabf_ref, dabb_ref)
    o_refs = (of_ref, ob_ref)

    gc_all, gl_all, beta_all = [], [], []
    for d in range(2):
        ab = dab_refs[d][0]
        g = -ea_ref[d] * _softplus(ab + bias_ref[d])
        beta_all.append(jax.nn.sigmoid(ab))
        tri = jnp.where(incl[d], 1.0, 0.0).astype(BF16)
        g_hi = g.astype(BF16)
        g_lo = (g - g_hi.astype(F32)).astype(BF16)
        gc_all.append(jnp.dot(tri, g_hi, preferred_element_type=F32)
                      + jnp.dot(tri, g_lo, preferred_element_type=F32))
        gl_all.append(jnp.sum(g, axis=0, keepdims=True))

    units = [(d, j) for d in range(2) for j in range(DN_HEADS // 2)]

    def halves(x):
        return x[:, :x.shape[1] // 2], x[:, x.shape[1] // 2:]

    def rhs1(t):
        z = jnp.zeros_like(t)
        return jnp.concatenate([jnp.concatenate([t, z], axis=1), jnp.concatenate([z, t], axis=1)], axis=0)

    kq_p, kqf, vkf, gccol, gcrow, bcol, gl, kT, decay = ({} for _ in range(9))
    for u in units:
        d, j = u
        sl = slice(2 * j * LANES, (2 * j + 2) * LANES)
        kq_p[u] = kq_refs[d][:, sl]
        kqf[u] = kq_p[u].astype(F32)
        vkf[u] = vk_refs[d][:, sl].astype(F32)
        gccol[u], gcrow[u], bcol[u], gl[u], kT[u], decay[u] = [], [], [], [], [], []
        for hh in range(2):
            h = 2 * j + hh
            gccol[u].append(jnp.broadcast_to(gc_all[d][:, h:h + 1], (C, LANES)))
            bcol[u].append(jnp.broadcast_to(beta_all[d][:, DN_HEADS + h:DN_HEADS + h + 1], (C, LANES)))
            gl[u].append(gl_all[d][:, h:h + 1])
            gcrow[u].append(gccol[u][hh].T)
            kT[u].append(kqf[u][:, hh * LANES:(hh + 1) * LANES].T[0:H, :])
            decay[u].append(jnp.where(incl[d], jnp.exp(jnp.minimum(gccol[u][hh] - gcrow[u][hh], 0.0)), 0.0))

    p1 = {u: _mm(kq_p[u], _bd2(rhs1(kT[u][0]), rhs1(kT[u][1]))) for u in units}
    a_p, qkd, x_p = {}, {}, {}
    for u in units:
        d = u[0]
        a_h, qk_h = [], []
        for hh in range(2):
            kk = p1[u][:, (2 * hh) * C:(2 * hh + 1) * C]
            qk = p1[u][:, (2 * hh + 1) * C:(2 * hh + 2) * C]
            a_h.append(jnp.where(strict[d], kk * decay[u][hh], 0.0) * bcol[u][hh])
            qk_h.append(qk * decay[u][hh])
        a_p[u] = jnp.concatenate(a_h, axis=1)
        qkd[u] = jnp.concatenate(qk_h, axis=1)
        x_p[u] = eye_p - jnp.where(blk2, a_p[u], 0.0)

    for msk in levels:
        g_p = {u: _mm(jnp.where(msk, a_p[u], 0.0), _bd2(*halves(x_p[u]))) for u in units}
        x_p = {u: x_p[u] - _mm(x_p[u], _bd2(*halves(g_p[u]))) for u in units}

    uw = {}
    for u in units:
        rhs2 = [vkf[u][:, hh * LANES:(hh + 1) * LANES] * bcol[u][hh]
                * jnp.where(lo_half, 1.0, jnp.exp(gccol[u][hh])) for hh in range(2)]
        uw[u] = _mm(x_p[u], _bd2(rhs2[0], rhs2[1]))
    ol = {u: _mm(qkd[u], _bd2(*halves(uw[u]))) for u in units}
    nw = {}
    for u in units:
        kdT = jnp.concatenate([kT[u][hh] * jnp.exp(gl[u][hh] - gcrow[u][hh][0:H, :]) for hh in range(2)], axis=0)
        nw[u] = _mm(kdT, uw[u])
    for u in units:
        d, j = u
        sl = slice(2 * j * LANES, (2 * j + 2) * LANES)
        zmult = jnp.concatenate(
            [jnp.where(lo_half, jnp.exp(gl[u][hh] - gccol[u][hh]), jnp.exp(gccol[u][hh])) for hh in range(2)], axis=1)
        z_p = kqf[u] * zmult - ol[u]
        nw_h = (nw[u][0:H, 0:LANES], nw[u][H:2 * H, LANES:2 * LANES])
        s_old = [s_ref[d, 2 * j + hh] for hh in range(2)]
        sblk = [jnp.concatenate([jnp.zeros((H, LANES), F32), s_old[hh]], axis=0) for hh in range(2)]
        r = _mm(jnp.concatenate([_bd2(*nw_h), z_p], axis=0), _bd2(sblk[0], sblk[1]))
        o_pair = r[2 * H:, :] + ol[u]
        o_refs[d][:, j * LANES:(j + 1) * LANES] = jnp.where(
            lo_half, o_pair[:, :LANES], pltpu.roll(o_pair[:, LANES:], H, 1))
        wks = (r[0:H, 0:LANES], r[H:2 * H, LANES:2 * LANES])
        for hh in range(2):
            s_new = jnp.exp(gl[u][hh]) * s_old[hh] + nw_h[hh] - wks[hh]
            s_ref[d, 2 * j + hh] = jnp.where(lane_s < H, s_new, 0.0)


def _delta_rule(kq, vk, dab, ea, bias, batch, seq):
    m = kq.shape[0]
    C = DN_CHUNK
    nc = seq // C
    width = kq.shape[1]
    fwd = lambda b, n: (b * nc + n, 0)
    bwd = lambda b, n: (b * nc + nc - 1 - n, 0)
    return pl.pallas_call(
        _delta_kernel, name="delta",
        grid=(batch, nc),
        in_specs=[
            pl.BlockSpec((C, width), fwd), pl.BlockSpec((C, width), bwd),
            pl.BlockSpec((C, width), fwd), pl.BlockSpec((C, width), bwd),
            pl.BlockSpec((1, C, 2 * DN_HEADS), lambda b, n: (0, b * nc + n, 0)),
            pl.BlockSpec((1, C, 2 * DN_HEADS), lambda b, n: (1, b * nc + nc - 1 - n, 0)),
            _full(ea.shape), _full(bias.shape),
        ],
        out_specs=(pl.BlockSpec((C, DN_W), fwd), pl.BlockSpec((C, DN_W), bwd)),
        out_shape=(jax.ShapeDtypeStruct((m, DN_W), F32), jax.ShapeDtypeStruct((m, DN_W), F32)),
        scratch_shapes=[pltpu.VMEM((2, DN_HEADS, DN_DIM, LANES), F32)],
        compiler_params=_cparams(("parallel", "arbitrary")),
    )(kq, kq, vk, vk, dab, dab, ea, bias)


def _merge_kernel(ot_ref, of_ref, ob_ref, dz_ref, sg_ref, x_ref, wa_ref, wb_ref, wo_ref, bdo_ref, onw_ref, n2_ref,
                  wr_hi_ref, wr_lo_ref, x1_ref, xp_ref, lg_ref):
    tm = x_ref.shape[0]
    att = ot_ref[...].reshape(ATT_Q_W, tm).astype(F32).T
    ya = _mm(att, wa_ref[...])
    o = of_ref[...] + ob_ref[...]
    ss = _mm(o * o, bdo_ref[...])
    dz = dz_ref[...].astype(F32)
    dn = o * lax.rsqrt(ss * (1.0 / DN_DIM) + EPS) * onw_ref[...] * (dz * jax.nn.sigmoid(dz))
    yb = _mm(dn, wb_ref[...])
    sg = sg_ref[...].astype(F32)
    mix = sg[:, :D_MODEL] * ya + sg[:, D_MODEL:] * yb
    x1 = x_ref[...] + _mm(mix, wo_ref[...])
    x1_ref[...] = x1
    ms = jnp.mean(x1 * x1, axis=-1, keepdims=True)
    xn = x1 * lax.rsqrt(ms + EPS) * n2_ref[...]
    xn_hi = xn.astype(BF16)
    xn_lo = (xn - xn_hi.astype(F32)).astype(BF16)
    xp_ref[0], xp_ref[1] = _pack_rows(xn)
    whi = wr_hi_ref[...]
    lg_ref[...] = (jnp.dot(xn_hi, whi, preferred_element_type=F32)
                   + jnp.dot(xn_lo, whi, preferred_element_type=F32)
                   + jnp.dot(xn_hi, wr_lo_ref[...], preferred_element_type=F32))


def _merge(ot, o_f, o_b, dz, sg, x2, wa, wb, wo, bdo, onw, n2, wr_hi, wr_lo):
    m = x2.shape[0]
    tm = 256
    row = lambda i: (i, 0)
    return pl.pallas_call(
        _merge_kernel, name="merge", grid=(m // tm,),
        in_specs=[
            pl.BlockSpec((ATT_HEADS, HEAD_DIM, tm), lambda i: (0, 0, i)),
            pl.BlockSpec((tm, o_f.shape[1]), row),
            pl.BlockSpec((tm, o_b.shape[1]), row),
            pl.BlockSpec((tm, dz.shape[1]), row),
            pl.BlockSpec((tm, sg.shape[1]), row),
            pl.BlockSpec((tm, D_MODEL), row),
            _full(wa.shape), _full(wb.shape), _full(wo.shape), _full(bdo.shape), _full(onw.shape),
            _full(n2.shape), _full(wr_hi.shape), _full(wr_lo.shape),
        ],
        out_specs=(pl.BlockSpec((tm, D_MODEL), row), pl.BlockSpec((2, tm, PACK_W), lambda i: (0, i, 0)),
                   pl.BlockSpec((tm, LANES), row)),
        out_shape=(jax.ShapeDtypeStruct((m, D_MODEL), F32), jax.ShapeDtypeStruct((2, m, PACK_W), jnp.uint32),
                   jax.ShapeDtypeStruct((m, LANES), F32)),
        compiler_params=_cparams(("parallel",)),
    )(ot, o_f, o_b, dz, sg, x2, wa, wb, wo, bdo, onw, n2, wr_hi, wr_lo)


def _route_kernel(lg_ref, earlier_ref, info_ref, cnt_ref):
    @pl.when(pl.program_id(0) == 0)
    def _():
        cnt_ref[...] = jnp.zeros_like(cnt_ref)

    lg = lg_ref[...]
    lane_i = lax.broadcasted_iota(jnp.int32, lg.shape, 1)
    lane = lane_i.astype(F32)
    group_of = ((lane_i - ROUTER_OFF) // EXPERTS_PER_GROUP).astype(F32)
    neg = -jnp.inf

    def first_argmax(v):
        mx = jnp.max(v, axis=-1, keepdims=True)
        idx = jnp.min(jnp.where(v == mx, lane, float(LANES)), axis=-1, keepdims=True)
        return mx, idx

    gl = jnp.where(lane_i < N_GROUPS, lg, neg)
    gmax, gidx = first_argmax(gl)
    gval = 1.0 / jnp.sum(jnp.exp(gl - gmax), axis=-1, keepdims=True)
    is_exp = (lane_i >= ROUTER_OFF) & (lane_i < ROUTER_OFF + N_EXPERTS)
    sel = is_exp & (group_of == gidx)
    el = jnp.where(sel, lg, neg)
    m1, i1 = first_argmax(el)
    el2 = jnp.where(lane == i1, neg, el)
    m2, i2 = first_argmax(el2)
    r = jnp.exp(m2 - m1)
    w1 = gval / (1.0 + r)
    w2 = gval * r / (1.0 + r)
    chosen =jnp.where((lane == i1) | (lane == i2), 1.0, 0.0)
    before = _mm(earlier_ref[...], chosen) + cnt_ref[0:1, :]
    rank1 = jnp.sum(jnp.where(lane == i1, before, 0.0), axis=-1, keepdims=True)
    rank2 = jnp.sum(jnp.where(lane == i2, before, 0.0), axis=-1, keepdims=True)
    cnt_ref[...] = cnt_ref[...] + jnp.sum(chosen, axis=0, keepdims=True)
    info_ref[...] = (jnp.where(lane_i == 0, i1 - ROUTER_OFF, 0.0) + jnp.where(lane_i == 1, i2 - ROUTER_OFF, 0.0)
                     + jnp.where(lane_i == 2, w1, 0.0) + jnp.where(lane_i == 3, w2, 0.0)
                     + jnp.where(lane_i == 4, rank1, 0.0) + jnp.where(lane_i == 5, rank2, 0.0))


def _route(logits):
    m = logits.shape[0]
    tm = 1024 if m % 1024 == 0 else 256
    spec = pl.BlockSpec((tm, LANES), lambda i: (i, 0))
    t = np.arange(tm)
    earlier = jnp.asarray(t[:, None] > t[None, :], BF16)
    return pl.pallas_call(
        _route_kernel, name="route", grid=(m // tm,), in_specs=[spec, _full(earlier.shape)],
        out_specs=(spec, pl.BlockSpec((8, LANES), lambda i: (0, 0))),
        out_shape=(jax.ShapeDtypeStruct((m, LANES), F32), jax.ShapeDtypeStruct((8, LANES), F32)),
        compiler_params=_cparams(("arbitrary",)),
    )(logits, earlier)


def _pack_rows(x):
    bits = pltpu.bitcast(x.astype(BF16).astype(F32), jnp.uint32)
    half = x.shape[1] // 2
    word = (bits[:, :half] >> 16) | (bits[:, half:] & jnp.uint32(0xFFFF0000))
    return word[:, :PACK_W], word[:, PACK_W:]


def _unpack_rows(w0, w1):
    lo = [pltpu.bitcast(w << 16, F32) for w in (w0, w1)]
    hi = [pltpu.bitcast(w & jnp.uint32(0xFFFF0000), F32) for w in (w0, w1)]
    return jnp.concatenate(lo + hi, axis=1)


def _sc_mesh():
    return plsc.VectorSubcoreMesh(core_axis_name="c", subcore_axis_name="s")


def _sc_scatter_rows(src, idx, n_out, reps):
    s, width = src.shape
    nblk = s // SC_WINDOW

    @pl.kernel(out_type=jax.ShapeDtypeStruct((n_out, width), src.dtype), mesh=_sc_mesh(), scratch_types=[])
    def scatter_kernel(x_hbm, i_hbm, o_hbm):
        def body(x_vmem, i_vmem):
            pltpu.sync_copy(x_vmem, o_hbm.at[i_vmem.at[0]])

        pltpu.emit_pipeline(
            body, grid=(reps * nblk,),
            in_specs=[pl.BlockSpec((SC_WINDOW, width), index_map=lambda i: (i % nblk, 0)),
                      pl.BlockSpec((1, SC_WINDOW), index_map=lambda i: (0, i))],
            out_specs=[], core_axis_name=("c", "s"), dimension_semantics=(pltpu.PARALLEL,),
        )(x_hbm, i_hbm)

    return scatter_kernel(src, idx.reshape(1, reps * s))


def _sc_gather_rows(table, idx):
    k = idx.shape[0]
    width = table.shape[1]

    @pl.kernel(out_type=jax.ShapeDtypeStruct((k, width), table.dtype), mesh=_sc_mesh())
    def gather_kernel(x_hbm, i_hbm, o_hbm):
        def body(i_vmem, o_vmem):
            pltpu.sync_copy(x_hbm.at[i_vmem.at[0]], o_vmem)

        pltpu.emit_pipeline(
            body, grid=(k // SC_WINDOW,),
            in_specs=[pl.BlockSpec((1, SC_WINDOW), index_map=lambda i: (0, i))],
            out_specs=[pl.BlockSpec((SC_WINDOW, width), index_map=lambda i: (i, 0))],
            core_axis_name=("c", "s"), dimension_semantics=(pltpu.PARALLEL,),
        )(i_hbm, o_hbm)

    return gather_kernel(table, idx.reshape(1, k))


def _routing_tables(info, counts):
    m = info.shape[0]
    e = info[:, 0:2].astype(jnp.int32)
    rank = info[:, 4:6].astype(jnp.int32)
    cnt = counts[0, ROUTER_OFF:ROUTER_OFF + N_EXPERTS].astype(jnp.int32)
    tiles_e = (cnt + MOE_TILE - 1) // MOE_TILE
    tile_end = jnp.cumsum(tiles_e)
    slot_start = (tile_end - tiles_e) * MOE_TILE
    expert_ids = jnp.arange(N_EXPERTS, dtype=jnp.int32)
    pos = jnp.sum(jnp.where(e[:, :, None] == expert_ids, slot_start, 0), axis=-1) + rank
    n_tiles = 2 * m // MOE_TILE + N_EXPERTS
    tile_ids = jnp.arange(n_tiles, dtype=jnp.int32)
    tile_expert = jnp.sum(tile_ids[:, None] >= tile_end[None, :], axis=1)
    tile_expert = jnp.minimum(tile_expert, N_EXPERTS - 1).astype(jnp.int32)
    n_used = tile_end[-1:].astype(jnp.int32)
    has = tiles_e > 0
    run_index = jnp.cumsum(has.astype(jnp.int32)) - 1
    later = (expert_ids[None, :] > expert_ids[:, None]) & has[None, :]
    next_e = jnp.min(jnp.where(later, expert_ids[None, :], N_EXPERTS), axis=1)
    next_e = jnp.where(next_e < N_EXPERTS, next_e, -1)
    first = (tile_ids == jnp.take(tile_end - tiles_e, tile_expert)) & (tile_ids < n_used[0])
    sched = (tile_expert, n_used, jnp.take(next_e, tile_expert).astype(jnp.int32), first.astype(jnp.int32),
             (jnp.take(run_index, tile_expert) % 2).astype(jnp.int32))
    return pos, sched


def _experts_kernel(te_ref, nu_ref, nxt_ref, first_ref, slot_ref, xs_ref, wg_hbm, wu_hbm, wd_hbm, ys_ref,
                    wg_buf, wu_buf, wd_buf, sem):
    i = pl.program_id(0)

    def weight_copies(e, s):
        return (pltpu.make_async_copy(wg_hbm.at[e], wg_buf.at[s], sem.at[0, s]),
                pltpu.make_async_copy(wu_hbm.at[e], wu_buf.at[s], sem.at[1, s]),
                pltpu.make_async_copy(wd_hbm.at[e], wd_buf.at[s], sem.at[2, s]))

    @pl.when(i == 0)
    def _():
        for c in weight_copies(te_ref[0], slot_ref[0]):
            c.start()

    s = slot_ref[i]

    @pl.when(first_ref[i] == 1)
    def _():
        for c in weight_copies(te_ref[i], s):
            c.wait()

        @pl.when(nxt_ref[i] >= 0)
        def _():
            for c in weight_copies(nxt_ref[i], 1 - s):
                c.start()

    @pl.when(i < nu_ref[0])
    def _():
        x = _unpack_rows(xs_ref[0], xs_ref[1]).astype(BF16)
        hg = _mm(x, wg_buf[s])
        hu = _mm(x, wu_buf[s])
        y = _mm(hg * jax.nn.sigmoid(hg) * hu, wd_buf[s])
        ys_ref[0], ys_ref[1] = _pack_rows(y)


def _experts(xs, sched, wg, wu, wd):
    n_slots = xs.shape[1]
    slots = pl.BlockSpec((2, MOE_TILE, PACK_W), lambda i, *_: (0, i, 0))
    hbm = pl.BlockSpec(memory_space=pl.ANY)
    return pl.pallas_call(
        _experts_kernel, name="experts",
        grid_spec=pltpu.PrefetchScalarGridSpec(
            num_scalar_prefetch=len(sched), grid=(n_slots // MOE_TILE,),
            in_specs=[slots, hbm, hbm, hbm],
            out_specs=slots,
            scratch_shapes=[pltpu.VMEM((2,) + wg.shape[1:], wg.dtype), pltpu.VMEM((2,) + wu.shape[1:], wu.dtype),
                            pltpu.VMEM((2,) + wd.shape[1:], wd.dtype), pltpu.SemaphoreType.DMA((3, 2))],
        ),
        out_shape=jax.ShapeDtypeStruct(xs.shape, jnp.uint32),
        compiler_params=_cparams(("arbitrary",)),
    )(*sched, xs, wg, wu, wd)


def _combine_kernel(x1_ref, info_ref, yg_ref, fw_ref, y_ref):
    info = info_ref[...]
    x = (x1_ref[...] + info[:, 2:3] * _unpack_rows(yg_ref[0, 0], yg_ref[0, 1])
         + info[:, 3:4] * _unpack_rows(yg_ref[1, 0], yg_ref[1, 1]))
    ms = jnp.mean(x * x, axis=-1, keepdims=True)
    y_ref[...] = x * lax.rsqrt(ms + EPS) * fw_ref[...]


def _combine(x1, info, yg, fw):
    m = x1.shape[0]
    tm = 512 if m % 512 == 0 else 256
    row = lambda i: (i, 0)
    return pl.pallas_call(
        _combine_kernel, name="combine", grid=(m // tm,),
        in_specs=[pl.BlockSpec((tm, D_MODEL), row), pl.BlockSpec((tm, LANES), row),
                  pl.BlockSpec((2, 2, tm, PACK_W), lambda i: (0, 0, i, 0)), _full(fw.shape)],
        out_specs=pl.BlockSpec((tm, D_MODEL), row),
        out_shape=jax.ShapeDtypeStruct((m, D_MODEL), F32),
        compiler_params=_cparams(("parallel",)),
    )(x1, info, yg, fw)


def _moe_final(xp, x1, info, counts, wg, wu, wd, fw):
    m = x1.shape[0]
    pos, sched = _routing_tables(info, counts)
    n_slots = 2 * m + N_EXPERTS * MOE_TILE
    idx = jnp.concatenate([h * n_slots + pos[:, k] for k in range(2) for h in range(2)])
    xs = _sc_scatter_rows(xp.reshape(2 * m, PACK_W), idx, 2 * n_slots, 2)
    ys = _experts(xs.reshape(2, n_slots, PACK_W), sched, wg, wu, wd)
    yg = _sc_gather_rows(ys.reshape(2 * n_slots, PACK_W), idx)
    return _combine(x1, info, yg.reshape(2, 2, m, PACK_W), fw)


def _block_ones(n, blk):
    idx = np.arange(n)
    return jnp.asarray((idx[:, None] // blk == idx[None, :] // blk), dtype=BF16)


def _prepare(norm1_w, w_in, att_q_norm, att_k_norm, dn_conv_w, dn_a_log, dn_dt_bias, dn_out_norm,
             w_branch_att, w_branch_dn, w_out, norm2_w, moe_group_router, moe_expert_router,
             moe_w_gate, moe_w_up, moe_w_down, final_norm_w):
    w_in = w_in[0]
    o_q, o_k, o_v = 0, ATT_Q_W, ATT_Q_W + ATT_KV_W
    o_dq = o_v + ATT_KV_W
    o_dz = o_dq + 3 * DN_W
    o_da = o_dz + DN_W
    o_db = o_da + 2 * DN_HEADS
    o_ga = o_db + 2 * DN_HEADS
    o_gb = o_ga + D_MODEL

    def deinterleave(w, heads):
        lead = w.shape[:-1]
        w = w.reshape(lead + (heads, HEAD_DIM // 2, 2))
        return jnp.swapaxes(w, -1, -2).reshape(lead + (heads * HEAD_DIM,))

    watt = jnp.concatenate([deinterleave(w_in[:, o_q:o_k], ATT_HEADS), deinterleave(w_in[:, o_k:o_v], ATT_KV_HEADS),
                            w_in[:, o_v:o_dq]], axis=1).astype(BF16)
    qnw = jnp.tile(deinterleave(att_q_norm[0], 1), ATT_HEADS)[None, :]
    knw = jnp.tile(deinterleave(att_k_norm[0], 1), ATT_KV_HEADS)[None, :]

    wdn =w_in[:, o_dq:o_dz].astype(BF16)
    cw = jnp.concatenate([dn_conv_w[0], jnp.zeros((8 - DN_CONV, 3 * DN_W), F32)], axis=0)
    scale = jnp.asarray(np.concatenate([np.full(DN_W, DN_DIM ** -0.5), np.ones(DN_W)]), F32)[None, :]

    wdz = w_in[:, o_dz:o_dz + DN_W].astype(BF16)
    onw = jnp.tile(dn_out_norm[0], DN_HEADS)[None, :]
    wb = w_branch_dn[0].astype(BF16)

    wab = jnp.zeros((D_MODEL, 2 * LANES), F32)
    for dirn in range(2):
        wab = wab.at[:, dirn * LANES:dirn * LANES + DN_HEADS].set(
            w_in[:, o_da + dirn * DN_HEADS:o_da + (dirn + 1) * DN_HEADS])
        wab = wab.at[:, dirn * LANES + DN_HEADS:dirn * LANES + 2 * DN_HEADS].set(
            w_in[:, o_db + dirn * DN_HEADS:o_db + (dirn + 1) * DN_HEADS])
    wab = wab.astype(BF16)
    wg = w_in[:, o_ga:o_gb + D_MODEL].astype(BF16)

    zeros8 = jnp.zeros((2, DN_HEADS), F32)
    ea = jnp.concatenate([jnp.exp(dn_a_log[0]), zeros8], axis=1)[:, None, :]
    bias = jnp.concatenate([dn_dt_bias[0], zeros8], axis=1)[:, None, :]


    wr = jnp.concatenate([moe_group_router[0], moe_expert_router[0],
                          jnp.zeros((D_MODEL, LANES - N_GROUPS - N_EXPERTS), F32)], axis=1)
    wr_hi = wr.astype(BF16)
    wr_lo = (wr - wr_hi.astype(F32)).astype(BF16)

    return dict(
        n1=norm1_w[0][None, :], watt=watt, wdn=wdn, wdz=wdz, wab=wab, wg=wg,
        bd_att=_block_ones(ATT_Q_W, HEAD_DIM), qnw=qnw, knw=knw,
        cw=cw, bd_dn=_block_ones(2 * LANES, DN_DIM), scale=scale, ea=ea, bias=bias,
        wa=w_branch_att[0].astype(BF16), wb=wb, wo=w_out[0].astype(BF16), bdo=_block_ones(DN_W, DN_DIM), onw=onw,
        n2=norm2_w[0][None, :], wr_hi=wr_hi, wr_lo=wr_lo,
        wge=moe_w_gate[0], wue=moe_w_up[0], wde=moe_w_down[0],
        fw=final_norm_w[None, :],
    )


def _rope_tables(seq):
    t = np.arange(seq)
    axis_dim = HEAD_DIM // 2
    inv = ROPE_THETA ** (-np.arange(0, axis_dim, 2, dtype=np.float32) / axis_dim)
    r = (t // GRID_W).astype(np.float32)
    c = (t % GRID_W).astype(np.float32)
    ang = np.concatenate([r[:, None] * inv, c[:, None] * inv], axis=-1).astype(np.float32)
    ang = jnp.asarray(ang)
    cos, sin = jnp.cos(ang), jnp.sin(ang)
    cos_t = jnp.tile(jnp.concatenate([cos, cos], axis=1), (1, LANES // HEAD_DIM))
    sin_t = jnp.tile(jnp.concatenate([-sin, sin], axis=1), (1, LANES // HEAD_DIM))
    return cos_t, sin_t


def _trunk(x, p):
    batch, seq, _ = x.shape
    x2 = x.reshape(batch * seq, D_MODEL)
    cos_t, sin_t = _rope_tables(seq)
    qt, kn, vt, dnpre, dz, dab, sg, kmx_tiles = _inproj(
        x2, seq, p["n1"], p["watt"], p["wdn"], p["wdz"], p["wab"], p["wg"], p["bd_att"], p["qnw"], p["knw"],
        cos_t, sin_t)
    kmx = jnp.max(kmx_tiles.reshape(batch, -1, 8, ATT_KV_HEADS, HEAD_DIM), axis=(1, 2, 4))
    kmx = jnp.broadcast_to(kmx[:, :, None, None], (batch, ATT_KV_HEADS, 8, LANES))
    ot = _attention(qt, kn, vt, kmx, batch, seq)
    kq, vk = _dnprep(dnpre, seq, p["cw"], p["bd_dn"], p["scale"])
    o_f, o_b = _delta_rule(kq, vk, dab, p["ea"], p["bias"], batch, seq)
    x1, xp, logits = _merge(ot, o_f, o_b, dz, sg, x2, p["wa"], p["wb"], p["wo"], p["bdo"], p["onw"], p["n2"],
                            p["wr_hi"], p["wr_lo"])
    info, counts = _route(logits)
    y = _moe_final(xp, x1, info, counts, p["wge"], p["wue"], p["wde"], p["fw"])
    return y.reshape(batch, seq, D_MODEL)


def kernel(x_prompt, x_sample, norm1_w, w_in, att_q_norm, att_k_norm, dn_conv_w, dn_a_log, dn_dt_bias, dn_out_norm, w_branch_att, w_branch_dn, w_out, norm2_w, moe_group_router, moe_expert_router, moe_w_gate, moe_w_up, moe_w_down, final_norm_w):
    p = _prepare(norm1_w, w_in, att_q_norm, att_k_norm, dn_conv_w, dn_a_log, dn_dt_bias, dn_out_norm,
                 w_branch_att, w_branch_dn, w_out, norm2_w, moe_group_router, moe_expert_router,
                 moe_w_gate, moe_w_up, moe_w_down, final_norm_w)
    return (_trunk(x_prompt, p), _trunk(x_sample, p))
```

```python
import functools
import math

import numpy as np
import jax
import jax.numpy as jnp
from jax import lax
from jax.experimental import pallas as pl
from jax.experimental.pallas import tpu as pltpu
from jax.experimental.pallas import tpu_sc as plsc

F32 = jnp.float32
BF16 = jnp.bfloat16

D_MODEL = 1024
GRID_W = 64
EPS = 1e-6
ATT_HEADS = 8
ATT_KV_HEADS = 2
ATT_GROUP = ATT_HEADS // ATT_KV_HEADS
HEAD_DIM = 64
ROPE_THETA = 10000.0
DN_HEADS = 8
DN_DIM = 64
DN_CONV = 5
N_GROUPS = 4
EXPERTS_PER_GROUP = 8
N_EXPERTS = N_GROUPS * EXPERTS_PER_GROUP
EXPERT_FF = 256

ATT_Q_W = ATT_HEADS * HEAD_DIM
ATT_KV_W = ATT_KV_HEADS * HEAD_DIM
DN_W = DN_HEADS * DN_DIM
LANES = 128
DN_CHUNK = 128
ROUTER_OFF = N_GROUPS
PACK_W = D_MODEL // 4
SC_WINDOW = 128
MOE_TILE = 512
ATT_TQ = 1024
ATT_TK = 1024
VMEM_LIMIT = 52 * 1024 * 1024
LOG2E = math.log2(math.e)
SCORE_BOUND_MAX = 50.0
SCORE_BOUND_SLACK = 1.05


def _mm(a, b):
    return jnp.dot(a.astype(BF16), b.astype(BF16), preferred_element_type=F32)


def _cparams(sem):
    return pltpu.CompilerParams(dimension_semantics=sem, vmem_limit_bytes=VMEM_LIMIT)


def _full(shape):
    nd = len(shape)
    return pl.BlockSpec(shape, lambda *_: (0,) * nd, pipeline_mode=pl.Buffered(1))


def _rope(x, cos, sin_signed):
    n = x.shape[1]
    lane = lax.broadcasted_iota(jnp.int32, x.shape, 1)
    first = (lane % HEAD_DIM) < (HEAD_DIM // 2)
    partner = jnp.where(first, pltpu.roll(x, n - HEAD_DIM // 2, 1), pltpu.roll(x, HEAD_DIM // 2, 1))
    return x * cos + partner * sin_signed


def _inproj_kernel(x_ref, n1_ref, watt_ref, wdn_ref, wdz_ref, wab_ref, wg_ref, bd_ref, qnw_ref, knw_ref,
                   cos_ref, sin_ref, qt_ref, k_ref, vt_ref, dn_ref, dz_ref, dab_ref, sg_ref, kmx_ref):
    tm = x_ref.shape[0]
    x = x_ref[...]
    ms = jnp.mean(x * x, axis=-1, keepdims=True)
    xn = (x * lax.rsqrt(ms + EPS) * n1_ref[...]).astype(BF16)

    att = jnp.dot(xn, watt_ref[...], preferred_element_type=F32)
    aq = att[:, :ATT_Q_W]
    ak = att[:, ATT_Q_W:ATT_Q_W + ATT_KV_W]
    av = att[:, ATT_Q_W + ATT_KV_W:]
    bd = bd_ref[...]
    qss = _mm(aq * aq, bd)
    kss = _mm(ak * ak, bd[:ATT_KV_W, :ATT_KV_W])
    cos = cos_ref[...]
    sin = sin_ref[...]
    cos4 = jnp.concatenate([cos] * (ATT_Q_W // LANES), axis=1)
    sin4 = jnp.concatenate([sin] * (ATT_Q_W // LANES), axis=1)
    q = aq * lax.rsqrt(qss * (1.0 / HEAD_DIM) + EPS) * qnw_ref[...]
    k = ak * lax.rsqrt(kss * (1.0 / HEAD_DIM) + EPS) * knw_ref[...]
    q = _rope(q, cos4, sin4) * (HEAD_DIM ** -0.5 * LOG2E)
    kmx_ref[0] = jnp.broadcast_to(jnp.max(_mm(k * k, bd[:ATT_KV_W, :ATT_KV_W]), axis=0, keepdims=True),
                                  (8, ATT_KV_W))
    k = _rope(k, cos, sin)
    qt_ref[...] = q.T.reshape(ATT_HEADS, HEAD_DIM, tm).astype(BF16)
    k_ref[0] = k[:, :HEAD_DIM].astype(BF16)
    k_ref[1] = k[:, HEAD_DIM:].astype(BF16)
    vt_ref[...] = av.T.reshape(ATT_KV_HEADS, HEAD_DIM, tm).astype(BF16)

    dn_ref[...] = jnp.dot(xn, wdn_ref[...], preferred_element_type=F32).astype(BF16)
    dz_ref[...] = jnp.dot(xn, wdz_ref[...], preferred_element_type=F32).astype(BF16)
    ab = jnp.dot(xn, wab_ref[...], preferred_element_type=F32)
    dab_ref[0] = ab[:, 0:2 * DN_HEADS]
    dab_ref[1] = ab[:, LANES:LANES + 2 * DN_HEADS]
    sg_ref[...] = jax.nn.sigmoid(jnp.dot(xn, wg_ref[...], preferred_element_type=F32)).astype(BF16)


def _inproj(x2, seq, n1, watt, wdn, wdz, wab, wg, bd, qnw, knw, cos_t, sin_t):
    m = x2.shape[0]
    tm = min(512, seq)
    per_seq = seq // tm
    row = lambda i: (i, 0)
    out_shape = (
        jax.ShapeDtypeStruct((ATT_HEADS, HEAD_DIM, m), BF16),
        jax.ShapeDtypeStruct((ATT_KV_HEADS, m, HEAD_DIM), BF16),
        jax.ShapeDtypeStruct((ATT_KV_HEADS, HEAD_DIM, m), BF16),
        jax.ShapeDtypeStruct((m, 3 * DN_W), BF16),
        jax.ShapeDtypeStruct((m, DN_W), BF16),
        jax.ShapeDtypeStruct((2, m, 2 * DN_HEADS), F32),
        jax.ShapeDtypeStruct((m, 2 * D_MODEL), BF16),
        jax.ShapeDtypeStruct((m // tm, 8, ATT_KV_W), F32),
    )
    in_specs = [
        pl.BlockSpec((tm, D_MODEL), row),
        _full(n1.shape), _full(watt.shape), _full(wdn.shape), _full(wdz.shape), _full(wab.shape),
        _full(wg.shape), _full(bd.shape), _full(qnw.shape), _full(knw.shape),
        pl.BlockSpec((tm, LANES), lambda i: (i % per_seq, 0)),
        pl.BlockSpec((tm, LANES), lambda i: (i % per_seq, 0)),
    ]
    out_specs = (
        pl.BlockSpec((ATT_HEADS, HEAD_DIM, tm), lambda i: (0, 0, i)),
        pl.BlockSpec((ATT_KV_HEADS, tm, HEAD_DIM), lambda i: (0, i, 0)),
        pl.BlockSpec((ATT_KV_HEADS, HEAD_DIM, tm), lambda i: (0, 0, i)),
        pl.BlockSpec((tm, 3 * DN_W), row),
        pl.BlockSpec((tm, DN_W), row),
        pl.BlockSpec((2, tm, 2 * DN_HEADS), lambda i: (0, i, 0)),
        pl.BlockSpec((tm, 2 * D_MODEL), row),
        pl.BlockSpec((1, 8, ATT_KV_W), lambda i: (i, 0, 0)),
    )
    return pl.pallas_call(
        _inproj_kernel, name="inproj", grid=(m // tm,), in_specs=in_specs, out_specs=out_specs, out_shape=out_shape,
        compiler_params=_cparams(("parallel",)),
    )(x2, n1, watt, wdn, wdz, wab, wg, bd, qnw, knw, cos_t, sin_t)


def _attn_kernel(qt_ref, k_ref, vt_ref, kmx_ref, ot_ref, p_ref, *, tk, nk):
    tq = qt_ref.shape[2]
    q = jnp.concatenate([qt_ref[r] for r in range(ATT_GROUP)], axis=1)
    n = q.shape[1]
    qf = q.astype(F32)
    kmx = jnp.max(kmx_ref[0], axis=0)
    head_lanes = lax.broadcasted_iota(jnp.int32, kmx.shape, 1) // HEAD_DIM == pl.program_id(1)
    kmx = jnp.max(jnp.where(head_lanes, kmx, 0.0), axis=1, keepdims=True)[0:1, :]
    bound = jnp.sqrt(jnp.sum(qf * qf, axis=0, keepdims=True) * kmx) * SCORE_BOUND_SLACK
    small = jnp.max(bound) <= SCORE_BOUND_MAX
    ones_rows = 16

    def blocks(kb):
        off = pl.multiple_of(kb * tk, tk)
        return k_ref[0, pl.ds(off, tk), :], vt_ref[0, :, pl.ds(off, tk)]

    def write(out):
        for r in range(ATT_GROUP):
            ot_ref[r] = out[:, r * tq:(r + 1) * tq].astype(BF16)

    @pl.when(small)
    def _():
        def probs(kb):
            s = jnp.dot(blocks(kb)[0], q, preferred_element_type=F32)
            return jnp.exp2(s - bound).astype(BF16)

        def weighted(kb, p):
            vaug = jnp.concatenate([blocks(kb)[1], jnp.ones((ones_rows, tk), BF16)], axis=0)
            return jnp.dot(vaug, p, preferred_element_type=F32)

        acc0 = jnp.zeros((HEAD_DIM + ones_rows, n), F32)
        p_ref[0] = probs(0)
        if nk == 1:
            acc = acc0 + weighted(0, p_ref[0])
        else:
            def pair(kb, acc, last):
                p_ref[1] = probs(kb + 1)
                acc = acc + weighted(kb, p_ref[0])
                if not last:
                    p_ref[0] = probs(kb + 2)
                return acc + weighted(kb + 1, p_ref[1])

            acc = lax.fori_loop(0, nk // 2 - 1, lambda j, a: pair(2 * j, a, False), acc0)
            acc = pair(nk - 2, acc, True)
        write(acc[0:HEAD_DIM] / acc[HEAD_DIM:HEAD_DIM + 1])

    @pl.when(jnp.logical_not(small))
    def _():
        def body(kb, carry):
            m_run, l_run, acc = carry
            kblk, vblk = blocks(kb)
            s = jnp.dot(kblk, q, preferred_element_type=F32)
            m_new = jnp.maximum(m_run, jnp.max(s, axis=0, keepdims=True))
            p = jnp.exp2(s - m_new)
            alpha = jnp.exp2(m_run - m_new)
            l_new = alpha * l_run + jnp.sum(p, axis=0, keepdims=True)
            acc_new = alpha * acc + jnp.dot(vblk, p.astype(BF16), preferred_element_type=F32)
            return m_new, l_new, acc_new

        init = (jnp.full((1, n), -jnp.inf, F32), jnp.zeros((1, n), F32), jnp.zeros((HEAD_DIM, n), F32))
        _, l_fin, acc = lax.fori_loop(0, nk, body, init)
        write(acc / l_fin)


def _attention(qt, kn, vt, kmx, batch, seq):
    m = qt.shape[2]
    tq = min(ATT_TQ, seq)
    tk = min(ATT_TK, seq)
    nq = seq // tq
    qspec = pl.BlockSpec((ATT_GROUP, HEAD_DIM, tq), lambda b, g, i: (g, 0, b * nq + i))
    return pl.pallas_call(
        functools.partial(_attn_kernel, tk=tk, nk=seq // tk), name="attention",
        grid=(batch, ATT_KV_HEADS, nq),
        in_specs=[
            qspec,
            pl.BlockSpec((1, seq, HEAD_DIM), lambda b, g, i: (g, b, 0)),
            pl.BlockSpec((1, HEAD_DIM, seq), lambda b, g, i: (g, 0, b)),
            pl.BlockSpec((1,) + kmx.shape[1:], lambda b, g, i: (b, 0, 0, 0)),
        ],
        out_specs=qspec,
        out_shape=jax.ShapeDtypeStruct((ATT_HEADS, HEAD_DIM, m), BF16),
        scratch_shapes=[pltpu.VMEM((2, tk, ATT_GROUP * tq), BF16)],
        compiler_params=_cparams(("parallel", "parallel", "parallel")),
    )(qt, kn, vt, kmx)


def _dnprep_kernel(cur_ref, prev_ref, next_ref, cw_ref, shift_ref, bd_ref, scale_ref, kq_ref, vk_ref,
                   buf_ref, *, seq):
    tm = cur_ref.shape[0]
    halo = prev_ref.shape[0]
    blk = shift_ref.shape[1] - 2 * halo
    i = pl.program_id(0)
    pos = (i * tm) % seq
    has_prev = pos != 0
    has_next = pos + tm != seq
    buf_ref[0:halo, :] = jnp.where(has_prev, prev_ref[...], jnp.zeros_like(prev_ref))
    buf_ref[halo:halo + tm, :] = cur_ref[...]
    buf_ref[halo + tm:, :] = jnp.where(has_next, next_ref[...], jnp.zeros_like(next_ref))
    cw = cw_ref[...]
    pad = DN_CONV // 2
    shift = shift_ref[...]
    rows = []
    for r in range(tm // blk):
        win = buf_ref[r * blk:(r + 1) * blk + 2 * halo, :]
        shifted = jnp.dot(shift, win, preferred_element_type=F32)
        acc = win[halo:halo + blk, :].astype(F32) * cw[pad:pad + 1, :]
        for n, j in enumerate(t for t in range(DN_CONV) if t != pad):
            acc = acc + shifted[n * blk:(n + 1) * blk, :] * cw[j:j + 1, :]
        rows.append(acc)
    y = jnp.concatenate(rows, axis=0) if len(rows) > 1 else rows[0]
    y = y * jax.nn.sigmoid(y)
    bd = bd_ref[...]
    w = bd.shape[0]
    qk = y[:, :2 * DN_W]
    qksq = qk * qk
    ss = jnp.concatenate([_mm(qksq[:, c * w:(c + 1) * w], bd) for c in range(2 * DN_W // w)], axis=1)
    qk = qk * lax.rsqrt(ss + EPS) * scale_ref[...]
    lo_half = lax.broadcasted_iota(jnp.int32, (tm, LANES), 1) < DN_DIM
    kq_cols, vk_cols = [], []
    for j in range(DN_HEADS // 2):
        sl = slice(j * LANES, (j + 1) * LANES)
        qb = qk[:, sl]
        kb = qk[:, DN_W + j * LANES:DN_W + (j + 1) * LANES]
        vb = y[:, 2 * DN_W + j * LANES:2 * DN_W + (j + 1) * LANES]
        qr, kr, vr = (pltpu.roll(t, DN_DIM, 1) for t in (qb, kb, vb))
        kq_cols += [jnp.where(lo_half, kb, qr), jnp.where(lo_half, kr, qb)]
        vk_cols += [jnp.where(lo_half, vb, kr), jnp.where(lo_half, vr, kb)]
    kq_ref[...] = jnp.concatenate(kq_cols, axis=1).astype(BF16)
    vk_ref[...] = jnp.concatenate(vk_cols, axis=1).astype(BF16)


def _dnprep(dnpre, seq, cw, bd, scale):
    m, width = dnpre.shape
    tm = min(512, seq)
    halo = 16
    hb = tm // halo
    last = m // halo - 1
    blk = min(128, tm)
    pad = DN_CONV // 2
    t = np.arange(blk)
    shift = np.zeros((DN_CONV - 1, blk, blk + 2 * halo), np.float32)
    for n, j in enumerate(x for x in range(DN_CONV) if x != pad):
        shift[n, t, halo + t + j - pad] = 1.0
    shift = jnp.asarray(shift.reshape((DN_CONV - 1) * blk, blk + 2 * halo), BF16)
    return pl.pallas_call(
        functools.partial(_dnprep_kernel, seq=seq), name="dnprep",
        grid=(m // tm,),
        in_specs=[
            pl.BlockSpec((tm, width), lambda i: (i, 0)),
            pl.BlockSpec((halo, width), lambda i: (jnp.maximum(i * hb - 1, 0), 0)),
            pl.BlockSpec((halo, width), lambda i: (jnp.minimum((i + 1) * hb, last), 0)),
            _full(cw.shape), _full(shift.shape), _full(bd.shape), _full(scale.shape),
        ],
        out_specs=(pl.BlockSpec((tm, 2 * DN_W), lambda i: (i, 0)),
                   pl.BlockSpec((tm, 2 * DN_W), lambda i: (i, 0))),
        out_shape=(jax.ShapeDtypeStruct((m, 2 * DN_W), BF16), jax.ShapeDtypeStruct((m, 2 * DN_W), BF16)),
        scratch_shapes=[pltpu.VMEM((tm + 2 * halo, width), BF16)],
        compiler_params=_cparams(("parallel",)),
    )(dnpre, dnpre, dnpre, cw, shift, bd, scale)


def _bd2(a, b):
    return jnp.concatenate([jnp.concatenate([a, jnp.zeros_like(b)], axis=1),
                            jnp.concatenate([jnp.zeros_like(a), b], axis=1)], axis=0)


def _softplus(x):
    return jnp.maximum(x, 0.0) + jnp.log1p(jnp.exp(-jnp.abs(x)))


def _delta_kernel(kqf_ref, kqb_ref, vkf_ref, vkb_ref, dabf_ref, dabb_ref, ea_ref, bias_ref, of_ref, ob_ref, s_ref):
    C = DN_CHUNK
    H = DN_DIM
    n = pl.program_id(1)

    @pl.when(n == 0)
    def _():
        s_ref[...] = jnp.zeros_like(s_ref)

    row = lax.broadcasted_iota(jnp.int32, (C, C), 0)
    col = lax.broadcasted_iota(jnp.int32, (C, C), 1)
    incl = (row >= col, row <= col)
    strict = (row > col, row < col)
    rowp = lax.broadcasted_iota(jnp.int32, (C, 2 * C), 0)
    colp = lax.broadcasted_iota(jnp.int32, (C, 2 * C), 1) % C
    lane = lax.broadcasted_iota(jnp.int32, (C, LANES), 1)
    lo_half = lane < H
    lane_s = lax.broadcasted_iota(jnp.int32, (H, LANES), 1)
    eye_p =jnp.where(rowp == colp, 1.0, 0.0)
    blk2 = rowp // 2 == colp // 2
    levels = []
    b = 2
    while b < C:
        levels.append((rowp // (2 * b) == colp // (2 * b)) & (rowp // b != colp // b))
        b *= 2

    kq_refs = (kqf_ref, kqb_ref)
    vk_refs = (vkf_ref, vkb_ref)
    dab_refs = (dabf_ref, dabb_ref)
    o_refs = (of_ref, ob_ref)

    gc_all, gl_all, beta_all = [], [], []
    for d in range(2):
        ab = dab_refs[d][0]
        g = -ea_ref[d] * _softplus(ab + bias_ref[d])
        beta_all.append(jax.nn.sigmoid(ab))
        tri = jnp.where(incl[d], 1.0, 0.0).astype(BF16)
        g_hi = g.astype(BF16)
        g_lo = (g - g_hi.astype(F32)).astype(BF16)
        gc_all.append(jnp.dot(tri, g_hi, preferred_element_type=F32)
                      + jnp.dot(tri, g_lo, preferred_element_type=F32))
        gl_all.append(jnp.sum(g, axis=0, keepdims=True))

    units = [(d, j) for d in range(2) for j in range(DN_HEADS // 2)]

    def halves(x):
        return x[:, :x.shape[1] // 2], x[:, x.shape[1] // 2:]

    def rhs1(t):
        z = jnp.zeros_like(t)
        return jnp.concatenate([jnp.concatenate([t, z], axis=1), jnp.concatenate([z, t], axis=1)], axis=0)

    kq_p, kqf, vkf, gccol, gcrow, bcol, gl, kT, decay = ({} for _ in range(9))
    for u in units:
        d, j = u
        sl = slice(2 * j * LANES, (2 * j + 2) * LANES)
        kq_p[u] = kq_refs[d][:, sl]
        kqf[u] = kq_p[u].astype(F32)
        vkf[u] = vk_refs[d][:, sl].astype(F32)
        gccol[u], gcrow[u], bcol[u], gl[u], kT[u], decay[u] = [], [], [], [], [], []
        for hh in range(2):
            h = 2 * j + hh
            gccol[u].append(jnp.broadcast_to(gc_all[d][:, h:h + 1], (C, LANES)))
            bcol[u].append(jnp.broadcast_to(beta_all[d][:, DN_HEADS + h:DN_HEADS + h + 1], (C, LANES)))
            gl[u].append(gl_all[d][:, h:h + 1])
            gcrow[u].append(gccol[u][hh].T)
            kT[u].append(kqf[u][:, hh * LANES:(hh + 1) * LANES].T[0:H, :])
            decay[u].append(jnp.where(incl[d], jnp.exp(jnp.minimum(gccol[u][hh] - gcrow[u][hh], 0.0)), 0.0))

    p1 = {u: _mm(kq_p[u], _bd2(rhs1(kT[u][0]), rhs1(kT[u][1]))) for u in units}
    a_p, qkd, x_p = {}, {}, {}
    for u in units:
        d = u[0]
        a_h, qk_h = [], []
        for hh in range(2):
            kk = p1[u][:, (2 * hh) * C:(2 * hh + 1) * C]
            qk = p1[u][:, (2 * hh + 1) * C:(2 * hh + 2) * C]
            a_h.append(jnp.where(strict[d], kk * decay[u][hh], 0.0) * bcol[u][hh])
            qk_h.append(qk * decay[u][hh])
        a_p[u] = jnp.concatenate(a_h, axis=1)
        qkd[u] = jnp.concatenate(qk_h, axis=1)
        x_p[u] = eye_p - jnp.where(blk2, a_p[u], 0.0)

    for msk in levels:
        g_p = {u: _mm(jnp.where(msk, a_p[u], 0.0), _bd2(*halves(x_p[u]))) for u in units}
        x_p = {u: x_p[u] - _mm(x_p[u], _bd2(*halves(g_p[u]))) for u in units}

    uw = {}
    for u in units:
        rhs2 = [vkf[u][:, hh * LANES:(hh + 1) * LANES] * bcol[u][hh]
                * jnp.where(lo_half, 1.0, jnp.exp(gccol[u][hh])) for hh in range(2)]
        uw[u] = _mm(x_p[u], _bd2(rhs2[0], rhs2[1]))
    ol = {u: _mm(qkd[u], _bd2(*halves(uw[u]))) for u in units}
    nw = {}
    for u in units:
        kdT = jnp.concatenate([kT[u][hh] * jnp.exp(gl[u][hh] - gcrow[u][hh][0:H, :]) for hh in range(2)], axis=0)
        nw[u] = _mm(kdT, uw[u])
    for u in units:
        d, j = u
        sl = slice(2 * j * LANES, (2 * j + 2) * LANES)
        zmult = jnp.concatenate(
            [jnp.where(lo_half, jnp.exp(gl[u][hh] - gccol[u][hh]), jnp.exp(gccol[u][hh])) for hh in range(2)], axis=1)
        z_p = kqf[u] * zmult - ol[u]
        nw_h = (nw[u][0:H, 0:LANES], nw[u][H:2 * H, LANES:2 * LANES])
        s_old = [s_ref[d, 2 * j + hh] for hh in range(2)]
        sblk = [jnp.concatenate([jnp.zeros((H, LANES), F32), s_old[hh]], axis=0) for hh in range(2)]
        r = _mm(jnp.concatenate([_bd2(*nw_h), z_p], axis=0), _bd2(sblk[0], sblk[1]))
        o_pair = r[2 * H:, :] + ol[u]
        o_refs[d][:, j * LANES:(j + 1) * LANES] = jnp.where(
            lo_half, o_pair[:, :LANES], pltpu.roll(o_pair[:, LANES:], H, 1))
        wks = (r[0:H, 0:LANES], r[H:2 * H, LANES:2 * LANES])
        for hh in range(2):
            s_new = jnp.exp(gl[u][hh]) * s_old[hh] + nw_h[hh] - wks[hh]
            s_ref[d, 2 * j + hh] = jnp.where(lane_s < H, s_new, 0.0)


def _delta_rule(kq, vk, dab, ea, bias, batch, seq):
    m = kq.shape[0]
    C = DN_CHUNK
    nc = seq // C
    width = kq.shape[1]
    fwd = lambda b, n: (b * nc + n, 0)
    bwd = lambda b, n: (b * nc + nc - 1 - n, 0)
    return pl.pallas_call(
        _delta_kernel, name="delta",
        grid=(batch, nc),
        in_specs=[
            pl.BlockSpec((C, width), fwd), pl.BlockSpec((C, width), bwd),
            pl.BlockSpec((C, width), fwd), pl.BlockSpec((C, width), bwd),
            pl.BlockSpec((1, C, 2 * DN_HEADS), lambda b, n: (0, b * nc + n, 0)),
            pl.BlockSpec((1, C, 2 * DN_HEADS), lambda b, n: (1, b * nc + nc - 1 - n, 0)),
            _full(ea.shape), _full(bias.shape),
        ],
        out_specs=(pl.BlockSpec((C, DN_W), fwd), pl.BlockSpec((C, DN_W), bwd)),
        out_shape=(jax.ShapeDtypeStruct((m, DN_W), F32), jax.ShapeDtypeStruct((m, DN_W), F32)),
        scratch_shapes=[pltpu.VMEM((2, DN_HEADS, DN_DIM, LANES), F32)],
        compiler_params=_cparams(("parallel", "arbitrary")),
    )(kq, kq, vk, vk, dab, dab, ea, bias)


def _merge_kernel(ot_ref, of_ref, ob_ref, dz_ref, sg_ref, x_ref, wa_ref, wb_ref, wo_ref, bdo_ref, onw_ref, n2_ref,
                  wr_hi_ref, wr_lo_ref, x1_ref, xp_ref, lg_ref):
    tm = x_ref.shape[0]
    ya = lax.dot_general(ot_ref[...].reshape(ATT_Q_W, tm), wa_ref[...], (((0,), (0,)), ((), ())),
                         preferred_element_type=F32)
    o = of_ref[...] + ob_ref[...]
    ss = _mm(o * o, bdo_ref[...])
    dz = dz_ref[...].astype(F32)
    dn = o * lax.rsqrt(ss * (1.0 / DN_DIM) + EPS) * onw_ref[...] * (dz * jax.nn.sigmoid(dz))
    yb = _mm(dn, wb_ref[...])
    sg = sg_ref[...].astype(F32)
    mix = sg[:, :D_MODEL] * ya + sg[:, D_MODEL:] * yb
    x1 = x_ref[...] + _mm(mix, wo_ref[...])
    x1_ref[...] = x1
    ms = jnp.mean(x1 * x1, axis=-1, keepdims=True)
    xn = x1 * lax.rsqrt(ms + EPS) * n2_ref[...]
    xn_hi = xn.astype(BF16)
    xn_lo = (xn - xn_hi.astype(F32)).astype(BF16)
    xp_ref[0], xp_ref[1] = _pack_rows(xn)
    whi = wr_hi_ref[...]
    lg_ref[...] = (jnp.dot(xn_hi, whi, preferred_element_type=F32)
                   + jnp.dot(xn_lo, whi, preferred_element_type=F32)
                   + jnp.dot(xn_hi, wr_lo_ref[...], preferred_element_type=F32))


def _merge(ot, o_f, o_b, dz, sg, x2, wa, wb, wo, bdo, onw, n2, wr_hi, wr_lo):
    m = x2.shape[0]
    tm = 256
    row = lambda i: (i, 0)
    return pl.pallas_call(
        _merge_kernel, name="merge", grid=(m // tm,),
        in_specs=[
            pl.BlockSpec((ATT_HEADS, HEAD_DIM, tm), lambda i: (0, 0, i)),
            pl.BlockSpec((tm, o_f.shape[1]), row),
            pl.BlockSpec((tm, o_b.shape[1]), row),
            pl.BlockSpec((tm, dz.shape[1]), row),
            pl.BlockSpec((tm, sg.shape[1]), row),
            pl.BlockSpec((tm, D_MODEL), row),
            _full(wa.shape), _full(wb.shape), _full(wo.shape), _full(bdo.shape), _full(onw.shape),
            _full(n2.shape), _full(wr_hi.shape), _full(wr_lo.shape),
        ],
        out_specs=(pl.BlockSpec((tm, D_MODEL), row), pl.BlockSpec((2, tm, PACK_W), lambda i: (0, i, 0)),
                   pl.BlockSpec((tm, LANES), row)),
        out_shape=(jax.ShapeDtypeStruct((m, D_MODEL), F32), jax.ShapeDtypeStruct((2, m, PACK_W), jnp.uint32),
                   jax.ShapeDtypeStruct((m, LANES), F32)),
        compiler_params=_cparams(("parallel",)),
    )(ot, o_f, o_b, dz, sg, x2, wa, wb, wo, bdo, onw, n2, wr_hi, wr_lo)


def _route_kernel(lg_ref, earlier_ref, info_ref, info_t_ref, cnt_ref):
    @pl.when(pl.program_id(0) == 0)
    def _():
        cnt_ref[...] = jnp.zeros_like(cnt_ref)

    lg = lg_ref[...]
    lane_i = lax.broadcasted_iota(jnp.int32, lg.shape, 1)
    lane = lane_i.astype(F32)
    group_of = ((lane_i - ROUTER_OFF) // EXPERTS_PER_GROUP).astype(F32)
    neg = -jnp.inf

    def first_argmax(v):
        mx = jnp.max(v, axis=-1, keepdims=True)
        idx = jnp.min(jnp.where(v == mx, lane, float(LANES)), axis=-1, keepdims=True)
        return mx, idx

    gl = jnp.where(lane_i < N_GROUPS, lg, neg)
    gmax, gidx = first_argmax(gl)
    gval = 1.0 / jnp.sum(jnp.exp(gl - gmax), axis=-1, keepdims=True)
    is_exp = (lane_i >= ROUTER_OFF) & (lane_i < ROUTER_OFF + N_EXPERTS)
    sel = is_exp & (group_of == gidx)
    el = jnp.where(sel, lg, neg)
    m1, i1 = first_argmax(el)
    el2 = jnp.where(lane == i1, neg, el)
    m2, i2 = first_argmax(el2)
    r = jnp.exp(m2 - m1)
    w1 = gval / (1.0 + r)
    w2 = gval * r / (1.0 + r)
    chosen =jnp.where((lane == i1) | (lane == i2), 1.0, 0.0)
    before = _mm(earlier_ref[...], chosen) + cnt_ref[0:1, :]
    rank1 = jnp.sum(jnp.where(lane == i1, before, 0.0), axis=-1, keepdims=True)
    rank2 = jnp.sum(jnp.where(lane == i2, before, 0.0), axis=-1, keepdims=True)
    cnt_ref[...] = cnt_ref[...] + jnp.sum(chosen, axis=0, keepdims=True)
    info = (jnp.where(lane_i == 0, i1 - ROUTER_OFF, 0.0) + jnp.where(lane_i == 1, i2 - ROUTER_OFF, 0.0)
            + jnp.where(lane_i == 2, w1, 0.0) + jnp.where(lane_i == 3, w2, 0.0)
            + jnp.where(lane_i == 4, rank1, 0.0) + jnp.where(lane_i == 5, rank2, 0.0))
    info_ref[...] = info
    info_t_ref[...] = info.T[0:8, :]


def _route(logits):
    m = logits.shape[0]
    tm = 1024 if m % 1024 == 0 else 256
    spec = pl.BlockSpec((tm, LANES), lambda i: (i, 0))
    t = np.arange(tm)
    earlier = jnp.asarray(t[:, None] > t[None, :], BF16)
    return pl.pallas_call(
        _route_kernel, name="route", grid=(m // tm,), in_specs=[spec, _full(earlier.shape)],
        out_specs=(spec, pl.BlockSpec((8, tm), lambda i: (0, i)), pl.BlockSpec((8, LANES), lambda i: (0, 0))),
        out_shape=(jax.ShapeDtypeStruct((m, LANES), F32), jax.ShapeDtypeStruct((8, m), F32),
                   jax.ShapeDtypeStruct((8, LANES), F32)),
        compiler_params=_cparams(("arbitrary",)),
    )(logits, earlier)


def _pack_rows(x):
    bits = pltpu.bitcast(x.astype(BF16).astype(F32), jnp.uint32)
    half = x.shape[1] // 2
    word = (bits[:, :half] >> 16) | (bits[:, half:] & jnp.uint32(0xFFFF0000))
    return word[:, :PACK_W], word[:, PACK_W:]


def _unpack_rows(w0, w1):
    lo = [pltpu.bitcast(w << 16, F32) for w in (w0, w1)]
    hi = [pltpu.bitcast(w & jnp.uint32(0xFFFF0000), F32) for w in (w0, w1)]
    return jnp.concatenate(lo + hi, axis=1)


def _sc_mesh():
    return plsc.VectorSubcoreMesh(core_axis_name="c", subcore_axis_name="s")


def _sc_scatter_rows(src, idx, n_out, reps):
    s, width = src.shape
    nblk = s // SC_WINDOW

    @pl.kernel(out_type=jax.ShapeDtypeStruct((n_out, width), src.dtype), mesh=_sc_mesh(), scratch_types=[])
    def scatter_kernel(x_hbm, i_hbm, o_hbm):
        def body(x_vmem, i_vmem):
            pltpu.sync_copy(x_vmem, o_hbm.at[i_vmem.at[0]])

        pltpu.emit_pipeline(
            body, grid=(reps * nblk,),
            in_specs=[pl.BlockSpec((SC_WINDOW, width), index_map=lambda i: (i % nblk, 0)),
                      pl.BlockSpec((1, SC_WINDOW), index_map=lambda i: (0, i))],
            out_specs=[], core_axis_name=("c", "s"), dimension_semantics=(pltpu.PARALLEL,),
        )(x_hbm, i_hbm)

    return scatter_kernel(src, idx.reshape(1, reps * s))


def _sc_gather_rows(table, idx):
    k = idx.shape[0]
    width = table.shape[1]

    @pl.kernel(out_type=jax.ShapeDtypeStruct((k, width), table.dtype), mesh=_sc_mesh())
    def gather_kernel(x_hbm, i_hbm, o_hbm):
        def body(i_vmem, o_vmem):
            pltpu.sync_copy(x_hbm.at[i_vmem.at[0]], o_vmem)

        pltpu.emit_pipeline(
            body, grid=(k // SC_WINDOW,),
            in_specs=[pl.BlockSpec((1, SC_WINDOW), index_map=lambda i: (0, i))],
            out_specs=[pl.BlockSpec((SC_WINDOW, width), index_map=lambda i: (i, 0))],
            core_axis_name=("c", "s"), dimension_semantics=(pltpu.PARALLEL,),
        )(i_hbm, o_hbm)

    return gather_kernel(table, idx.reshape(1, k))


def _routing_tables(info_t, counts):
    m = info_t.shape[1]
    e = info_t[0:2].astype(jnp.int32)
    rank = info_t[4:6].astype(jnp.int32)
    cnt = counts[0, ROUTER_OFF:ROUTER_OFF + N_EXPERTS].astype(jnp.int32)
    tiles_e = (cnt + MOE_TILE - 1) // MOE_TILE
    tile_end = jnp.cumsum(tiles_e)
    slot_start = (tile_end - tiles_e) * MOE_TILE
    expert_ids = jnp.arange(N_EXPERTS, dtype=jnp.int32)
    pos = jnp.sum(jnp.where(e[None] == expert_ids[:, None, None], slot_start[:, None, None], 0), axis=0) + rank
    n_tiles = 2 * m // MOE_TILE + N_EXPERTS
    tile_ids = jnp.arange(n_tiles, dtype=jnp.int32)
    tile_expert = jnp.sum(tile_ids[:, None] >= tile_end[None, :], axis=1)
    tile_expert = jnp.minimum(tile_expert, N_EXPERTS - 1).astype(jnp.int32)
    n_used = tile_end[-1:].astype(jnp.int32)
    has = tiles_e > 0
    run_index = jnp.cumsum(has.astype(jnp.int32)) - 1
    later = (expert_ids[None, :] > expert_ids[:, None]) & has[None, :]
    next_e = jnp.min(jnp.where(later, expert_ids[None, :], N_EXPERTS), axis=1)
    next_e = jnp.where(next_e < N_EXPERTS, next_e, -1)
    first = (tile_ids == jnp.take(tile_end - tiles_e, tile_expert)) & (tile_ids < n_used[0])
    sched = (tile_expert, n_used, jnp.take(next_e, tile_expert).astype(jnp.int32), first.astype(jnp.int32),
             (jnp.take(run_index, tile_expert) % 2).astype(jnp.int32))
    return pos, sched


def _experts_kernel(te_ref, nu_ref, nxt_ref, first_ref, slot_ref, xs_ref, wg_hbm, wu_hbm, wd_hbm, ys_ref,
                    wg_buf, wu_buf, wd_buf, sem):
    i = pl.program_id(0)

    def weight_copies(e, s):
        return (pltpu.make_async_copy(wg_hbm.at[e], wg_buf.at[s], sem.at[0, s]),
                pltpu.make_async_copy(wu_hbm.at[e], wu_buf.at[s], sem.at[1, s]),
                pltpu.make_async_copy(wd_hbm.at[e], wd_buf.at[s], sem.at[2, s]))

    @pl.when(i == 0)
    def _():
        for c in weight_copies(te_ref[0], slot_ref[0]):
            c.start()

    s = slot_ref[i]

    @pl.when(first_ref[i] == 1)
    def _():
        for c in weight_copies(te_ref[i], s):
            c.wait()

        @pl.when(nxt_ref[i] >= 0)
        def _():
            for c in weight_copies(nxt_ref[i], 1 - s):
                c.start()

    @pl.when(i < nu_ref[0])
    def _():
        x = _unpack_rows(xs_ref[0], xs_ref[1]).astype(BF16)
        hg = _mm(x, wg_buf[s])
        hu = _mm(x, wu_buf[s])
        y = _mm(hg * jax.nn.sigmoid(hg) * hu, wd_buf[s])
        ys_ref[0], ys_ref[1] = _pack_rows(y)


def _experts(xs, sched, wg, wu, wd):
    n_slots = xs.shape[1]
    slots = pl.BlockSpec((2, MOE_TILE, PACK_W), lambda i, *_: (0, i, 0))
    hbm = pl.BlockSpec(memory_space=pl.ANY)
    return pl.pallas_call(
        _experts_kernel, name="experts",
        grid_spec=pltpu.PrefetchScalarGridSpec(
            num_scalar_prefetch=len(sched), grid=(n_slots // MOE_TILE,),
            in_specs=[slots, hbm, hbm, hbm],
            out_specs=slots,
            scratch_shapes=[pltpu.VMEM((2,) + wg.shape[1:], wg.dtype), pltpu.VMEM((2,) + wu.shape[1:], wu.dtype),
                            pltpu.VMEM((2,) + wd.shape[1:], wd.dtype), pltpu.SemaphoreType.DMA((3, 2))],
        ),
        out_shape=jax.ShapeDtypeStruct(xs.shape, jnp.uint32),
        compiler_params=_cparams(("arbitrary",)),
    )(*sched, xs, wg, wu, wd)


def _combine_kernel(x1_ref, info_ref, yg_ref, fw_ref, y_ref):
    info = info_ref[...]
    x = (x1_ref[...] + info[:, 2:3] * _unpack_rows(yg_ref[0, 0], yg_ref[0, 1])
         + info[:, 3:4] * _unpack_rows(yg_ref[1, 0], yg_ref[1, 1]))
    ms = jnp.mean(x * x, axis=-1, keepdims=True)
    y_ref[...] = x * lax.rsqrt(ms + EPS) * fw_ref[...]


def _combine(x1, info, yg, fw):
    m = x1.shape[0]
    tm = 512 if m % 512 == 0 else 256
    row = lambda i: (i, 0)
    return pl.pallas_call(
        _combine_kernel, name="combine", grid=(m // tm,),
        in_specs=[pl.BlockSpec((tm, D_MODEL), row), pl.BlockSpec((tm, LANES), row),
                  pl.BlockSpec((2, 2, tm, PACK_W), lambda i: (0, 0, i, 0)), _full(fw.shape)],
        out_specs=pl.BlockSpec((tm, D_MODEL), row),
        out_shape=jax.ShapeDtypeStruct((m, D_MODEL), F32),
        compiler_params=_cparams(("parallel",)),
    )(x1, info, yg, fw)


def _moe_final(xp, x1, info, info_t, counts, wg, wu, wd, fw):
    m = x1.shape[0]
    pos, sched = _routing_tables(info_t, counts)
    n_slots = 2 * m + N_EXPERTS * MOE_TILE
    idx = jnp.concatenate([h * n_slots + pos[k] for k in range(2) for h in range(2)])
    xs = _sc_scatter_rows(xp.reshape(2 * m, PACK_W), idx, 2 * n_slots, 2)
    ys = _experts(xs.reshape(2, n_slots, PACK_W), sched, wg, wu, wd)
    yg = _sc_gather_rows(ys.reshape(2 * n_slots, PACK_W), idx)
    return _combine(x1, info, yg.reshape(2, 2, m, PACK_W), fw)


def _block_ones(n, blk):
    idx = np.arange(n)
    return jnp.asarray((idx[:, None] // blk == idx[None, :] // blk), dtype=BF16)


def _prepare(norm1_w, w_in, att_q_norm, att_k_norm, dn_conv_w, dn_a_log, dn_dt_bias, dn_out_norm,
             w_branch_att, w_branch_dn, w_out, norm2_w, moe_group_router, moe_expert_router,
             moe_w_gate, moe_w_up, moe_w_down, final_norm_w):
    w_in = w_in[0]
    o_q, o_k, o_v = 0, ATT_Q_W, ATT_Q_W + ATT_KV_W
    o_dq = o_v + ATT_KV_W
    o_dz = o_dq + 3 * DN_W
    o_da = o_dz + DN_W
    o_db = o_da + 2 * DN_HEADS
    o_ga = o_db + 2 * DN_HEADS
    o_gb = o_ga + D_MODEL

    def deinterleave(w, heads):
        lead = w.shape[:-1]
        w = w.reshape(lead + (heads, HEAD_DIM // 2, 2))
        return jnp.swapaxes(w, -1, -2).reshape(lead + (heads * HEAD_DIM,))

    watt = jnp.concatenate([deinterleave(w_in[:, o_q:o_k], ATT_HEADS), deinterleave(w_in[:, o_k:o_v], ATT_KV_HEADS),
                            w_in[:, o_v:o_dq]], axis=1).astype(BF16)
    qnw = jnp.tile(deinterleave(att_q_norm[0], 1), ATT_HEADS)[None, :]
    knw = jnp.tile(deinterleave(att_k_norm[0], 1), ATT_KV_HEADS)[None, :]

    wdn =w_in[:, o_dq:o_dz].astype(BF16)
    cw = jnp.concatenate([dn_conv_w[0], jnp.zeros((8 - DN_CONV, 3 * DN_W), F32)], axis=0)
    scale = jnp.asarray(np.concatenate([np.full(DN_W, DN_DIM ** -0.5), np.ones(DN_W)]), F32)[None, :]

    wdz = w_in[:, o_dz:o_dz + DN_W].astype(BF16)
    onw = jnp.tile(dn_out_norm[0], DN_HEADS)[None, :]
    wb = w_branch_dn[0].astype(BF16)

    wab = jnp.zeros((D_MODEL, 2 * LANES), F32)
    for dirn in range(2):
        wab = wab.at[:, dirn * LANES:dirn * LANES + DN_HEADS].set(
            w_in[:, o_da + dirn * DN_HEADS:o_da + (dirn + 1) * DN_HEADS])
        wab = wab.at[:, dirn * LANES + DN_HEADS:dirn * LANES + 2 * DN_HEADS].set(
            w_in[:, o_db + dirn * DN_HEADS:o_db + (dirn + 1) * DN_HEADS])
    wab = wab.astype(BF16)
    wg = w_in[:, o_ga:o_gb + D_MODEL].astype(BF16)

    zeros8 = jnp.zeros((2, DN_HEADS), F32)
    ea = jnp.concatenate([jnp.exp(dn_a_log[0]), zeros8], axis=1)[:, None, :]
    bias = jnp.concatenate([dn_dt_bias[0], zeros8], axis=1)[:, None, :]


    wr = jnp.concatenate([moe_group_router[0], moe_expert_router[0],
                          jnp.zeros((D_MODEL, LANES - N_GROUPS - N_EXPERTS), F32)], axis=1)
    wr_hi = wr.astype(BF16)
    wr_lo = (wr - wr_hi.astype(F32)).astype(BF16)

    return dict(
        n1=norm1_w[0][None, :], watt=watt, wdn=wdn, wdz=wdz, wab=wab, wg=wg,
        bd_att=_block_ones(ATT_Q_W, HEAD_DIM), qnw=qnw, knw=knw,
        cw=cw, bd_dn=_block_ones(2 * LANES, DN_DIM), scale=scale, ea=ea, bias=bias,
        wa=w_branch_att[0].astype(BF16), wb=wb, wo=w_out[0].astype(BF16), bdo=_block_ones(DN_W, DN_DIM), onw=onw,
        n2=norm2_w[0][None, :], wr_hi=wr_hi, wr_lo=wr_lo,
        wge=moe_w_gate[0], wue=moe_w_up[0], wde=moe_w_down[0],
        fw=final_norm_w[None, :],
    )


def _rope_tables(seq):
    t = np.arange(seq)
    axis_dim = HEAD_DIM // 2
    inv = ROPE_THETA ** (-np.arange(0, axis_dim, 2, dtype=np.float32) / axis_dim)
    r = (t // GRID_W).astype(np.float32)
    c = (t % GRID_W).astype(np.float32)
    ang = np.concatenate([r[:, None] * inv, c[:, None] * inv], axis=-1).astype(np.float32)
    ang = jnp.asarray(ang)
    cos, sin = jnp.cos(ang), jnp.sin(ang)
    cos_t = jnp.tile(jnp.concatenate([cos, cos], axis=1), (1, LANES // HEAD_DIM))
    sin_t = jnp.tile(jnp.concatenate([-sin, sin], axis=1), (1, LANES // HEAD_DIM))
    return cos_t, sin_t


def _trunk(x, p):
    batch, seq, _ = x.shape
    x2 = x.reshape(batch * seq, D_MODEL)
    cos_t, sin_t = _rope_tables(seq)
    qt, kn, vt, dnpre, dz, dab, sg, kmx_tiles = _inproj(
        x2, seq, p["n1"], p["watt"], p["wdn"], p["wdz"], p["wab"], p["wg"], p["bd_att"], p["qnw"], p["knw"],
        cos_t, sin_t)
    ot = _attention(qt, kn, vt, kmx_tiles.reshape(batch, -1, 8, ATT_KV_W), batch, seq)
    kq, vk = _dnprep(dnpre, seq, p["cw"], p["bd_dn"], p["scale"])
    o_f, o_b = _delta_rule(kq, vk, dab, p["ea"], p["bias"], batch, seq)
    x1, xp, logits = _merge(ot, o_f, o_b, dz, sg, x2, p["wa"], p["wb"], p["wo"], p["bdo"], p["onw"], p["n2"],
                            p["wr_hi"], p["wr_lo"])
    info, info_t, counts = _route(logits)
    y = _moe_final(xp, x1, info, info_t, counts, p["wge"], p["wue"], p["wde"], p["fw"])
    return y.reshape(batch, seq, D_MODEL)


def kernel(x_prompt, x_sample, norm1_w, w_in, att_q_norm, att_k_norm, dn_conv_w, dn_a_log, dn_dt_bias, dn_out_norm, w_branch_att, w_branch_dn, w_out, norm2_w, moe_group_router, moe_expert_router, moe_w_gate, moe_w_up, moe_w_down, final_norm_w):
    p = _prepare(norm1_w, w_in, att_q_norm, att_k_norm, dn_conv_w, dn_a_log, dn_dt_bias, dn_out_norm,
                 w_branch_att, w_branch_dn, w_out, norm2_w, moe_group_router, moe_expert_router,
                 moe_w_gate, moe_w_up, moe_w_down, final_norm_w)
    return (_trunk(x_prompt, p), _trunk(x_sample, p))
```

```python
import functools
import math

import numpy as np
import jax
import jax.numpy as jnp
from jax import lax
from jax.experimental import pallas as pl
from jax.experimental.pallas import tpu as pltpu
from jax.experimental.pallas import tpu_sc as plsc

F32 = jnp.float32
BF16 = jnp.bfloat16

D_MODEL = 1024
GRID_W = 64
EPS = 1e-6
ATT_HEADS = 8
ATT_KV_HEADS = 2
ATT_GROUP = ATT_HEADS // ATT_KV_HEADS
HEAD_DIM = 64
ROPE_THETA = 10000.0
DN_HEADS = 8
DN_DIM = 64
DN_CONV = 5
N_GROUPS = 4
EXPERTS_PER_GROUP = 8
N_EXPERTS = N_GROUPS * EXPERTS_PER_GROUP
EXPERT_FF = 256

ATT_Q_W = ATT_HEADS * HEAD_DIM
ATT_KV_W = ATT_KV_HEADS * HEAD_DIM
DN_W = DN_HEADS * DN_DIM
LANES = 128
DN_CHUNK = 128
ROUTER_OFF = N_GROUPS
PACK_W = D_MODEL // 4
SC_WINDOW = 128
MOE_TILE = 512
ATT_TQ = 1024
ATT_TK = 1024
VMEM_LIMIT = 52 * 1024 * 1024
LOG2E = math.log2(math.e)
SCORE_BOUND_MAX = 50.0
SCORE_BOUND_SLACK = 1.05


def _mm(a, b):
    return jnp.dot(a.astype(BF16), b.astype(BF16), preferred_element_type=F32)


def _cparams(sem):
    return pltpu.CompilerParams(dimension_semantics=sem, vmem_limit_bytes=VMEM_LIMIT)


def _full(shape):
    nd = len(shape)
    return pl.BlockSpec(shape, lambda *_: (0,) * nd, pipeline_mode=pl.Buffered(1))


def _rope(x, cos, sin_signed):
    n = x.shape[1]
    lane = lax.broadcasted_iota(jnp.int32, x.shape, 1)
    first = (lane % HEAD_DIM) < (HEAD_DIM // 2)
    partner = jnp.where(first, pltpu.roll(x, n - HEAD_DIM // 2, 1), pltpu.roll(x, HEAD_DIM // 2, 1))
    return x * cos + partner * sin_signed


def _inproj_kernel(x_ref, n1_ref, watt_ref, wdn_ref, wdz_ref, wab_ref, wg_ref, bd_ref, qnw_ref, knw_ref,
                   cos_ref, sin_ref, qt_ref, k_ref, vt_ref, dn_ref, dz_ref, dab_ref, sg_ref, kmx_ref):
    tm = x_ref.shape[0]
    x = x_ref[...]
    ms = jnp.mean(x * x, axis=-1, keepdims=True)
    xn = (x * lax.rsqrt(ms + EPS) * n1_ref[...]).astype(BF16)

    att = jnp.dot(xn, watt_ref[...], preferred_element_type=F32)
    aq = att[:, :ATT_Q_W]
    ak = att[:, ATT_Q_W:ATT_Q_W + ATT_KV_W]
    av = att[:, ATT_Q_W + ATT_KV_W:]
    bd = bd_ref[...]
    qss = _mm(aq * aq, bd)
    kss = _mm(ak * ak, bd[:ATT_KV_W, :ATT_KV_W])
    cos = cos_ref[...]
    sin = sin_ref[...]
    cos4 = jnp.concatenate([cos] * (ATT_Q_W // LANES), axis=1)
    sin4 = jnp.concatenate([sin] * (ATT_Q_W // LANES), axis=1)
    q = aq * lax.rsqrt(qss * (1.0 / HEAD_DIM) + EPS) * qnw_ref[...]
    k = ak * lax.rsqrt(kss * (1.0 / HEAD_DIM) + EPS) * knw_ref[...]
    q = _rope(q, cos4, sin4) * (HEAD_DIM ** -0.5 * LOG2E)
    kmx_ref[0] = jnp.broadcast_to(jnp.max(_mm(k * k, bd[:ATT_KV_W, :ATT_KV_W]), axis=0, keepdims=True),
                                  (8, ATT_KV_W))
    k = _rope(k, cos, sin)
    qt_ref[...] = q.T.reshape(ATT_HEADS, HEAD_DIM, tm).astype(BF16)
    k_ref[0] = k[:, :HEAD_DIM].astype(BF16)
    k_ref[1] = k[:, HEAD_DIM:].astype(BF16)
    vt_ref[...] = av.T.reshape(ATT_KV_HEADS, HEAD_DIM, tm).astype(BF16)

    dn_ref[...] = jnp.dot(xn, wdn_ref[...], preferred_element_type=F32).astype(BF16)
    dz_ref[...] = jnp.dot(xn, wdz_ref[...], preferred_element_type=F32).astype(BF16)
    ab = jnp.dot(xn, wab_ref[...], preferred_element_type=F32)
    dab_ref[0] = ab[:, 0:2 * DN_HEADS]
    dab_ref[1] = ab[:, LANES:LANES + 2 * DN_HEADS]
    sg_ref[...] = jax.nn.sigmoid(jnp.dot(xn, wg_ref[...], preferred_element_type=F32)).astype(BF16)


def _inproj(x2, seq, n1, watt, wdn, wdz, wab, wg, bd, qnw, knw, cos_t, sin_t):
    m = x2.shape[0]
    tm = min(512, seq)
    per_seq = seq // tm
    row = lambda i: (i, 0)
    out_shape = (
        jax.ShapeDtypeStruct((ATT_HEADS, HEAD_DIM, m), BF16),
        jax.ShapeDtypeStruct((ATT_KV_HEADS, m, HEAD_DIM), BF16),
        jax.ShapeDtypeStruct((ATT_KV_HEADS, HEAD_DIM, m), BF16),
        jax.ShapeDtypeStruct((m, 3 * DN_W), BF16),
        jax.ShapeDtypeStruct((m, DN_W), BF16),
        jax.ShapeDtypeStruct((2, m, 2 * DN_HEADS), F32),
        jax.ShapeDtypeStruct((m, 2 * D_MODEL), BF16),
        jax.ShapeDtypeStruct((m // tm, 8, ATT_KV_W), F32),
    )
    in_specs = [
        pl.BlockSpec((tm, D_MODEL), row),
        _full(n1.shape), _full(watt.shape), _full(wdn.shape), _full(wdz.shape), _full(wab.shape),
        _full(wg.shape), _full(bd.shape), _full(qnw.shape), _full(knw.shape),
        pl.BlockSpec((tm, LANES), lambda i: (i % per_seq, 0)),
        pl.BlockSpec((tm, LANES), lambda i: (i % per_seq, 0)),
    ]
    out_specs = (
        pl.BlockSpec((ATT_HEADS, HEAD_DIM, tm), lambda i: (0, 0, i)),
        pl.BlockSpec((ATT_KV_HEADS, tm, HEAD_DIM), lambda i: (0, i, 0)),
        pl.BlockSpec((ATT_KV_HEADS, HEAD_DIM, tm), lambda i: (0, 0, i)),
        pl.BlockSpec((tm, 3 * DN_W), row),
        pl.BlockSpec((tm, DN_W), row),
        pl.BlockSpec((2, tm, 2 * DN_HEADS), lambda i: (0, i, 0)),
        pl.BlockSpec((tm, 2 * D_MODEL), row),
        pl.BlockSpec((1, 8, ATT_KV_W), lambda i: (i, 0, 0)),
    )
    return pl.pallas_call(
        _inproj_kernel, name="inproj", grid=(m // tm,), in_specs=in_specs, out_specs=out_specs, out_shape=out_shape,
        compiler_params=_cparams(("parallel",)),
    )(x2, n1, watt, wdn, wdz, wab, wg, bd, qnw, knw, cos_t, sin_t)


def _attn_kernel(qt_ref, k_ref, vt_ref, kmx_ref, ot_ref, p_ref, *, tk, nk):
    tq = qt_ref.shape[2]
    q = jnp.concatenate([qt_ref[r] for r in range(ATT_GROUP)], axis=1)
    n = q.shape[1]
    qf = q.astype(F32)
    kmx = jnp.max(kmx_ref[0], axis=0)
    head_lanes = lax.broadcasted_iota(jnp.int32, kmx.shape, 1) // HEAD_DIM == pl.program_id(1)
    kmx = jnp.max(jnp.where(head_lanes, kmx, 0.0), axis=1, keepdims=True)[0:1, :]
    bound = jnp.sqrt(jnp.sum(qf * qf, axis=0, keepdims=True) * kmx) * SCORE_BOUND_SLACK
    small = jnp.max(bound) <= SCORE_BOUND_MAX
    ones_rows = 16

    def blocks(kb):
        off = pl.multiple_of(kb * tk, tk)
        return k_ref[0, pl.ds(off, tk), :], vt_ref[0, :, pl.ds(off, tk)]

    def write(out):
        for r in range(ATT_GROUP):
            ot_ref[r] = out[:, r * tq:(r + 1) * tq].astype(BF16)

    @pl.when(small)
    def _():
        def probs(kb):
            s = jnp.dot(blocks(kb)[0], q, preferred_element_type=F32)
            return jnp.exp2(s - bound).astype(BF16)

        def weighted(kb, p):
            vaug = jnp.concatenate([blocks(kb)[1], jnp.ones((ones_rows, tk), BF16)], axis=0)
            return jnp.dot(vaug, p, preferred_element_type=F32)

        acc0 = jnp.zeros((HEAD_DIM + ones_rows, n), F32)
        p_ref[0] = probs(0)
        if nk == 1:
            acc = acc0 + weighted(0, p_ref[0])
        else:
            def pair(kb, acc, last):
                p_ref[1] = probs(kb + 1)
                acc = acc + weighted(kb, p_ref[0])
                if not last:
                    p_ref[0] = probs(kb + 2)
                return acc + weighted(kb + 1, p_ref[1])

            acc = lax.fori_loop(0, nk // 2 - 1, lambda j, a: pair(2 * j, a, False), acc0)
            acc = pair(nk - 2, acc, True)
        write(acc[0:HEAD_DIM] / acc[HEAD_DIM:HEAD_DIM + 1])

    @pl.when(jnp.logical_not(small))
    def _():
        def body(kb, carry):
            m_run, l_run, acc = carry
            kblk, vblk = blocks(kb)
            s = jnp.dot(kblk, q, preferred_element_type=F32)
            m_new = jnp.maximum(m_run, jnp.max(s, axis=0, keepdims=True))
            p = jnp.exp2(s - m_new)
            alpha = jnp.exp2(m_run - m_new)
            l_new = alpha * l_run + jnp.sum(p, axis=0, keepdims=True)
            acc_new = alpha * acc + jnp.dot(vblk, p.astype(BF16), preferred_element_type=F32)
            return m_new, l_new, acc_new

        init = (jnp.full((1, n), -jnp.inf, F32), jnp.zeros((1, n), F32), jnp.zeros((HEAD_DIM, n), F32))
        _, l_fin, acc = lax.fori_loop(0, nk, body, init)
        write(acc / l_fin)


def _attention(qt, kn, vt, kmx, batch, seq):
    m = qt.shape[2]
    tq = min(ATT_TQ, seq)
    tk = min(ATT_TK, seq)
    nq = seq // tq
    qspec = pl.BlockSpec((ATT_GROUP, HEAD_DIM, tq), lambda b, g, i: (g, 0, b * nq + i))
    return pl.pallas_call(
        functools.partial(_attn_kernel, tk=tk, nk=seq // tk), name="attention",
        grid=(batch, ATT_KV_HEADS, nq),
        in_specs=[
            qspec,
            pl.BlockSpec((1, seq, HEAD_DIM), lambda b, g, i: (g, b, 0)),
            pl.BlockSpec((1, HEAD_DIM, seq), lambda b, g, i: (g, 0, b)),
            pl.BlockSpec((1,) + kmx.shape[1:], lambda b, g, i: (b, 0, 0, 0)),
        ],
        out_specs=qspec,
        out_shape=jax.ShapeDtypeStruct((ATT_HEADS, HEAD_DIM, m), BF16),
        scratch_shapes=[pltpu.VMEM((2, tk, ATT_GROUP * tq), BF16)],
        compiler_params=_cparams(("parallel", "parallel", "parallel")),
    )(qt, kn, vt, kmx)


def _dnprep_kernel(cur_ref, prev_ref, next_ref, cw_ref, shift_ref, bd_ref, scale_ref, kq_ref, vk_ref,
                   buf_ref, *, seq):
    tm = cur_ref.shape[0]
    halo = prev_ref.shape[0]
    blk = shift_ref.shape[1] - 2 * halo
    i = pl.program_id(0)
    pos = (i * tm) % seq
    has_prev = pos != 0
    has_next = pos + tm != seq
    buf_ref[0:halo, :] = jnp.where(has_prev, prev_ref[...], jnp.zeros_like(prev_ref))
    buf_ref[halo:halo + tm, :] = cur_ref[...]
    buf_ref[halo + tm:, :] = jnp.where(has_next, next_ref[...], jnp.zeros_like(next_ref))
    cw = cw_ref[...]
    pad = DN_CONV // 2
    shift = shift_ref[...]
    rows = []
    for r in range(tm // blk):
        win = buf_ref[r * blk:(r + 1) * blk + 2 * halo, :]
        shifted = jnp.dot(shift, win, preferred_element_type=F32)
        acc = win[halo:halo + blk, :].astype(F32) * cw[pad:pad + 1, :]
        for n, j in enumerate(t for t in range(DN_CONV) if t != pad):
            acc = acc + shifted[n * blk:(n + 1) * blk, :] * cw[j:j + 1, :]
        rows.append(acc)
    y = jnp.concatenate(rows, axis=0) if len(rows) > 1 else rows[0]
    y = y * jax.nn.sigmoid(y)
    bd = bd_ref[...]
    w = bd.shape[0]
    qk = y[:, :2 * DN_W]
    qksq = qk * qk
    ss = jnp.concatenate([_mm(qksq[:, c * w:(c + 1) * w], bd) for c in range(2 * DN_W // w)], axis=1)
    qk = qk * lax.rsqrt(ss + EPS) * scale_ref[...]
    lo_half = lax.broadcasted_iota(jnp.int32, (tm, LANES), 1) < DN_DIM
    kq_cols, vk_cols = [], []
    for j in range(DN_HEADS // 2):
        sl = slice(j * LANES, (j + 1) * LANES)
        qb = qk[:, sl]
        kb = qk[:, DN_W + j * LANES:DN_W + (j + 1) * LANES]
        vb = y[:, 2 * DN_W + j * LANES:2 * DN_W + (j + 1) * LANES]
        qr, kr, vr = (pltpu.roll(t, DN_DIM, 1) for t in (qb, kb, vb))
        kq_cols += [jnp.where(lo_half, kb, qr), jnp.where(lo_half, kr, qb)]
        vk_cols += [jnp.where(lo_half, vb, kr), jnp.where(lo_half, vr, kb)]
    kq_ref[...] = jnp.concatenate(kq_cols, axis=1).astype(BF16)
    vk_ref[...] = jnp.concatenate(vk_cols, axis=1).astype(BF16)


def _dnprep(dnpre, seq, cw, bd, scale):
    m, width = dnpre.shape
    tm = min(512, seq)
    halo = 16
    hb = tm // halo
    last = m // halo - 1
    blk = min(128, tm)
    pad = DN_CONV // 2
    t = np.arange(blk)
    shift = np.zeros((DN_CONV - 1, blk, blk + 2 * halo), np.float32)
    for n, j in enumerate(x for x in range(DN_CONV) if x != pad):
        shift[n, t, halo + t + j - pad] = 1.0
    shift = jnp.asarray(shift.reshape((DN_CONV - 1) * blk, blk + 2 * halo), BF16)
    return pl.pallas_call(
        functools.partial(_dnprep_kernel, seq=seq), name="dnprep",
        grid=(m // tm,),
        in_specs=[
            pl.BlockSpec((tm, width), lambda i: (i, 0)),
            pl.BlockSpec((halo, width), lambda i: (jnp.maximum(i * hb - 1, 0), 0)),
            pl.BlockSpec((halo, width), lambda i: (jnp.minimum((i + 1) * hb, last), 0)),
            _full(cw.shape), _full(shift.shape), _full(bd.shape), _full(scale.shape),
        ],
        out_specs=(pl.BlockSpec((tm, 2 * DN_W), lambda i: (i, 0)),
                   pl.BlockSpec((tm, 2 * DN_W), lambda i: (i, 0))),
        out_shape=(jax.ShapeDtypeStruct((m, 2 * DN_W), BF16), jax.ShapeDtypeStruct((m, 2 * DN_W), BF16)),
        scratch_shapes=[pltpu.VMEM((tm + 2 * halo, width), BF16)],
        compiler_params=_cparams(("parallel",)),
    )(dnpre, dnpre, dnpre, cw, shift, bd, scale)


def _bd2(a, b):
    return jnp.concatenate([jnp.concatenate([a, jnp.zeros_like(b)], axis=1),
                            jnp.concatenate([jnp.zeros_like(a), b], axis=1)], axis=0)


def _softplus(x):
    return jnp.maximum(x, 0.0) + jnp.log1p(jnp.exp(-jnp.abs(x)))


def _delta_kernel(kqf_ref, kqb_ref, vkf_ref, vkb_ref, dabf_ref, dabb_ref, ea_ref, bias_ref, of_ref, ob_ref, s_ref):
    C = DN_CHUNK
    H = DN_DIM
    n = pl.program_id(1)

    @pl.when(n == 0)
    def _():
        s_ref[...] = jnp.zeros_like(s_ref)

    row = lax.broadcasted_iota(jnp.int32, (C, C), 0)
    col = lax.broadcasted_iota(jnp.int32, (C, C), 1)
    incl = (row >= col, row <= col)
    strict = (row > col, row < col)
    rowp = lax.broadcasted_iota(jnp.int32, (C, 2 * C), 0)
    colp = lax.broadcasted_iota(jnp.int32, (C, 2 * C), 1) % C
    lane = lax.broadcasted_iota(jnp.int32, (C, LANES), 1)
    lo_half = lane < H
    lane_s = lax.broadcasted_iota(jnp.int32, (H, LANES), 1)
    eye_p =jnp.where(rowp == colp, 1.0, 0.0)
    blk2 = rowp // 2 == colp // 2
    levels = []
    b = 2
    while b < C:
        levels.append((rowp // (2 * b) == colp // (2 * b)) & (rowp // b != colp // b))
        b *= 2

    kq_refs = (kqf_ref, kqb_ref)
    vk_refs = (vkf_ref, vkb_ref)
    dab_refs = (dabf_ref, dabb_ref)
    o_refs = (of_ref, ob_ref)

    gc_all, gl_all, beta_all = [], [], []
    for d in range(2):
        ab = dab_refs[d][0]
        g = -ea_ref[d] * _softplus(ab + bias_ref[d])
        beta_all.append(jax.nn.sigmoid(ab))
        tri = jnp.where(incl[d], 1.0, 0.0).astype(BF16)
        g_hi = g.astype(BF16)
        g_lo = (g - g_hi.astype(F32)).astype(BF16)
        gc_all.append(jnp.dot(tri, g_hi, preferred_element_type=F32)
                      + jnp.dot(tri, g_lo, preferred_element_type=F32))
        gl_all.append(jnp.sum(g, axis=0, keepdims=True))

    units = [(d, j) for d in range(2) for j in range(DN_HEADS // 2)]

    def halves(x):
        return x[:, :x.shape[1] // 2], x[:, x.shape[1] // 2:]

    def rhs1(t):
        z = jnp.zeros_like(t)
        return jnp.concatenate([jnp.concatenate([t, z], axis=1), jnp.concatenate([z, t], axis=1)], axis=0)

    kq_p, kqf, vkf, gccol, gcrow, bcol, gl, kT, decay = ({} for _ in range(9))
    for u in units:
        d, j = u
        sl = slice(2 * j * LANES, (2 * j + 2) * LANES)
        kq_p[u] = kq_refs[d][:, sl]
        kqf[u] = kq_p[u].astype(F32)
        vkf[u] = vk_refs[d][:, sl].astype(F32)
        gccol[u], gcrow[u], bcol[u], gl[u], kT[u], decay[u] = [], [], [], [], [], []
        for hh in range(2):
            h = 2 * j + hh
            gccol[u].append(jnp.broadcast_to(gc_all[d][:, h:h + 1], (C, LANES)))
            bcol[u].append(jnp.broadcast_to(beta_all[d][:, DN_HEADS + h:DN_HEADS + h + 1], (C, LANES)))
            gl[u].append(gl_all[d][:, h:h + 1])
            gcrow[u].append(gccol[u][hh].T)
            kT[u].append(kqf[u][:, hh * LANES:(hh + 1) * LANES].T[0:H, :])
            decay[u].append(jnp.where(incl[d], jnp.exp(jnp.minimum(gccol[u][hh] - gcrow[u][hh], 0.0)), 0.0))

    p1 = {u: _mm(kq_p[u], _bd2(rhs1(kT[u][0]), rhs1(kT[u][1]))) for u in units}
    a_p, qkd, x_p = {}, {}, {}
    for u in units:
        d = u[0]
        a_h, qk_h = [], []
        for hh in range(2):
            kk = p1[u][:, (2 * hh) * C:(2 * hh + 1) * C]
            qk = p1[u][:, (2 * hh + 1) * C:(2 * hh + 2) * C]
            a_h.append(jnp.where(strict[d], kk * decay[u][hh], 0.0) * bcol[u][hh])
            qk_h.append(qk * decay[u][hh])
        a_p[u] = jnp.concatenate(a_h, axis=1)
        qkd[u] = jnp.concatenate(qk_h, axis=1)
        x_p[u] = eye_p - jnp.where(blk2, a_p[u], 0.0)

    for msk in levels:
        g_p = {u: _mm(jnp.where(msk, a_p[u], 0.0), _bd2(*halves(x_p[u]))) for u in units}
        x_p = {u: x_p[u] - _mm(x_p[u], _bd2(*halves(g_p[u]))) for u in units}

    uw = {}
    for u in units:
        rhs2 = [vkf[u][:, hh * LANES:(hh + 1) * LANES] * bcol[u][hh]
                * jnp.where(lo_half, 1.0, jnp.exp(gccol[u][hh])) for hh in range(2)]
        uw[u] = _mm(x_p[u], _bd2(rhs2[0], rhs2[1]))
    ol = {u: _mm(qkd[u], _bd2(*halves(uw[u]))) for u in units}
    nw = {}
    for u in units:
        kdT = jnp.concatenate([kT[u][hh] * jnp.exp(gl[u][hh] - gcrow[u][hh][0:H, :]) for hh in range(2)], axis=0)
        nw[u] = _mm(kdT, uw[u])
    for u in units:
        d, j = u
        sl = slice(2 * j * LANES, (2 * j + 2) * LANES)
        zmult = jnp.concatenate(
            [jnp.where(lo_half, jnp.exp(gl[u][hh] - gccol[u][hh]), jnp.exp(gccol[u][hh])) for hh in range(2)], axis=1)
        z_p = kqf[u] * zmult - ol[u]
        nw_h = (nw[u][0:H, 0:LANES], nw[u][H:2 * H, LANES:2 * LANES])
        s_old = [s_ref[d, 2 * j + hh] for hh in range(2)]
        sblk = [jnp.concatenate([jnp.zeros((H, LANES), F32), s_old[hh]], axis=0) for hh in range(2)]
        r = _mm(jnp.concatenate([_bd2(*nw_h), z_p], axis=0), _bd2(sblk[0], sblk[1]))
        o_pair = r[2 * H:, :] + ol[u]
        o_refs[d][:, j * LANES:(j + 1) * LANES] = jnp.where(
            lo_half, o_pair[:, :LANES], pltpu.roll(o_pair[:, LANES:], H, 1))
        wks = (r[0:H, 0:LANES], r[H:2 * H, LANES:2 * LANES])
        for hh in range(2):
            s_new = jnp.exp(gl[u][hh]) * s_old[hh] + nw_h[hh] - wks[hh]
            s_ref[d, 2 * j + hh] = jnp.where(lane_s < H, s_new, 0.0)


def _delta_rule(kq, vk, dab, ea, bias, batch, seq):
    m = kq.shape[0]
    C = DN_CHUNK
    nc = seq // C
    width = kq.shape[1]
    fwd = lambda b, n: (b * nc + n, 0)
    bwd = lambda b, n: (b * nc + nc - 1 - n, 0)
    return pl.pallas_call(
        _delta_kernel, name="delta",
        grid=(batch, nc),
        in_specs=[
            pl.BlockSpec((C, width), fwd), pl.BlockSpec((C, width), bwd),
            pl.BlockSpec((C, width), fwd), pl.BlockSpec((C, width), bwd),
            pl.BlockSpec((1, C, 2 * DN_HEADS), lambda b, n: (0, b * nc + n, 0)),
            pl.BlockSpec((1, C, 2 * DN_HEADS), lambda b, n: (1, b * nc + nc - 1 - n, 0)),
            _full(ea.shape), _full(bias.shape),
        ],
        out_specs=(pl.BlockSpec((C, DN_W), fwd), pl.BlockSpec((C, DN_W), bwd)),
        out_shape=(jax.ShapeDtypeStruct((m, DN_W), F32), jax.ShapeDtypeStruct((m, DN_W), F32)),
        scratch_shapes=[pltpu.VMEM((2, DN_HEADS, DN_DIM, LANES), F32)],
        compiler_params=_cparams(("parallel", "arbitrary")),
    )(kq, kq, vk, vk, dab, dab, ea, bias)


def _merge_kernel(ot_ref, of_ref, ob_ref, dz_ref, sg_ref, x_ref, wa_ref, wb_ref, wo_ref, bdo_ref, onw_ref, n2_ref,
                  wr_hi_ref, wr_lo_ref, x1_ref, xp_ref, lg_ref):
    tm = x_ref.shape[0]
    ya = lax.dot_general(ot_ref[...].reshape(ATT_Q_W, tm), wa_ref[...], (((0,), (0,)), ((), ())),
                         preferred_element_type=F32)
    o = of_ref[...] + ob_ref[...]
    ss = _mm(o * o, bdo_ref[...])
    dz = dz_ref[...].astype(F32)
    dn = o * lax.rsqrt(ss * (1.0 / DN_DIM) + EPS) * onw_ref[...] * (dz * jax.nn.sigmoid(dz))
    yb = _mm(dn, wb_ref[...])
    sg = sg_ref[...].astype(F32)
    mix = sg[:, :D_MODEL] * ya + sg[:, D_MODEL:] * yb
    x1 = x_ref[...] + _mm(mix, wo_ref[...])
    x1_ref[...] = x1
    ms = jnp.mean(x1 * x1, axis=-1, keepdims=True)
    xn = x1 * lax.rsqrt(ms + EPS) * n2_ref[...]
    xn_hi = xn.astype(BF16)
    xn_lo = (xn - xn_hi.astype(F32)).astype(BF16)
    xp_ref[0], xp_ref[1] = _pack_rows(xn)
    whi = wr_hi_ref[...]
    lg_ref[...] = (jnp.dot(xn_hi, whi, preferred_element_type=F32)
                   + jnp.dot(xn_lo, whi, preferred_element_type=F32)
                   + jnp.dot(xn_hi, wr_lo_ref[...], preferred_element_type=F32))


def _merge(ot, o_f, o_b, dz, sg, x2, wa, wb, wo, bdo, onw, n2, wr_hi, wr_lo):
    m = x2.shape[0]
    tm = 256
    row = lambda i: (i, 0)
    return pl.pallas_call(
        _merge_kernel, name="merge", grid=(m // tm,),
        in_specs=[
            pl.BlockSpec((ATT_HEADS, HEAD_DIM, tm), lambda i: (0, 0, i)),
            pl.BlockSpec((tm, o_f.shape[1]), row),
            pl.BlockSpec((tm, o_b.shape[1]), row),
            pl.BlockSpec((tm, dz.shape[1]), row),
            pl.BlockSpec((tm, sg.shape[1]), row),
            pl.BlockSpec((tm, D_MODEL), row),
            _full(wa.shape), _full(wb.shape), _full(wo.shape), _full(bdo.shape), _full(onw.shape),
            _full(n2.shape), _full(wr_hi.shape), _full(wr_lo.shape),
        ],
        out_specs=(pl.BlockSpec((tm, D_MODEL), row), pl.BlockSpec((2, tm, PACK_W), lambda i: (0, i, 0)),
                   pl.BlockSpec((tm, LANES), row)),
        out_shape=(jax.ShapeDtypeStruct((m, D_MODEL), F32), jax.ShapeDtypeStruct((2, m, PACK_W), jnp.uint32),
                   jax.ShapeDtypeStruct((m, LANES), F32)),
        compiler_params=_cparams(("parallel",)),
    )(ot, o_f, o_b, dz, sg, x2, wa, wb, wo, bdo, onw, n2, wr_hi, wr_lo)


def _route_kernel(lg_ref, earlier_ref, info_ref, info_t_ref, cnt_ref):
    @pl.when(pl.program_id(0) == 0)
    def _():
        cnt_ref[...] = jnp.zeros_like(cnt_ref)

    lg = lg_ref[...]
    lane_i = lax.broadcasted_iota(jnp.int32, lg.shape, 1)
    lane = lane_i.astype(F32)
    group_of = ((lane_i - ROUTER_OFF) // EXPERTS_PER_GROUP).astype(F32)
    neg = -jnp.inf

    def first_argmax(v):
        mx = jnp.max(v, axis=-1, keepdims=True)
        idx = jnp.min(jnp.where(v == mx, lane, float(LANES)), axis=-1, keepdims=True)
        return mx, idx

    gl = jnp.where(lane_i < N_GROUPS, lg, neg)
    gmax, gidx = first_argmax(gl)
    gval = 1.0 / jnp.sum(jnp.exp(gl - gmax), axis=-1, keepdims=True)
    is_exp = (lane_i >= ROUTER_OFF) & (lane_i < ROUTER_OFF + N_EXPERTS)
    sel = is_exp & (group_of == gidx)
    el = jnp.where(sel, lg, neg)
    m1, i1 = first_argmax(el)
    el2 = jnp.where(lane == i1, neg, el)
    m2, i2 = first_argmax(el2)
    r = jnp.exp(m2 - m1)
    w1 = gval / (1.0 + r)
    w2 = gval * r / (1.0 + r)
    chosen =jnp.where((lane == i1) | (lane == i2), 1.0, 0.0)
    before = _mm(earlier_ref[...], chosen) + cnt_ref[0:1, :]
    rank1 = jnp.sum(jnp.where(lane == i1, before, 0.0), axis=-1, keepdims=True)
    rank2 = jnp.sum(jnp.where(lane == i2, before, 0.0), axis=-1, keepdims=True)
    cnt_ref[...] = cnt_ref[...] + jnp.sum(chosen, axis=0, keepdims=True)
    info = (jnp.where(lane_i == 0, i1 - ROUTER_OFF, 0.0) + jnp.where(lane_i == 1, i2 - ROUTER_OFF, 0.0)
            + jnp.where(lane_i == 2, w1, 0.0) + jnp.where(lane_i == 3, w2, 0.0)
            + jnp.where(lane_i == 4, rank1, 0.0) + jnp.where(lane_i == 5, rank2, 0.0))
    info_ref[...] = info
    info_t_ref[...] = info.T[0:8, :]


def _route(logits):
    m = logits.shape[0]
    tm = 1024 if m % 1024 == 0 else 256
    spec = pl.BlockSpec((tm, LANES), lambda i: (i, 0))
    t = np.arange(tm)
    earlier = jnp.asarray(t[:, None] > t[None, :], BF16)
    return pl.pallas_call(
        _route_kernel, name="route", grid=(m // tm,), in_specs=[spec, _full(earlier.shape)],
        out_specs=(spec, pl.BlockSpec((8, tm), lambda i: (0, i)), pl.BlockSpec((8, LANES), lambda i: (0, 0))),
        out_shape=(jax.ShapeDtypeStruct((m, LANES), F32), jax.ShapeDtypeStruct((8, m), F32),
                   jax.ShapeDtypeStruct((8, LANES), F32)),
        compiler_params=_cparams(("arbitrary",)),
    )(logits, earlier)


def _pack_rows(x):
    bits = pltpu.bitcast(x.astype(BF16).astype(F32), jnp.uint32)
    half = x.shape[1] // 2
    word = (bits[:, :half] >> 16) | (bits[:, half:] & jnp.uint32(0xFFFF0000))
    return word[:, :PACK_W], word[:, PACK_W:]


def _unpack_rows(w0, w1):
    lo = [pltpu.bitcast(w << 16, F32) for w in (w0, w1)]
    hi = [pltpu.bitcast(w & jnp.uint32(0xFFFF0000), F32) for w in (w0, w1)]
    return jnp.concatenate(lo + hi, axis=1)


def _sc_mesh():
    return plsc.VectorSubcoreMesh(core_axis_name="c", subcore_axis_name="s")


def _sc_scatter_rows(src, idx, n_out, reps):
    s, width = src.shape
    nblk = s // SC_WINDOW

    @pl.kernel(out_type=jax.ShapeDtypeStruct((n_out, width), src.dtype), mesh=_sc_mesh(), scratch_types=[])
    def scatter_kernel(x_hbm, i_hbm, o_hbm):
        def body(x_vmem, i_vmem):
            pltpu.sync_copy(x_vmem, o_hbm.at[i_vmem.at[0]])

        pltpu.emit_pipeline(
            body, grid=(reps * nblk,),
            in_specs=[pl.BlockSpec((SC_WINDOW, width), index_map=lambda i: (i % nblk, 0)),
                      pl.BlockSpec((1, SC_WINDOW), index_map=lambda i: (0, i))],
            out_specs=[], core_axis_name=("c", "s"), dimension_semantics=(pltpu.PARALLEL,),
        )(x_hbm, i_hbm)

    return scatter_kernel(src, idx.reshape(1, reps * s))


def _sc_gather_rows(table, idx):
    k = idx.shape[0]
    width = table.shape[1]

    @pl.kernel(out_type=jax.ShapeDtypeStruct((k, width), table.dtype), mesh=_sc_mesh())
    def gather_kernel(x_hbm, i_hbm, o_hbm):
        def body(i_vmem, o_vmem):
            pltpu.sync_copy(x_hbm.at[i_vmem.at[0]], o_vmem)

        pltpu.emit_pipeline(
            body, grid=(k // SC_WINDOW,),
            in_specs=[pl.BlockSpec((1, SC_WINDOW), index_map=lambda i: (0, i))],
            out_specs=[pl.BlockSpec((SC_WINDOW, width), index_map=lambda i: (i, 0))],
            core_axis_name=("c", "s"), dimension_semantics=(pltpu.PARALLEL,),
        )(i_hbm, o_hbm)

    return gather_kernel(table, idx.reshape(1, k))


def _routing_tables(info_t, counts):
    m = info_t.shape[1]
    e = info_t[0:2].astype(jnp.int32)
    rank = info_t[4:6].astype(jnp.int32)
    cnt = counts[0, ROUTER_OFF:ROUTER_OFF + N_EXPERTS].astype(jnp.int32)
    tiles_e = (cnt + MOE_TILE - 1) // MOE_TILE
    tile_end = jnp.cumsum(tiles_e)
    slot_start = (tile_end - tiles_e) * MOE_TILE
    expert_ids = jnp.arange(N_EXPERTS, dtype=jnp.int32)
    pos = jnp.sum(jnp.where(e[None] == expert_ids[:, None, None], slot_start[:, None, None], 0), axis=0) + rank
    n_tiles = 2 * m // MOE_TILE + N_EXPERTS
    tile_ids = jnp.arange(n_tiles, dtype=jnp.int32)
    tile_expert = jnp.sum(tile_ids[:, None] >= tile_end[None, :], axis=1)
    tile_expert = jnp.minimum(tile_expert, N_EXPERTS - 1).astype(jnp.int32)
    n_used = tile_end[-1:].astype(jnp.int32)
    has = tiles_e > 0
    run_index = jnp.cumsum(has.astype(jnp.int32)) - 1
    later = (expert_ids[None, :] > expert_ids[:, None]) & has[None, :]
    next_e = jnp.min(jnp.where(later, expert_ids[None, :], N_EXPERTS), axis=1)
    next_e = jnp.where(next_e < N_EXPERTS, next_e, -1)
    of_tile = (tile_expert[:, None] == expert_ids[None, :]).astype(jnp.int32)
    lookup = lambda table: jnp.sum(of_tile * table[None, :], axis=1)
    first = (tile_ids == lookup(tile_end - tiles_e)) & (tile_ids < n_used[0])
    sched = (tile_expert, n_used, lookup(next_e).astype(jnp.int32), first.astype(jnp.int32),
             (lookup(run_index) % 2).astype(jnp.int32))
    return pos, sched


def _experts_kernel(te_ref, nu_ref, nxt_ref, first_ref, slot_ref, xs_ref, wg_hbm, wu_hbm, wd_hbm, ys_ref,
                    wg_buf, wu_buf, wd_buf, sem):
    i = pl.program_id(0)

    def weight_copies(e, s):
        return (pltpu.make_async_copy(wg_hbm.at[e], wg_buf.at[s], sem.at[0, s]),
                pltpu.make_async_copy(wu_hbm.at[e], wu_buf.at[s], sem.at[1, s]),
                pltpu.make_async_copy(wd_hbm.at[e], wd_buf.at[s], sem.at[2, s]))

    @pl.when(i == 0)
    def _():
        for c in weight_copies(te_ref[0], slot_ref[0]):
            c.start()

    s = slot_ref[i]

    @pl.when(first_ref[i] == 1)
    def _():
        for c in weight_copies(te_ref[i], s):
            c.wait()

        @pl.when(nxt_ref[i] >= 0)
        def _():
            for c in weight_copies(nxt_ref[i], 1 - s):
                c.start()

    @pl.when(i < nu_ref[0])
    def _():
        x = _unpack_rows(xs_ref[0], xs_ref[1]).astype(BF16)
        hg = _mm(x, wg_buf[s])
        hu = _mm(x, wu_buf[s])
        y = _mm(hg * jax.nn.sigmoid(hg) * hu, wd_buf[s])
        ys_ref[0], ys_ref[1] = _pack_rows(y)


def _experts(xs, sched, wg, wu, wd):
    n_slots = xs.shape[1]
    slots = pl.BlockSpec((2, MOE_TILE, PACK_W), lambda i, *_: (0, i, 0))
    hbm = pl.BlockSpec(memory_space=pl.ANY)
    return pl.pallas_call(
        _experts_kernel, name="experts",
        grid_spec=pltpu.PrefetchScalarGridSpec(
            num_scalar_prefetch=len(sched), grid=(n_slots // MOE_TILE,),
            in_specs=[slots, hbm, hbm, hbm],
            out_specs=slots,
            scratch_shapes=[pltpu.VMEM((2,) + wg.shape[1:], wg.dtype), pltpu.VMEM((2,) + wu.shape[1:], wu.dtype),
                            pltpu.VMEM((2,) + wd.shape[1:], wd.dtype), pltpu.SemaphoreType.DMA((3, 2))],
        ),
        out_shape=jax.ShapeDtypeStruct(xs.shape, jnp.uint32),
        compiler_params=_cparams(("arbitrary",)),
    )(*sched, xs, wg, wu, wd)


def _combine_kernel(x1_ref, info_ref, yg_ref, fw_ref, y_ref):
    info = info_ref[...]
    x = (x1_ref[...] + info[:, 2:3] * _unpack_rows(yg_ref[0, 0], yg_ref[0, 1])
         + info[:, 3:4] * _unpack_rows(yg_ref[1, 0], yg_ref[1, 1]))
    ms = jnp.mean(x * x, axis=-1, keepdims=True)
    y_ref[...] = x * lax.rsqrt(ms + EPS) * fw_ref[...]


def _combine(x1, info, yg, fw):
    m = x1.shape[0]
    tm = 512 if m % 512 == 0 else 256
    row = lambda i: (i, 0)
    return pl.pallas_call(
        _combine_kernel, name="combine", grid=(m // tm,),
        in_specs=[pl.BlockSpec((tm, D_MODEL), row), pl.BlockSpec((tm, LANES), row),
                  pl.BlockSpec((2, 2, tm, PACK_W), lambda i: (0, 0, i, 0)), _full(fw.shape)],
        out_specs=pl.BlockSpec((tm, D_MODEL), row),
        out_shape=jax.ShapeDtypeStruct((m, D_MODEL), F32),
        compiler_params=_cparams(("parallel",)),
    )(x1, info, yg, fw)


def _moe_final(xp, x1, info, info_t, counts, wg, wu, wd, fw):
    m = x1.shape[0]
    pos, sched = _routing_tables(info_t, counts)
    n_slots = 2 * m + N_EXPERTS * MOE_TILE
    idx = jnp.concatenate([h * n_slots + pos[k] for k in range(2) for h in range(2)])
    xs = _sc_scatter_rows(xp.reshape(2 * m, PACK_W), idx, 2 * n_slots, 2)
    ys = _experts(xs.reshape(2, n_slots, PACK_W), sched, wg, wu, wd)
    yg = _sc_gather_rows(ys.reshape(2 * n_slots, PACK_W), idx)
    return _combine(x1, info, yg.reshape(2, 2, m, PACK_W), fw)


def _block_ones(n, blk):
    idx = np.arange(n)
    return jnp.asarray((idx[:, None] // blk == idx[None, :] // blk), dtype=BF16)


def _prepare(norm1_w, w_in, att_q_norm, att_k_norm, dn_conv_w, dn_a_log, dn_dt_bias, dn_out_norm,
             w_branch_att, w_branch_dn, w_out, norm2_w, moe_group_router, moe_expert_router,
             moe_w_gate, moe_w_up, moe_w_down, final_norm_w):
    w_in = w_in[0]
    o_q, o_k, o_v = 0, ATT_Q_W, ATT_Q_W + ATT_KV_W
    o_dq = o_v + ATT_KV_W
    o_dz = o_dq + 3 * DN_W
    o_da = o_dz + DN_W
    o_db = o_da + 2 * DN_HEADS
    o_ga = o_db + 2 * DN_HEADS
    o_gb = o_ga + D_MODEL

    def deinterleave(w, heads):
        lead = w.shape[:-1]
        w = w.reshape(lead + (heads, HEAD_DIM // 2, 2))
        return jnp.swapaxes(w, -1, -2).reshape(lead + (heads * HEAD_DIM,))

    watt = jnp.concatenate([deinterleave(w_in[:, o_q:o_k], ATT_HEADS), deinterleave(w_in[:, o_k:o_v], ATT_KV_HEADS),
                            w_in[:, o_v:o_dq]], axis=1).astype(BF16)
    qnw = jnp.tile(deinterleave(att_q_norm[0], 1), ATT_HEADS)[None, :]
    knw = jnp.tile(deinterleave(att_k_norm[0], 1), ATT_KV_HEADS)[None, :]

    wdn =w_in[:, o_dq:o_dz].astype(BF16)
    cw = jnp.concatenate([dn_conv_w[0], jnp.zeros((8 - DN_CONV, 3 * DN_W), F32)], axis=0)
    scale = jnp.asarray(np.concatenate([np.full(DN_W, DN_DIM ** -0.5), np.ones(DN_W)]), F32)[None, :]

    wdz = w_in[:, o_dz:o_dz + DN_W].astype(BF16)
    onw = jnp.tile(dn_out_norm[0], DN_HEADS)[None, :]
    wb = w_branch_dn[0].astype(BF16)

    wab = jnp.zeros((D_MODEL, 2 * LANES), F32)
    for dirn in range(2):
        wab = wab.at[:, dirn * LANES:dirn * LANES + DN_HEADS].set(
            w_in[:, o_da + dirn * DN_HEADS:o_da + (dirn + 1) * DN_HEADS])
        wab = wab.at[:, dirn * LANES + DN_HEADS:dirn * LANES + 2 * DN_HEADS].set(
            w_in[:, o_db + dirn * DN_HEADS:o_db + (dirn + 1) * DN_HEADS])
    wab = wab.astype(BF16)
    wg = w_in[:, o_ga:o_gb + D_MODEL].astype(BF16)

    zeros8 = jnp.zeros((2, DN_HEADS), F32)
    ea = jnp.concatenate([jnp.exp(dn_a_log[0]), zeros8], axis=1)[:, None, :]
    bias = jnp.concatenate([dn_dt_bias[0], zeros8], axis=1)[:, None, :]


    wr = jnp.concatenate([moe_group_router[0], moe_expert_router[0],
                          jnp.zeros((D_MODEL, LANES - N_GROUPS - N_EXPERTS), F32)], axis=1)
    wr_hi = wr.astype(BF16)
    wr_lo = (wr - wr_hi.astype(F32)).astype(BF16)

    return dict(
        n1=norm1_w[0][None, :], watt=watt, wdn=wdn, wdz=wdz, wab=wab, wg=wg,
        bd_att=_block_ones(ATT_Q_W, HEAD_DIM), qnw=qnw, knw=knw,
        cw=cw, bd_dn=_block_ones(2 * LANES, DN_DIM), scale=scale, ea=ea, bias=bias,
        wa=w_branch_att[0].astype(BF16), wb=wb, wo=w_out[0].astype(BF16), bdo=_block_ones(DN_W, DN_DIM), onw=onw,
        n2=norm2_w[0][None, :], wr_hi=wr_hi, wr_lo=wr_lo,
        wge=moe_w_gate[0], wue=moe_w_up[0], wde=moe_w_down[0],
        fw=final_norm_w[None, :],
    )


def _rope_tables(seq):
    axis_dim = HEAD_DIM // 2
    inv = ROPE_THETA ** (-np.arange(0, axis_dim, 2, dtype=np.float32) / axis_dim)
    lane = np.arange(LANES)
    pair = lane % (HEAD_DIM // 2)
    freq = jnp.asarray(inv[pair % (axis_dim // 2)], F32)[None, :]
    use_row = jnp.asarray(pair < axis_dim // 2)[None, :]
    sign = jnp.asarray(np.where(lane % HEAD_DIM < HEAD_DIM // 2, -1.0, 1.0), F32)[None, :]
    t = jnp.arange(seq, dtype=jnp.int32)[:, None]
    ang = jnp.where(use_row, (t // GRID_W).astype(F32), (t % GRID_W).astype(F32)) * freq
    return jnp.cos(ang), jnp.sin(ang) * sign


def _trunk(x, p):
    batch, seq, _ = x.shape
    x2 = x.reshape(batch * seq, D_MODEL)
    cos_t, sin_t = _rope_tables(seq)
    qt, kn, vt, dnpre, dz, dab, sg, kmx_tiles = _inproj(
        x2, seq, p["n1"], p["watt"], p["wdn"], p["wdz"], p["wab"], p["wg"], p["bd_att"], p["qnw"], p["knw"],
        cos_t, sin_t)
    ot = _attention(qt, kn, vt, kmx_tiles.reshape(batch, -1, 8, ATT_KV_W), batch, seq)
    kq, vk = _dnprep(dnpre, seq, p["cw"], p["bd_dn"], p["scale"])
    o_f, o_b = _delta_rule(kq, vk, dab, p["ea"], p["bias"], batch, seq)
    x1, xp, logits = _merge(ot, o_f, o_b, dz, sg, x2, p["wa"], p["wb"], p["wo"], p["bdo"], p["onw"], p["n2"],
                            p["wr_hi"], p["wr_lo"])
    info, info_t, counts = _route(logits)
    y = _moe_final(xp, x1, info, info_t, counts, p["wge"], p["wue"], p["wde"], p["fw"])
    return y.reshape(batch, seq, D_MODEL)


def kernel(x_prompt, x_sample, norm1_w, w_in, att_q_norm, att_k_norm, dn_conv_w, dn_a_log, dn_dt_bias, dn_out_norm, w_branch_att, w_branch_dn, w_out, norm2_w, moe_group_router, moe_expert_router, moe_w_gate, moe_w_up, moe_w_down, final_norm_w):
    p = _prepare(norm1_w, w_in, att_q_norm, att_k_norm, dn_conv_w, dn_a_log, dn_dt_bias, dn_out_norm,
                 w_branch_att, w_branch_dn, w_out, norm2_w, moe_group_router, moe_expert_router,
                 moe_w_gate, moe_w_up, moe_w_down, final_norm_w)
    return (_trunk(x_prompt, p), _trunk(x_sample, p))
```

```python
import functools
import math

import numpy as np
import jax
import jax.numpy as jnp
from jax import lax
from jax.experimental import pallas as pl
from jax.experimental.pallas import tpu as pltpu
from jax.experimental.pallas import tpu_sc as plsc

F32 = jnp.float32
BF16 = jnp.bfloat16

D_MODEL = 1024
GRID_W = 64
EPS = 1e-6
ATT_HEADS = 8
ATT_KV_HEADS = 2
ATT_GROUP = ATT_HEADS // ATT_KV_HEADS
HEAD_DIM = 64
ROPE_THETA = 10000.0
DN_HEADS = 8
DN_DIM = 64
DN_CONV = 5
N_GROUPS = 4
EXPERTS_PER_GROUP = 8
N_EXPERTS = N_GROUPS * EXPERTS_PER_GROUP
EXPERT_FF = 256

ATT_Q_W = ATT_HEADS * HEAD_DIM
ATT_KV_W = ATT_KV_HEADS * HEAD_DIM
DN_W = DN_HEADS * DN_DIM
LANES = 128
DN_CHUNK = 128
DN_SUBCHUNKS = 4
ROUTER_OFF = N_GROUPS
PACK_W = D_MODEL // 4
SC_WINDOW = 128
MOE_TILE = 512
ATT_TQ = 1024
ATT_TK = 1024
VMEM_LIMIT = 52 * 1024 * 1024
LOG2E = math.log2(math.e)
SCORE_BOUND_MAX = 50.0
SCORE_BOUND_SLACK = 1.05


def _mm(a, b):
    return jnp.dot(a.astype(BF16), b.astype(BF16), preferred_element_type=F32)


def _cparams(sem):
    return pltpu.CompilerParams(dimension_semantics=sem, vmem_limit_bytes=VMEM_LIMIT)


def _full(shape):
    nd = len(shape)
    return pl.BlockSpec(shape, lambda *_: (0,) * nd, pipeline_mode=pl.Buffered(1))


def _rope(x, cos, sin_signed):
    n = x.shape[1]
    lane = lax.broadcasted_iota(jnp.int32, x.shape, 1)
    first = (lane % HEAD_DIM) < (HEAD_DIM // 2)
    partner = jnp.where(first, pltpu.roll(x, n - HEAD_DIM // 2, 1), pltpu.roll(x, HEAD_DIM // 2, 1))
    return x * cos + partner * sin_signed


def _inproj_kernel(x_ref, n1_ref, watt_ref, wdn_ref, wdz_ref, wab_ref, wg_ref, bd_ref, qnw_ref, knw_ref,
                   cos_ref, sin_ref, qt_ref, k_ref, vt_ref, dn_ref, dz_ref, dab_ref, sg_ref, kmx_ref):
    tm = x_ref.shape[0]
    x = x_ref[...]
    ms = jnp.mean(x * x, axis=-1, keepdims=True)
    xn = (x * lax.rsqrt(ms + EPS) * n1_ref[...]).astype(BF16)

    att = jnp.dot(xn, watt_ref[...], preferred_element_type=F32)
    aq = att[:, :ATT_Q_W]
    ak = att[:, ATT_Q_W:ATT_Q_W + ATT_KV_W]
    av = att[:, ATT_Q_W + ATT_KV_W:]
    bd = bd_ref[...]
    qss = _mm(aq * aq, bd)
    kss = _mm(ak * ak, bd[:ATT_KV_W, :ATT_KV_W])
    cos = cos_ref[...]
    sin = sin_ref[...]
    cos4 = jnp.concatenate([cos] * (ATT_Q_W // LANES), axis=1)
    sin4 = jnp.concatenate([sin] * (ATT_Q_W // LANES), axis=1)
    q = aq * lax.rsqrt(qss * (1.0 / HEAD_DIM) + EPS) * qnw_ref[...]
    k = ak * lax.rsqrt(kss * (1.0 / HEAD_DIM) + EPS) * knw_ref[...]
    q = _rope(q, cos4, sin4) * (HEAD_DIM ** -0.5 * LOG2E)
    kmx_ref[0] = jnp.broadcast_to(jnp.max(_mm(k * k, bd[:ATT_KV_W, :ATT_KV_W]), axis=0, keepdims=True),
                                  (8, ATT_KV_W))
    k = _rope(k, cos, sin)
    qt_ref[...] = q.T.reshape(ATT_HEADS, HEAD_DIM, tm).astype(BF16)
    k_ref[0] = k[:, :HEAD_DIM].astype(BF16)
    k_ref[1] = k[:, HEAD_DIM:].astype(BF16)
    vt_ref[...] = av.T.reshape(ATT_KV_HEADS, HEAD_DIM, tm).astype(BF16)

    dn_ref[...] = jnp.dot(xn, wdn_ref[...], preferred_element_type=F32).astype(BF16)
    dz_ref[...] = jnp.dot(xn, wdz_ref[...], preferred_element_type=F32).astype(BF16)
    ab = jnp.dot(xn, wab_ref[...], preferred_element_type=F32)
    dab_ref[0] = ab[:, 0:2 * DN_HEADS]
    dab_ref[1] = ab[:, LANES:LANES + 2 * DN_HEADS]
    sg_ref[...] = jax.nn.sigmoid(jnp.dot(xn, wg_ref[...], preferred_element_type=F32)).astype(BF16)


def _inproj(x2, seq, n1, watt, wdn, wdz, wab, wg, bd, qnw, knw, cos_t, sin_t):
    m = x2.shape[0]
    tm = min(512, seq)
    per_seq = seq // tm
    row = lambda i: (i, 0)
    out_shape = (
        jax.ShapeDtypeStruct((ATT_HEADS, HEAD_DIM, m), BF16),
        jax.ShapeDtypeStruct((ATT_KV_HEADS, m, HEAD_DIM), BF16),
        jax.ShapeDtypeStruct((ATT_KV_HEADS, HEAD_DIM, m), BF16),
        jax.ShapeDtypeStruct((m, 3 * DN_W), BF16),
        jax.ShapeDtypeStruct((m, DN_W), BF16),
        jax.ShapeDtypeStruct((2, m, 2 * DN_HEADS), F32),
        jax.ShapeDtypeStruct((m, 2 * D_MODEL), BF16),
        jax.ShapeDtypeStruct((m // tm, 8, ATT_KV_W), F32),
    )
    in_specs = [
        pl.BlockSpec((tm, D_MODEL), row),
        _full(n1.shape), _full(watt.shape), _full(wdn.shape), _full(wdz.shape), _full(wab.shape),
        _full(wg.shape), _full(bd.shape), _full(qnw.shape), _full(knw.shape),
        pl.BlockSpec((tm, LANES), lambda i: (i % per_seq, 0)),
        pl.BlockSpec((tm, LANES), lambda i: (i % per_seq, 0)),
    ]
    out_specs = (
        pl.BlockSpec((ATT_HEADS, HEAD_DIM, tm), lambda i: (0, 0, i)),
        pl.BlockSpec((ATT_KV_HEADS, tm, HEAD_DIM), lambda i: (0, i, 0)),
        pl.BlockSpec((ATT_KV_HEADS, HEAD_DIM, tm), lambda i: (0, 0, i)),
        pl.BlockSpec((tm, 3 * DN_W), row),
        pl.BlockSpec((tm, DN_W), row),
        pl.BlockSpec((2, tm, 2 * DN_HEADS), lambda i: (0, i, 0)),
        pl.BlockSpec((tm, 2 * D_MODEL), row),
        pl.BlockSpec((1, 8, ATT_KV_W), lambda i: (i, 0, 0)),
    )
    return pl.pallas_call(
        _inproj_kernel, name="inproj", grid=(m // tm,), in_specs=in_specs, out_specs=out_specs, out_shape=out_shape,
        compiler_params=_cparams(("parallel",)),
    )(x2, n1, watt, wdn, wdz, wab, wg, bd, qnw, knw, cos_t, sin_t)


def _attn_kernel(qt_ref, k_ref, vt_ref, kmx_ref, ot_ref, p_ref, *, tk, nk):
    tq = qt_ref.shape[2]
    q = jnp.concatenate([qt_ref[r] for r in range(ATT_GROUP)], axis=1)
    n = q.shape[1]
    qf = q.astype(F32)
    kmx = jnp.max(kmx_ref[0], axis=0)
    head_lanes = lax.broadcasted_iota(jnp.int32, kmx.shape, 1) // HEAD_DIM == pl.program_id(1)
    kmx = jnp.max(jnp.where(head_lanes, kmx, 0.0), axis=1, keepdims=True)[0:1, :]
    bound = jnp.sqrt(jnp.sum(qf * qf, axis=0, keepdims=True) * kmx) * SCORE_BOUND_SLACK
    small = jnp.max(bound) <= SCORE_BOUND_MAX
    ones_rows = 16

    def blocks(kb):
        off = pl.multiple_of(kb * tk, tk)
        return k_ref[0, pl.ds(off, tk), :], vt_ref[0, :, pl.ds(off, tk)]

    def write(out):
        for r in range(ATT_GROUP):
            ot_ref[r] = out[:, r * tq:(r + 1) * tq].astype(BF16)

    @pl.when(small)
    def _():
        def probs(kb):
            s = jnp.dot(blocks(kb)[0], q, preferred_element_type=F32)
            return jnp.exp2(s - bound).astype(BF16)

        def weighted(kb, p):
            vaug = jnp.concatenate([blocks(kb)[1], jnp.ones((ones_rows, tk), BF16)], axis=0)
            return jnp.dot(vaug, p, preferred_element_type=F32)

        acc0 = jnp.zeros((HEAD_DIM + ones_rows, n), F32)
        p_ref[0] = probs(0)
        if nk == 1:
            acc = acc0 + weighted(0, p_ref[0])
        else:
            def pair(kb, acc, last):
                p_ref[1] = probs(kb + 1)
                acc = acc + weighted(kb, p_ref[0])
                if not last:
                    p_ref[0] = probs(kb + 2)
                return acc + weighted(kb + 1, p_ref[1])

            acc = lax.fori_loop(0, nk // 2 - 1, lambda j, a: pair(2 * j, a, False), acc0)
            acc = pair(nk - 2, acc, True)
        write(acc[0:HEAD_DIM] / acc[HEAD_DIM:HEAD_DIM + 1])

    @pl.when(jnp.logical_not(small))
    def _():
        def body(kb, carry):
            m_run, l_run, acc = carry
            kblk, vblk = blocks(kb)
            s = jnp.dot(kblk, q, preferred_element_type=F32)
            m_new = jnp.maximum(m_run, jnp.max(s, axis=0, keepdims=True))
            p = jnp.exp2(s - m_new)
            alpha = jnp.exp2(m_run - m_new)
            l_new = alpha * l_run + jnp.sum(p, axis=0, keepdims=True)
            acc_new = alpha * acc + jnp.dot(vblk, p.astype(BF16), preferred_element_type=F32)
            return m_new, l_new, acc_new

        init = (jnp.full((1, n), -jnp.inf, F32), jnp.zeros((1, n), F32), jnp.zeros((HEAD_DIM, n), F32))
        _, l_fin, acc = lax.fori_loop(0, nk, body, init)
        write(acc / l_fin)


def _attention(qt, kn, vt, kmx, batch, seq):
    m = qt.shape[2]
    tq = min(ATT_TQ, seq)
    tk = min(ATT_TK, seq)
    nq = seq // tq
    qspec = pl.BlockSpec((ATT_GROUP, HEAD_DIM, tq), lambda b, g, i: (g, 0, b * nq + i))
    return pl.pallas_call(
        functools.partial(_attn_kernel, tk=tk, nk=seq // tk), name="attention",
        grid=(batch, ATT_KV_HEADS, nq),
        in_specs=[
            qspec,
            pl.BlockSpec((1, seq, HEAD_DIM), lambda b, g, i: (g, b, 0)),
            pl.BlockSpec((1, HEAD_DIM, seq), lambda b, g, i: (g, 0, b)),
            pl.BlockSpec((1,) + kmx.shape[1:], lambda b, g, i: (b, 0, 0, 0)),
        ],
        out_specs=qspec,
        out_shape=jax.ShapeDtypeStruct((ATT_HEADS, HEAD_DIM, m), BF16),
        scratch_shapes=[pltpu.VMEM((2, tk, ATT_GROUP * tq), BF16)],
        compiler_params=_cparams(("parallel", "parallel", "parallel")),
    )(qt, kn, vt, kmx)


def _dnprep_kernel(cur_ref, prev_ref, next_ref, cw_ref, shift_ref, bd_ref, scale_ref, kq_ref, vk_ref,
                   buf_ref, *, seq):
    tm = cur_ref.shape[0]
    halo = prev_ref.shape[0]
    blk = shift_ref.shape[1] - 2 * halo
    i = pl.program_id(0)
    pos = (i * tm) % seq
    has_prev = pos != 0
    has_next = pos + tm != seq
    buf_ref[0:halo, :] = jnp.where(has_prev, prev_ref[...], jnp.zeros_like(prev_ref))
    buf_ref[halo:halo + tm, :] = cur_ref[...]
    buf_ref[halo + tm:, :] = jnp.where(has_next, next_ref[...], jnp.zeros_like(next_ref))
    cw = cw_ref[...]
    pad = DN_CONV // 2
    shift = shift_ref[...]
    rows = []
    for r in range(tm // blk):
        win = buf_ref[r * blk:(r + 1) * blk + 2 * halo, :]
        shifted = jnp.dot(shift, win, preferred_element_type=F32)
        acc = win[halo:halo + blk, :].astype(F32) * cw[pad:pad + 1, :]
        for n, j in enumerate(t for t in range(DN_CONV) if t != pad):
            acc = acc + shifted[n * blk:(n + 1) * blk, :] * cw[j:j + 1, :]
        rows.append(acc)
    y = jnp.concatenate(rows, axis=0) if len(rows) > 1 else rows[0]
    y = y * jax.nn.sigmoid(y)
    bd = bd_ref[...]
    w = bd.shape[0]
    qk = y[:, :2 * DN_W]
    qksq = qk * qk
    ss = jnp.concatenate([_mm(qksq[:, c * w:(c + 1) * w], bd) for c in range(2 * DN_W // w)], axis=1)
    qk = qk * lax.rsqrt(ss + EPS) * scale_ref[...]
    lo_half = lax.broadcasted_iota(jnp.int32, (tm, LANES), 1) < DN_DIM
    kq_cols, vk_cols = [], []
    for j in range(DN_HEADS // 2):
        sl = slice(j * LANES, (j + 1) * LANES)
        qb = qk[:, sl]
        kb = qk[:, DN_W + j * LANES:DN_W + (j + 1) * LANES]
        vb = y[:, 2 * DN_W + j * LANES:2 * DN_W + (j + 1) * LANES]
        qr, kr, vr = (pltpu.roll(t, DN_DIM, 1) for t in (qb, kb, vb))
        kq_cols += [jnp.where(lo_half, kb, qr), jnp.where(lo_half, kr, qb)]
        vk_cols += [jnp.where(lo_half, vb, kr), jnp.where(lo_half, vr, kb)]
    kq_ref[...] = jnp.concatenate(kq_cols, axis=1).astype(BF16)
    vk_ref[...] = jnp.concatenate(vk_cols, axis=1).astype(BF16)


def _dnprep(dnpre, seq, cw, bd, scale):
    m, width = dnpre.shape
    tm = min(512, seq)
    halo = 16
    hb = tm // halo
    last = m // halo - 1
    blk = min(128, tm)
    pad = DN_CONV // 2
    t = np.arange(blk)
    shift = np.zeros((DN_CONV - 1, blk, blk + 2 * halo), np.float32)
    for n, j in enumerate(x for x in range(DN_CONV) if x != pad):
        shift[n, t, halo + t + j - pad] = 1.0
    shift = jnp.asarray(shift.reshape((DN_CONV - 1) * blk, blk + 2 * halo), BF16)
    return pl.pallas_call(
        functools.partial(_dnprep_kernel, seq=seq), name="dnprep",
        grid=(m // tm,),
        in_specs=[
            pl.BlockSpec((tm, width), lambda i: (i, 0)),
            pl.BlockSpec((halo, width), lambda i: (jnp.maximum(i * hb - 1, 0), 0)),
            pl.BlockSpec((halo, width), lambda i: (jnp.minimum((i + 1) * hb, last), 0)),
            _full(cw.shape), _full(shift.shape), _full(bd.shape), _full(scale.shape),
        ],
        out_specs=(pl.BlockSpec((tm, 2 * DN_W), lambda i: (i, 0)),
                   pl.BlockSpec((tm, 2 * DN_W), lambda i: (i, 0))),
        out_shape=(jax.ShapeDtypeStruct((m, 2 * DN_W), BF16), jax.ShapeDtypeStruct((m, 2 * DN_W), BF16)),
        scratch_shapes=[pltpu.VMEM((tm + 2 * halo, width), BF16)],
        compiler_params=_cparams(("parallel",)),
    )(dnpre, dnpre, dnpre, cw, shift, bd, scale)


def _bd2(a, b):
    return jnp.concatenate([jnp.concatenate([a, jnp.zeros_like(b)], axis=1),
                            jnp.concatenate([jnp.zeros_like(a), b], axis=1)], axis=0)


def _softplus(x):
    return jnp.maximum(x, 0.0) + jnp.log1p(jnp.exp(-jnp.abs(x)))


def _delta_kernel(kqf_ref, kqb_ref, vkf_ref, vkb_ref, dabf_ref, dabb_ref, ea_ref, bias_ref, of_ref, ob_ref, s_ref):
    @pl.when(pl.program_id(1) == 0)
    def _():
        s_ref[...] = jnp.zeros_like(s_ref)

    for sub in range(DN_SUBCHUNKS):
        rf = pl.ds(sub * DN_CHUNK, DN_CHUNK)
        rb = pl.ds((DN_SUBCHUNKS - 1 - sub) * DN_CHUNK, DN_CHUNK)
        _delta_chunk(kqf_ref.at[rf], kqb_ref.at[rb], vkf_ref.at[rf], vkb_ref.at[rb], dabf_ref.at[:, rf],
                     dabb_ref.at[:, rb], ea_ref, bias_ref, of_ref.at[rf], ob_ref.at[rb], s_ref)


def _delta_chunk(kqf_ref, kqb_ref, vkf_ref, vkb_ref, dabf_ref, dabb_ref, ea_ref, bias_ref, of_ref, ob_ref, s_ref):
    C = DN_CHUNK
    H = DN_DIM
    row = lax.broadcasted_iota(jnp.int32, (C, C), 0)
    col = lax.broadcasted_iota(jnp.int32, (C, C), 1)
    incl = (row >= col, row <= col)
    strict = (row > col, row < col)
    rowp = lax.broadcasted_iota(jnp.int32, (C, 2 * C), 0)
    colp = lax.broadcasted_iota(jnp.int32, (C, 2 * C), 1) % C
    lane = lax.broadcasted_iota(jnp.int32, (C, LANES), 1)
    lo_half = lane < H
    lane_s = lax.broadcasted_iota(jnp.int32, (H, LANES), 1)
    eye_p =jnp.where(rowp == colp, 1.0, 0.0)
    blk2 = rowp // 2 == colp // 2
    levels = []
    b = 2
    while b < C:
        levels.append((rowp // (2 * b) == colp // (2 * b)) & (rowp // b != colp // b))
        b *= 2

    kq_refs = (kqf_ref, kqb_ref)
    vk_refs = (vkf_ref, vkb_ref)
    dab_refs = (dabf_ref, dabb_ref)
    o_refs = (of_ref, ob_ref)

    gc_all, gl_all, beta_all = [], [], []
    for d in range(2):
        ab = dab_refs[d][0]
        g = -ea_ref[d] * _softplus(ab + bias_ref[d])
        beta_all.append(jax.nn.sigmoid(ab))
        tri = jnp.where(incl[d], 1.0, 0.0).astype(BF16)
        g_hi = g.astype(BF16)
        g_lo = (g - g_hi.astype(F32)).astype(BF16)
        gc_all.append(jnp.dot(tri, g_hi, preferred_element_type=F32)
                      + jnp.dot(tri, g_lo, preferred_element_type=F32))
        gl_all.append(jnp.sum(g, axis=0, keepdims=True))

    units = [(d, j) for d in range(2) for j in range(DN_HEADS // 2)]

    def halves(x):
        return x[:, :x.shape[1] // 2], x[:, x.shape[1] // 2:]

    def rhs1(t):
        z = jnp.zeros_like(t)
        return jnp.concatenate([jnp.concatenate([t, z], axis=1), jnp.concatenate([z, t], axis=1)], axis=0)

    kq_p, kqf, vkf, gccol, gcrow, bcol, gl, kT, decay = ({} for _ in range(9))
    for u in units:
        d, j = u
        sl = slice(2 * j * LANES, (2 * j + 2) * LANES)
        kq_p[u] = kq_refs[d][:, sl]
        kqf[u] = kq_p[u].astype(F32)
        vkf[u] = vk_refs[d][:, sl].astype(F32)
        gccol[u], gcrow[u], bcol[u], gl[u], kT[u], decay[u] = [], [], [], [], [], []
        for hh in range(2):
            h = 2 * j + hh
            gccol[u].append(jnp.broadcast_to(gc_all[d][:, h:h + 1], (C, LANES)))
            bcol[u].append(jnp.broadcast_to(beta_all[d][:, DN_HEADS + h:DN_HEADS + h + 1], (C, LANES)))
            gl[u].append(gl_all[d][:, h:h + 1])
            gcrow[u].append(gccol[u][hh].T)
            kT[u].append(kqf[u][:, hh * LANES:(hh + 1) * LANES].T[0:H, :])
            decay[u].append(jnp.where(incl[d], jnp.exp(jnp.minimum(gccol[u][hh] - gcrow[u][hh], 0.0)), 0.0))

    p1 = {u: _mm(kq_p[u], _bd2(rhs1(kT[u][0]), rhs1(kT[u][1]))) for u in units}
    a_p, qkd, x_p = {}, {}, {}
    for u in units:
        d = u[0]
        a_h, qk_h = [], []
        for hh in range(2):
            kk = p1[u][:, (2 * hh) * C:(2 * hh + 1) * C]
            qk = p1[u][:, (2 * hh + 1) * C:(2 * hh + 2) * C]
            a_h.append(jnp.where(strict[d], kk * decay[u][hh], 0.0) * bcol[u][hh])
            qk_h.append(qk * decay[u][hh])
        a_p[u] = jnp.concatenate(a_h, axis=1)
        qkd[u] = jnp.concatenate(qk_h, axis=1)
        x_p[u] = eye_p - jnp.where(blk2, a_p[u], 0.0)

    for msk in levels:
        g_p = {u: _mm(jnp.where(msk, a_p[u], 0.0), _bd2(*halves(x_p[u]))) for u in units}
        x_p = {u: x_p[u] - _mm(x_p[u], _bd2(*halves(g_p[u]))) for u in units}

    uw = {}
    for u in units:
        rhs2 = [vkf[u][:, hh * LANES:(hh + 1) * LANES] * bcol[u][hh]
                * jnp.where(lo_half, 1.0, jnp.exp(gccol[u][hh])) for hh in range(2)]
        uw[u] = _mm(x_p[u], _bd2(rhs2[0], rhs2[1]))
    ol = {u: _mm(qkd[u], _bd2(*halves(uw[u]))) for u in units}
    nw = {}
    for u in units:
        kdT = jnp.concatenate([kT[u][hh] * jnp.exp(gl[u][hh] - gcrow[u][hh][0:H, :]) for hh in range(2)], axis=0)
        nw[u] = _mm(kdT, uw[u])
    for u in units:
        d, j = u
        sl = slice(2 * j * LANES, (2 * j + 2) * LANES)
        zmult = jnp.concatenate(
            [jnp.where(lo_half, jnp.exp(gl[u][hh] - gccol[u][hh]), jnp.exp(gccol[u][hh])) for hh in range(2)], axis=1)
        z_p = kqf[u] * zmult - ol[u]
        nw_h = (nw[u][0:H, 0:LANES], nw[u][H:2 * H, LANES:2 * LANES])
        s_old = [s_ref[d, 2 * j + hh] for hh in range(2)]
        sblk = [jnp.concatenate([jnp.zeros((H, LANES), F32), s_old[hh]], axis=0) for hh in range(2)]
        r = _mm(jnp.concatenate([_bd2(*nw_h), z_p], axis=0), _bd2(sblk[0], sblk[1]))
        o_pair = r[2 * H:, :] + ol[u]
        o_refs[d][:, j * LANES:(j + 1) * LANES] = jnp.where(
            lo_half, o_pair[:, :LANES], pltpu.roll(o_pair[:, LANES:], H, 1))
        wks = (r[0:H, 0:LANES], r[H:2 * H, LANES:2 * LANES])
        for hh in range(2):
            s_new = jnp.exp(gl[u][hh]) * s_old[hh] + nw_h[hh] - wks[hh]
            s_ref[d, 2 * j + hh] = jnp.where(lane_s < H, s_new, 0.0)


def _delta_rule(kq, vk, dab, ea, bias, batch, seq):
    m = kq.shape[0]
    C = DN_CHUNK * DN_SUBCHUNKS
    nc = seq // C
    width = kq.shape[1]
    fwd = lambda b, n: (b * nc + n, 0)
    bwd = lambda b, n: (b * nc + nc - 1 - n, 0)
    return pl.pallas_call(
        _delta_kernel, name="delta",
        grid=(batch, nc),
        in_specs=[
            pl.BlockSpec((C, width), fwd), pl.BlockSpec((C, width), bwd),
            pl.BlockSpec((C, width), fwd), pl.BlockSpec((C, width), bwd),
            pl.BlockSpec((1, C, 2 * DN_HEADS), lambda b, n: (0, b * nc + n, 0)),
            pl.BlockSpec((1, C, 2 * DN_HEADS), lambda b, n: (1, b * nc + nc - 1 - n, 0)),
            _full(ea.shape), _full(bias.shape),
        ],
        out_specs=(pl.BlockSpec((C, DN_W), fwd), pl.BlockSpec((C, DN_W), bwd)),
        out_shape=(jax.ShapeDtypeStruct((m, DN_W), F32), jax.ShapeDtypeStruct((m, DN_W), F32)),
        scratch_shapes=[pltpu.VMEM((2, DN_HEADS, DN_DIM, LANES), F32)],
        compiler_params=_cparams(("parallel", "arbitrary")),
    )(kq, kq, vk, vk, dab, dab, ea, bias)


def _merge_kernel(ot_ref, of_ref, ob_ref, dz_ref, sg_ref, x_ref, wa_ref, wb_ref, wo_ref, bdo_ref, onw_ref, n2_ref,
                  wr_hi_ref, wr_lo_ref, x1_ref, xp_ref, lg_ref):
    tm = x_ref.shape[0]
    ya = lax.dot_general(ot_ref[...].reshape(ATT_Q_W, tm), wa_ref[...], (((0,), (0,)), ((), ())),
                         preferred_element_type=F32)
    o = of_ref[...] + ob_ref[...]
    ss = _mm(o * o, bdo_ref[...])
    dz = dz_ref[...].astype(F32)
    dn = o * lax.rsqrt(ss * (1.0 / DN_DIM) + EPS) * onw_ref[...] * (dz * jax.nn.sigmoid(dz))
    yb = _mm(dn, wb_ref[...])
    sg = sg_ref[...].astype(F32)
    mix = sg[:, :D_MODEL] * ya + sg[:, D_MODEL:] * yb
    x1 = x_ref[...] + _mm(mix, wo_ref[...])
    x1_ref[...] = x1
    ms = jnp.mean(x1 * x1, axis=-1, keepdims=True)
    xn = x1 * lax.rsqrt(ms + EPS) * n2_ref[...]
    xn_hi = xn.astype(BF16)
    xn_lo = (xn - xn_hi.astype(F32)).astype(BF16)
    xp_ref[0], xp_ref[1] = _pack_rows(xn)
    whi = wr_hi_ref[...]
    lg_ref[...] = (jnp.dot(xn_hi, whi, preferred_element_type=F32)
                   + jnp.dot(xn_lo, whi, preferred_element_type=F32)
                   + jnp.dot(xn_hi, wr_lo_ref[...], preferred_element_type=F32))


def _merge(ot, o_f, o_b, dz, sg, x2, wa, wb, wo, bdo, onw, n2, wr_hi, wr_lo):
    m = x2.shape[0]
    tm = 256
    row = lambda i: (i, 0)
    return pl.pallas_call(
        _merge_kernel, name="merge", grid=(m // tm,),
        in_specs=[
            pl.BlockSpec((ATT_HEADS, HEAD_DIM, tm), lambda i: (0, 0, i)),
            pl.BlockSpec((tm, o_f.shape[1]), row),
            pl.BlockSpec((tm, o_b.shape[1]), row),
            pl.BlockSpec((tm, dz.shape[1]), row),
            pl.BlockSpec((tm, sg.shape[1]), row),
            pl.BlockSpec((tm, D_MODEL), row),
            _full(wa.shape), _full(wb.shape), _full(wo.shape), _full(bdo.shape), _full(onw.shape),
            _full(n2.shape), _full(wr_hi.shape), _full(wr_lo.shape),
        ],
        out_specs=(pl.BlockSpec((tm, D_MODEL), row), pl.BlockSpec((2, tm, PACK_W), lambda i: (0, i, 0)),
                   pl.BlockSpec((tm, LANES), row)),
        out_shape=(jax.ShapeDtypeStruct((m, D_MODEL), F32), jax.ShapeDtypeStruct((2, m, PACK_W), jnp.uint32),
                   jax.ShapeDtypeStruct((m, LANES), F32)),
        compiler_params=_cparams(("parallel",)),
    )(ot, o_f, o_b, dz, sg, x2, wa, wb, wo, bdo, onw, n2, wr_hi, wr_lo)


def _route_kernel(lg_ref, earlier_ref, info_ref, info_t_ref, cnt_ref):
    @pl.when(pl.program_id(0) == 0)
    def _():
        cnt_ref[...] = jnp.zeros_like(cnt_ref)

    lg = lg_ref[...]
    lane_i = lax.broadcasted_iota(jnp.int32, lg.shape, 1)
    lane = lane_i.astype(F32)
    group_of = ((lane_i - ROUTER_OFF) // EXPERTS_PER_GROUP).astype(F32)
    neg = -jnp.inf

    def first_argmax(v):
        mx = jnp.max(v, axis=-1, keepdims=True)
        idx = jnp.min(jnp.where(v == mx, lane, float(LANES)), axis=-1, keepdims=True)
        return mx, idx

    gl = jnp.where(lane_i < N_GROUPS, lg, neg)
    gmax, gidx = first_argmax(gl)
    gval = 1.0 / jnp.sum(jnp.exp(gl - gmax), axis=-1, keepdims=True)
    is_exp = (lane_i >= ROUTER_OFF) & (lane_i < ROUTER_OFF + N_EXPERTS)
    sel = is_exp & (group_of == gidx)
    el = jnp.where(sel, lg, neg)
    m1, i1 = first_argmax(el)
    el2 = jnp.where(lane == i1, neg, el)
    m2, i2 = first_argmax(el2)
    r = jnp.exp(m2 - m1)
    w1 = gval / (1.0 + r)
    w2 = gval * r / (1.0 + r)
    chosen =jnp.where((lane == i1) | (lane == i2), 1.0, 0.0)
    before = _mm(earlier_ref[...], chosen) + cnt_ref[0:1, :]
    rank1 = jnp.sum(jnp.where(lane == i1, before, 0.0), axis=-1, keepdims=True)
    rank2 = jnp.sum(jnp.where(lane == i2, before, 0.0), axis=-1, keepdims=True)
    cnt_ref[...] = cnt_ref[...] + jnp.sum(chosen, axis=0, keepdims=True)
    info = (jnp.where(lane_i == 0, i1 - ROUTER_OFF, 0.0) + jnp.where(lane_i == 1, i2 - ROUTER_OFF, 0.0)
            + jnp.where(lane_i == 2, w1, 0.0) + jnp.where(lane_i == 3, w2, 0.0)
            + jnp.where(lane_i == 4, rank1, 0.0) + jnp.where(lane_i == 5, rank2, 0.0))
    info_ref[...] = info
    info_t_ref[...] = info.T[0:8, :]


def _route(logits):
    m = logits.shape[0]
    tm = 1024 if m % 1024 == 0 else 256
    spec = pl.BlockSpec((tm, LANES), lambda i: (i, 0))
    t = np.arange(tm)
    earlier = jnp.asarray(t[:, None] > t[None, :], BF16)
    return pl.pallas_call(
        _route_kernel, name="route", grid=(m // tm,), in_specs=[spec, _full(earlier.shape)],
        out_specs=(spec, pl.BlockSpec((8, tm), lambda i: (0, i)), pl.BlockSpec((8, LANES), lambda i: (0, 0))),
        out_shape=(jax.ShapeDtypeStruct((m, LANES), F32), jax.ShapeDtypeStruct((8, m), F32),
                   jax.ShapeDtypeStruct((8, LANES), F32)),
        compiler_params=_cparams(("arbitrary",)),
    )(logits, earlier)


def _pack_rows(x):
    bits = pltpu.bitcast(x.astype(BF16).astype(F32), jnp.uint32)
    half = x.shape[1] // 2
    word = (bits[:, :half] >> 16) | (bits[:, half:] & jnp.uint32(0xFFFF0000))
    return word[:, :PACK_W], word[:, PACK_W:]


def _unpack_rows(w0, w1):
    lo = [pltpu.bitcast(w << 16, F32) for w in (w0, w1)]
    hi = [pltpu.bitcast(w & jnp.uint32(0xFFFF0000), F32) for w in (w0, w1)]
    return jnp.concatenate(lo + hi, axis=1)


def _sc_mesh():
    return plsc.VectorSubcoreMesh(core_axis_name="c", subcore_axis_name="s")


def _sc_scatter_rows(src, idx, n_out, reps):
    s, width = src.shape
    nblk = s // SC_WINDOW

    @pl.kernel(out_type=jax.ShapeDtypeStruct((n_out, width), src.dtype), mesh=_sc_mesh(), scratch_types=[])
    def scatter_kernel(x_hbm, i_hbm, o_hbm):
        def body(x_vmem, i_vmem):
            pltpu.sync_copy(x_vmem, o_hbm.at[i_vmem.at[0]])

        pltpu.emit_pipeline(
            body, grid=(reps * nblk,),
            in_specs=[pl.BlockSpec((SC_WINDOW, width), index_map=lambda i: (i % nblk, 0)),
                      pl.BlockSpec((1, SC_WINDOW), index_map=lambda i: (0, i))],
            out_specs=[], core_axis_name=("c", "s"), dimension_semantics=(pltpu.PARALLEL,),
        )(x_hbm, i_hbm)

    return scatter_kernel(src, idx.reshape(1, reps * s))


def _sc_gather_rows(table, idx):
    k = idx.shape[0]
    width = table.shape[1]

    @pl.kernel(out_type=jax.ShapeDtypeStruct((k, width), table.dtype), mesh=_sc_mesh())
    def gather_kernel(x_hbm, i_hbm, o_hbm):
        def body(i_vmem, o_vmem):
            pltpu.sync_copy(x_hbm.at[i_vmem.at[0]], o_vmem)

        pltpu.emit_pipeline(
            body, grid=(k // SC_WINDOW,),
            in_specs=[pl.BlockSpec((1, SC_WINDOW), index_map=lambda i: (0, i))],
            out_specs=[pl.BlockSpec((SC_WINDOW, width), index_map=lambda i: (i, 0))],
            core_axis_name=("c", "s"), dimension_semantics=(pltpu.PARALLEL,),
        )(i_hbm, o_hbm)

    return gather_kernel(table, idx.reshape(1, k))


def _routing_tables(info_t, counts):
    m = info_t.shape[1]
    e = info_t[0:2].astype(jnp.int32)
    rank = info_t[4:6].astype(jnp.int32)
    cnt = counts[0, ROUTER_OFF:ROUTER_OFF + N_EXPERTS].astype(jnp.int32)
    tiles_e = (cnt + MOE_TILE - 1) // MOE_TILE
    tile_end = jnp.cumsum(tiles_e)
    slot_start = (tile_end - tiles_e) * MOE_TILE
    expert_ids = jnp.arange(N_EXPERTS, dtype=jnp.int32)
    pos = jnp.sum(jnp.where(e[None] == expert_ids[:, None, None], slot_start[:, None, None], 0), axis=0) + rank
    n_tiles = 2 * m // MOE_TILE + N_EXPERTS
    tile_ids = jnp.arange(n_tiles, dtype=jnp.int32)
    tile_expert = jnp.sum(tile_ids[:, None] >= tile_end[None, :], axis=1)
    tile_expert = jnp.minimum(tile_expert, N_EXPERTS - 1).astype(jnp.int32)
    n_used = tile_end[-1:].astype(jnp.int32)
    has = tiles_e > 0
    run_index = jnp.cumsum(has.astype(jnp.int32)) - 1
    later = (expert_ids[None, :] > expert_ids[:, None]) & has[None, :]
    next_e = jnp.min(jnp.where(later, expert_ids[None, :], N_EXPERTS), axis=1)
    next_e = jnp.where(next_e < N_EXPERTS, next_e, -1)
    of_tile = (tile_expert[:, None] == expert_ids[None, :]).astype(jnp.int32)
    lookup = lambda table: jnp.sum(of_tile * table[None, :], axis=1)
    first = (tile_ids == lookup(tile_end - tiles_e)) & (tile_ids < n_used[0])
    sched = (tile_expert, n_used, lookup(next_e).astype(jnp.int32), first.astype(jnp.int32),
             (lookup(run_index) % 2).astype(jnp.int32))
    return pos, sched


def _experts_kernel(te_ref, nu_ref, nxt_ref, first_ref, slot_ref, xs_ref, wg_hbm, wu_hbm, wd_hbm, ys_ref,
                    wg_buf, wu_buf, wd_buf, sem):
    i = pl.program_id(0)

    def weight_copies(e, s):
        return (pltpu.make_async_copy(wg_hbm.at[e], wg_buf.at[s], sem.at[0, s]),
                pltpu.make_async_copy(wu_hbm.at[e], wu_buf.at[s], sem.at[1, s]),
                pltpu.make_async_copy(wd_hbm.at[e], wd_buf.at[s], sem.at[2, s]))

    @pl.when(i == 0)
    def _():
        for c in weight_copies(te_ref[0], slot_ref[0]):
            c.start()

    s = slot_ref[i]

    @pl.when(first_ref[i] == 1)
    def _():
        for c in weight_copies(te_ref[i], s):
            c.wait()

        @pl.when(nxt_ref[i] >= 0)
        def _():
            for c in weight_copies(nxt_ref[i], 1 - s):
                c.start()

    @pl.when(i < nu_ref[0])
    def _():
        x = _unpack_rows(xs_ref[0], xs_ref[1]).astype(BF16)
        hg = _mm(x, wg_buf[s])
        hu = _mm(x, wu_buf[s])
        y = _mm(hg * jax.nn.sigmoid(hg) * hu, wd_buf[s])
        ys_ref[0], ys_ref[1] = _pack_rows(y)


def _experts(xs, sched, wg, wu, wd):
    n_slots = xs.shape[1]
    slots = pl.BlockSpec((2, MOE_TILE, PACK_W), lambda i, *_: (0, i, 0))
    hbm = pl.BlockSpec(memory_space=pl.ANY)
    return pl.pallas_call(
        _experts_kernel, name="experts",
        grid_spec=pltpu.PrefetchScalarGridSpec(
            num_scalar_prefetch=len(sched), grid=(n_slots // MOE_TILE,),
            in_specs=[slots, hbm, hbm, hbm],
            out_specs=slots,
            scratch_shapes=[pltpu.VMEM((2,) + wg.shape[1:], wg.dtype), pltpu.VMEM((2,) + wu.shape[1:], wu.dtype),
                            pltpu.VMEM((2,) + wd.shape[1:], wd.dtype), pltpu.SemaphoreType.DMA((3, 2))],
        ),
        out_shape=jax.ShapeDtypeStruct(xs.shape, jnp.uint32),
        compiler_params=_cparams(("arbitrary",)),
    )(*sched, xs, wg, wu, wd)


def _combine_kernel(x1_ref, info_ref, yg_ref, fw_ref, y_ref):
    info = info_ref[...]
    x = (x1_ref[...] + info[:, 2:3] * _unpack_rows(yg_ref[0, 0], yg_ref[0, 1])
         + info[:, 3:4] * _unpack_rows(yg_ref[1, 0], yg_ref[1, 1]))
    ms = jnp.mean(x * x, axis=-1, keepdims=True)
    y_ref[...] = x * lax.rsqrt(ms + EPS) * fw_ref[...]


def _combine(x1, info, yg, fw):
    m = x1.shape[0]
    tm = 1024 if m % 1024 == 0 else 256
    row = lambda i: (i, 0)
    return pl.pallas_call(
        _combine_kernel, name="combine", grid=(m // tm,),
        in_specs=[pl.BlockSpec((tm, D_MODEL), row), pl.BlockSpec((tm, LANES), row),
                  pl.BlockSpec((2, 2, tm, PACK_W), lambda i: (0, 0, i, 0)), _full(fw.shape)],
        out_specs=pl.BlockSpec((tm, D_MODEL), row),
        out_shape=jax.ShapeDtypeStruct((m, D_MODEL), F32),
        compiler_params=_cparams(("parallel",)),
    )(x1, info, yg, fw)


def _moe_final(xp, x1, info, info_t, counts, wg, wu, wd, fw):
    m = x1.shape[0]
    pos, sched = _routing_tables(info_t, counts)
    n_slots = 2 * m + N_EXPERTS * MOE_TILE
    idx = jnp.concatenate([h * n_slots + pos[k] for k in range(2) for h in range(2)])
    xs = _sc_scatter_rows(xp.reshape(2 * m, PACK_W), idx, 2 * n_slots, 2)
    ys = _experts(xs.reshape(2, n_slots, PACK_W), sched, wg, wu, wd)
    yg = _sc_gather_rows(ys.reshape(2 * n_slots, PACK_W), idx)
    return _combine(x1, info, yg.reshape(2, 2, m, PACK_W), fw)


def _block_ones(n, blk):
    idx = np.arange(n)
    return jnp.asarray((idx[:, None] // blk == idx[None, :] // blk), dtype=BF16)


def _prepare(norm1_w, w_in, att_q_norm, att_k_norm, dn_conv_w, dn_a_log, dn_dt_bias, dn_out_norm,
             w_branch_att, w_branch_dn, w_out, norm2_w, moe_group_router, moe_expert_router,
             moe_w_gate, moe_w_up, moe_w_down, final_norm_w):
    w_in = w_in[0]
    o_q, o_k, o_v = 0, ATT_Q_W, ATT_Q_W + ATT_KV_W
    o_dq = o_v + ATT_KV_W
    o_dz = o_dq + 3 * DN_W
    o_da = o_dz + DN_W
    o_db = o_da + 2 * DN_HEADS
    o_ga = o_db + 2 * DN_HEADS
    o_gb = o_ga + D_MODEL

    def deinterleave(w, heads):
        lead = w.shape[:-1]
        w = w.reshape(lead + (heads, HEAD_DIM // 2, 2))
        return jnp.swapaxes(w, -1, -2).reshape(lead + (heads * HEAD_DIM,))

    watt = jnp.concatenate([deinterleave(w_in[:, o_q:o_k], ATT_HEADS), deinterleave(w_in[:, o_k:o_v], ATT_KV_HEADS),
                            w_in[:, o_v:o_dq]], axis=1).astype(BF16)
    qnw = jnp.tile(deinterleave(att_q_norm[0], 1), ATT_HEADS)[None, :]
    knw = jnp.tile(deinterleave(att_k_norm[0], 1), ATT_KV_HEADS)[None, :]

    wdn =w_in[:, o_dq:o_dz].astype(BF16)
    cw = jnp.concatenate([dn_conv_w[0], jnp.zeros((8 - DN_CONV, 3 * DN_W), F32)], axis=0)
    scale = jnp.asarray(np.concatenate([np.full(DN_W, DN_DIM ** -0.5), np.ones(DN_W)]), F32)[None, :]

    wdz = w_in[:, o_dz:o_dz + DN_W].astype(BF16)
    onw = jnp.tile(dn_out_norm[0], DN_HEADS)[None, :]
    wb = w_branch_dn[0].astype(BF16)

    wab = jnp.zeros((D_MODEL, 2 * LANES), F32)
    for dirn in range(2):
        wab = wab.at[:, dirn * LANES:dirn * LANES + DN_HEADS].set(
            w_in[:, o_da + dirn * DN_HEADS:o_da + (dirn + 1) * DN_HEADS])
        wab = wab.at[:, dirn * LANES + DN_HEADS:dirn * LANES + 2 * DN_HEADS].set(
            w_in[:, o_db + dirn * DN_HEADS:o_db + (dirn + 1) * DN_HEADS])
    wab = wab.astype(BF16)
    wg = w_in[:, o_ga:o_gb + D_MODEL].astype(BF16)

    zeros8 = jnp.zeros((2, DN_HEADS), F32)
    ea = jnp.concatenate([jnp.exp(dn_a_log[0]), zeros8], axis=1)[:, None, :]
    bias = jnp.concatenate([dn_dt_bias[0], zeros8], axis=1)[:, None, :]


    wr = jnp.concatenate([moe_group_router[0], moe_expert_router[0],
                          jnp.zeros((D_MODEL, LANES - N_GROUPS - N_EXPERTS), F32)], axis=1)
    wr_hi = wr.astype(BF16)
    wr_lo = (wr - wr_hi.astype(F32)).astype(BF16)

    return dict(
        n1=norm1_w[0][None, :], watt=watt, wdn=wdn, wdz=wdz, wab=wab, wg=wg,
        bd_att=_block_ones(ATT_Q_W, HEAD_DIM), qnw=qnw, knw=knw,
        cw=cw, bd_dn=_block_ones(2 * LANES, DN_DIM), scale=scale, ea=ea, bias=bias,
        wa=w_branch_att[0].astype(BF16), wb=wb, wo=w_out[0].astype(BF16), bdo=_block_ones(DN_W, DN_DIM), onw=onw,
        n2=norm2_w[0][None, :], wr_hi=wr_hi, wr_lo=wr_lo,
        wge=moe_w_gate[0], wue=moe_w_up[0], wde=moe_w_down[0],
        fw=final_norm_w[None, :],
    )


def _rope_tables(seq):
    axis_dim = HEAD_DIM // 2
    inv = ROPE_THETA ** (-np.arange(0, axis_dim, 2, dtype=np.float32) / axis_dim)
    lane = np.arange(LANES)
    pair = lane % (HEAD_DIM // 2)
    freq = jnp.asarray(inv[pair % (axis_dim // 2)], F32)[None, :]
    use_row = jnp.asarray(pair < axis_dim // 2)[None, :]
    sign = jnp.asarray(np.where(lane % HEAD_DIM < HEAD_DIM // 2, -1.0, 1.0), F32)[None, :]
    rows = jnp.arange(seq // GRID_W, dtype=jnp.int32).astype(F32)[:, None] * freq
    cols = jnp.arange(GRID_W, dtype=jnp.int32).astype(F32)[:, None] * freq

    def expand(fn, scale):
        by_row = jnp.where(use_row, fn(rows) * scale, 0.0)[:, None, :]
        by_col = jnp.where(use_row, 0.0, fn(cols) * scale)[None, :, :]
        return (by_row + by_col).reshape(seq, LANES)

    return expand(jnp.cos, 1.0), expand(jnp.sin, sign)


def _trunk(x, p):
    batch, seq, _ = x.shape
    x2 = x.reshape(batch * seq, D_MODEL)
    cos_t, sin_t = _rope_tables(seq)
    qt, kn, vt, dnpre, dz, dab, sg, kmx_tiles = _inproj(
        x2, seq, p["n1"], p["watt"], p["wdn"], p["wdz"], p["wab"], p["wg"], p["bd_att"], p["qnw"], p["knw"],
        cos_t, sin_t)
    ot = _attention(qt, kn, vt, kmx_tiles.reshape(batch, -1, 8, ATT_KV_W), batch, seq)
    kq, vk = _dnprep(dnpre, seq, p["cw"], p["bd_dn"], p["scale"])
    o_f, o_b = _delta_rule(kq, vk, dab, p["ea"], p["bias"], batch, seq)
    x1, xp, logits = _merge(ot, o_f, o_b, dz, sg, x2, p["wa"], p["wb"], p["wo"], p["bdo"], p["onw"], p["n2"],
                            p["wr_hi"], p["wr_lo"])
    info, info_t, counts = _route(logits)
    y = _moe_final(xp, x1, info, info_t, counts, p["wge"], p["wue"], p["wde"], p["fw"])
    return y.reshape(batch, seq, D_MODEL)


def kernel(x_prompt, x_sample, norm1_w, w_in, att_q_norm, att_k_norm, dn_conv_w, dn_a_log, dn_dt_bias, dn_out_norm, w_branch_att, w_branch_dn, w_out, norm2_w, moe_group_router, moe_expert_router, moe_w_gate, moe_w_up, moe_w_down, final_norm_w):
    p = _prepare(norm1_w, w_in, att_q_norm, att_k_norm, dn_conv_w, dn_a_log, dn_dt_bias, dn_out_norm,
                 w_branch_att, w_branch_dn, w_out, norm2_w, moe_group_router, moe_expert_router,
                 moe_w_gate, moe_w_up, moe_w_down, final_norm_w)
    return (_trunk(x_prompt, p), _trunk(x_sample, p))
```

```python
import functools
import math

import numpy as np
import jax
import jax.numpy as jnp
from jax import lax
from jax.experimental import pallas as pl
from jax.experimental.pallas import tpu as pltpu
from jax.experimental.pallas import tpu_sc as plsc

F32 = jnp.float32
BF16 = jnp.bfloat16

D_MODEL = 1024
GRID_W = 64
EPS = 1e-6
ATT_HEADS = 8
ATT_KV_HEADS = 2
ATT_GROUP = ATT_HEADS // ATT_KV_HEADS
HEAD_DIM = 64
ROPE_THETA = 10000.0
DN_HEADS = 8
DN_DIM = 64
DN_CONV = 5
N_GROUPS = 4
EXPERTS_PER_GROUP = 8
N_EXPERTS = N_GROUPS * EXPERTS_PER_GROUP
EXPERT_FF = 256

ATT_Q_W = ATT_HEADS * HEAD_DIM
ATT_KV_W = ATT_KV_HEADS * HEAD_DIM
DN_W = DN_HEADS * DN_DIM
LANES = 128
DN_CHUNK = 128
DN_SUBCHUNKS = 4
ROUTER_OFF = 8
PACK_W = D_MODEL // 4
SC_WINDOW = 128
MOE_TILE = 512
ATT_TQ = 1024
ATT_TK = 1024
VMEM_LIMIT = 52 * 1024 * 1024
LOG2E = math.log2(math.e)
SCORE_BOUND_MAX = 50.0
SCORE_BOUND_SLACK = 1.05


def _mm(a, b):
    return jnp.dot(a.astype(BF16), b.astype(BF16), preferred_element_type=F32)


def _cparams(sem):
    return pltpu.CompilerParams(dimension_semantics=sem, vmem_limit_bytes=VMEM_LIMIT)


def _full(shape):
    nd = len(shape)
    return pl.BlockSpec(shape, lambda *_: (0,) * nd, pipeline_mode=pl.Buffered(1))


def _rope(x, cos, sin_signed):
    n = x.shape[1]
    lane = lax.broadcasted_iota(jnp.int32, x.shape, 1)
    first = (lane % HEAD_DIM) < (HEAD_DIM // 2)
    partner = jnp.where(first, pltpu.roll(x, n - HEAD_DIM // 2, 1), pltpu.roll(x, HEAD_DIM // 2, 1))
    return x * cos + partner * sin_signed


def _inproj_kernel(x_ref, n1_ref, watt_ref, wdn_ref, wdz_ref, wab_ref, wg_ref, bd_ref, qnw_ref, knw_ref,
                   cos_ref, sin_ref, qt_ref, k_ref, vt_ref, dn_ref, dz_ref, dab_ref, sg_ref, kmx_ref):
    tm = x_ref.shape[0]
    x = x_ref[...]
    ms = jnp.mean(x * x, axis=-1, keepdims=True)
    xn = (x * lax.rsqrt(ms + EPS) * n1_ref[...]).astype(BF16)

    att = jnp.dot(xn, watt_ref[...], preferred_element_type=F32)
    aq = att[:, :ATT_Q_W]
    ak = att[:, ATT_Q_W:ATT_Q_W + ATT_KV_W]
    av = att[:, ATT_Q_W + ATT_KV_W:]
    bd = bd_ref[...]
    qss = _mm(aq * aq, bd)
    kss = _mm(ak * ak, bd[:ATT_KV_W, :ATT_KV_W])
    cos = cos_ref[...]
    sin = sin_ref[...]
    cos4 = jnp.concatenate([cos] * (ATT_Q_W // LANES), axis=1)
    sin4 = jnp.concatenate([sin] * (ATT_Q_W // LANES), axis=1)
    q = aq * lax.rsqrt(qss * (1.0 / HEAD_DIM) + EPS) * qnw_ref[...]
    k = ak * lax.rsqrt(kss * (1.0 / HEAD_DIM) + EPS) * knw_ref[...]
    q = _rope(q, cos4, sin4) * (HEAD_DIM ** -0.5 * LOG2E)
    kmx_ref[0] = jnp.broadcast_to(jnp.max(_mm(k * k, bd[:ATT_KV_W, :ATT_KV_W]), axis=0, keepdims=True),
                                  (8, ATT_KV_W))
    k = _rope(k, cos, sin)
    qt_ref[...] = q.T.reshape(ATT_HEADS, HEAD_DIM, tm).astype(BF16)
    k_ref[0] = k[:, :HEAD_DIM].astype(BF16)
    k_ref[1] = k[:, HEAD_DIM:].astype(BF16)
    vt_ref[...] = av.T.reshape(ATT_KV_HEADS, HEAD_DIM, tm).astype(BF16)

    dn_ref[...] = jnp.dot(xn, wdn_ref[...], preferred_element_type=F32).astype(BF16)
    dz_ref[...] = jnp.dot(xn, wdz_ref[...], preferred_element_type=F32).astype(BF16)
    ab = jnp.dot(xn, wab_ref[...], preferred_element_type=F32)
    dab_ref[0] = ab[:, 0:2 * DN_HEADS]
    dab_ref[1] = ab[:, LANES:LANES + 2 * DN_HEADS]
    sg_ref[...] = jax.nn.sigmoid(jnp.dot(xn, wg_ref[...], preferred_element_type=F32)).astype(BF16)


def _inproj(x2, seq, n1, watt, wdn, wdz, wab, wg, bd, qnw, knw, cos_t, sin_t):
    m = x2.shape[0]
    tm = min(512, seq)
    per_seq = seq // tm
    row = lambda i: (i, 0)
    out_shape = (
        jax.ShapeDtypeStruct((ATT_HEADS, HEAD_DIM, m), BF16),
        jax.ShapeDtypeStruct((ATT_KV_HEADS, m, HEAD_DIM), BF16),
        jax.ShapeDtypeStruct((ATT_KV_HEADS, HEAD_DIM, m), BF16),
        jax.ShapeDtypeStruct((m, 3 * DN_W), BF16),
        jax.ShapeDtypeStruct((m, DN_W), BF16),
        jax.ShapeDtypeStruct((2, m, 2 * DN_HEADS), F32),
        jax.ShapeDtypeStruct((m, 2 * D_MODEL), BF16),
        jax.ShapeDtypeStruct((m // tm, 8, ATT_KV_W), F32),
    )
    in_specs = [
        pl.BlockSpec((tm, D_MODEL), row),
        _full(n1.shape), _full(watt.shape), _full(wdn.shape), _full(wdz.shape), _full(wab.shape),
        _full(wg.shape), _full(bd.shape), _full(qnw.shape), _full(knw.shape),
        pl.BlockSpec((tm, LANES), lambda i: (i % per_seq, 0)),
        pl.BlockSpec((tm, LANES), lambda i: (i % per_seq, 0)),
    ]
    out_specs = (
        pl.BlockSpec((ATT_HEADS, HEAD_DIM, tm), lambda i: (0, 0, i)),
        pl.BlockSpec((ATT_KV_HEADS, tm, HEAD_DIM), lambda i: (0, i, 0)),
        pl.BlockSpec((ATT_KV_HEADS, HEAD_DIM, tm), lambda i: (0, 0, i)),
        pl.BlockSpec((tm, 3 * DN_W), row),
        pl.BlockSpec((tm, DN_W), row),
        pl.BlockSpec((2, tm, 2 * DN_HEADS), lambda i: (0, i, 0)),
        pl.BlockSpec((tm, 2 * D_MODEL), row),
        pl.BlockSpec((1, 8, ATT_KV_W), lambda i: (i, 0, 0)),
    )
    return pl.pallas_call(
        _inproj_kernel, name="inproj", grid=(m // tm,), in_specs=in_specs, out_specs=out_specs, out_shape=out_shape,
        compiler_params=_cparams(("parallel",)),
    )(x2, n1, watt, wdn, wdz, wab, wg, bd, qnw, knw, cos_t, sin_t)


def _attn_kernel(qt_ref, k_ref, vt_ref, kmx_ref, ot_ref, p_ref, *, tk, nk):
    tq = qt_ref.shape[2]
    q = jnp.concatenate([qt_ref[r] for r in range(ATT_GROUP)], axis=1)
    n = q.shape[1]
    qf = q.astype(F32)
    kmx = jnp.max(kmx_ref[0], axis=0)
    head_lanes = lax.broadcasted_iota(jnp.int32, kmx.shape, 1) // HEAD_DIM == pl.program_id(1)
    kmx = jnp.max(jnp.where(head_lanes, kmx, 0.0), axis=1, keepdims=True)[0:1, :]
    bound = jnp.sqrt(jnp.sum(qf * qf, axis=0, keepdims=True) * kmx) * SCORE_BOUND_SLACK
    small = jnp.max(bound) <= SCORE_BOUND_MAX
    ones_rows = 16

    def blocks(kb):
        off = pl.multiple_of(kb * tk, tk)
        return k_ref[0, pl.ds(off, tk), :], vt_ref[0, :, pl.ds(off, tk)]

    def write(out):
        for r in range(ATT_GROUP):
            ot_ref[r] = out[:, r * tq:(r + 1) * tq].astype(BF16)

    @pl.when(small)
    def _():
        def probs(kb):
            s = jnp.dot(blocks(kb)[0], q, preferred_element_type=F32)
            return jnp.exp2(s - bound).astype(BF16)

        def weighted(kb, p):
            vaug = jnp.concatenate([blocks(kb)[1], jnp.ones((ones_rows, tk), BF16)], axis=0)
            return jnp.dot(vaug, p, preferred_element_type=F32)

        acc0 = jnp.zeros((HEAD_DIM + ones_rows, n), F32)
        p_ref[0] = probs(0)
        if nk == 1:
            acc = acc0 + weighted(0, p_ref[0])
        else:
            def pair(kb, acc, last):
                p_ref[1] = probs(kb + 1)
                acc = acc + weighted(kb, p_ref[0])
                if not last:
                    p_ref[0] = probs(kb + 2)
                return acc + weighted(kb + 1, p_ref[1])

            acc = lax.fori_loop(0, nk // 2 - 1, lambda j, a: pair(2 * j, a, False), acc0)
            acc = pair(nk - 2, acc, True)
        write(acc[0:HEAD_DIM] / acc[HEAD_DIM:HEAD_DIM + 1])

    @pl.when(jnp.logical_not(small))
    def _():
        def body(kb, carry):
            m_run, l_run, acc = carry
            kblk, vblk = blocks(kb)
            s = jnp.dot(kblk, q, preferred_element_type=F32)
            m_new = jnp.maximum(m_run, jnp.max(s, axis=0, keepdims=True))
            p = jnp.exp2(s - m_new)
            alpha = jnp.exp2(m_run - m_new)
            l_new = alpha * l_run + jnp.sum(p, axis=0, keepdims=True)
            acc_new = alpha * acc + jnp.dot(vblk, p.astype(BF16), preferred_element_type=F32)
            return m_new, l_new, acc_new

        init = (jnp.full((1, n), -jnp.inf, F32), jnp.zeros((1, n), F32), jnp.zeros((HEAD_DIM, n), F32))
        _, l_fin, acc = lax.fori_loop(0, nk, body, init)
        write(acc / l_fin)


def _attention(qt, kn, vt, kmx, batch, seq):
    m = qt.shape[2]
    tq = min(ATT_TQ, seq)
    tk = min(ATT_TK, seq)
    nq = seq // tq
    qspec = pl.BlockSpec((ATT_GROUP, HEAD_DIM, tq), lambda b, g, i: (g, 0, b * nq + i))
    return pl.pallas_call(
        functools.partial(_attn_kernel, tk=tk, nk=seq // tk), name="attention",
        grid=(batch, ATT_KV_HEADS, nq),
        in_specs=[
            qspec,
            pl.BlockSpec((1, seq, HEAD_DIM), lambda b, g, i: (g, b, 0)),
            pl.BlockSpec((1, HEAD_DIM, seq), lambda b, g, i: (g, 0, b)),
            pl.BlockSpec((1,) + kmx.shape[1:], lambda b, g, i: (b, 0, 0, 0)),
        ],
        out_specs=qspec,
        out_shape=jax.ShapeDtypeStruct((ATT_HEADS, HEAD_DIM, m), BF16),
        scratch_shapes=[pltpu.VMEM((2, tk, ATT_GROUP * tq), BF16)],
        compiler_params=_cparams(("parallel", "parallel", "parallel")),
    )(qt, kn, vt, kmx)


def _dnprep_kernel(cur_ref, prev_ref, next_ref, cw_ref, shift_ref, bd_ref, scale_ref, kq_ref, vk_ref,
                   buf_ref, *, seq):
    tm = cur_ref.shape[0]
    halo = prev_ref.shape[0]
    blk = shift_ref.shape[1] - 2 * halo
    i = pl.program_id(0)
    pos = (i * tm) % seq
    has_prev = pos != 0
    has_next = pos + tm != seq
    buf_ref[0:halo, :] = jnp.where(has_prev, prev_ref[...], jnp.zeros_like(prev_ref))
    buf_ref[halo:halo + tm, :] = cur_ref[...]
    buf_ref[halo + tm:, :] = jnp.where(has_next, next_ref[...], jnp.zeros_like(next_ref))
    cw = cw_ref[...]
    pad = DN_CONV // 2
    shift = shift_ref[...]
    rows = []
    for r in range(tm // blk):
        win = buf_ref[r * blk:(r + 1) * blk + 2 * halo, :]
        shifted = jnp.dot(shift, win, preferred_element_type=F32)
        acc = win[halo:halo + blk, :].astype(F32) * cw[pad:pad + 1, :]
        for n, j in enumerate(t for t in range(DN_CONV) if t != pad):
            acc = acc + shifted[n * blk:(n + 1) * blk, :] * cw[j:j + 1, :]
        rows.append(acc)
    y = jnp.concatenate(rows, axis=0) if len(rows) > 1 else rows[0]
    y = y * jax.nn.sigmoid(y)
    bd = bd_ref[...]
    w = bd.shape[0]
    qk = y[:, :2 * DN_W]
    qksq = qk * qk
    ss = jnp.concatenate([_mm(qksq[:, c * w:(c + 1) * w], bd) for c in range(2 * DN_W // w)], axis=1)
    qk = qk * lax.rsqrt(ss + EPS) * scale_ref[...]
    lo_half = lax.broadcasted_iota(jnp.int32, (tm, LANES), 1) < DN_DIM
    kq_cols, vk_cols = [], []
    for j in range(DN_HEADS // 2):
        sl = slice(j * LANES, (j + 1) * LANES)
        qb = qk[:, sl]
        kb = qk[:, DN_W + j * LANES:DN_W + (j + 1) * LANES]
        vb = y[:, 2 * DN_W + j * LANES:2 * DN_W + (j + 1) * LANES]
        qr, kr, vr = (pltpu.roll(t, DN_DIM, 1) for t in (qb, kb, vb))
        kq_cols += [jnp.where(lo_half, kb, qr), jnp.where(lo_half, kr, qb)]
        vk_cols += [jnp.where(lo_half, vb, kr), jnp.where(lo_half, vr, kb)]
    kq_ref[...] = jnp.concatenate(kq_cols, axis=1).astype(BF16)
    vk_ref[...] = jnp.concatenate(vk_cols, axis=1).astype(BF16)


def _dnprep(dnpre, seq, cw, bd, scale):
    m, width = dnpre.shape
    tm = min(512, seq)
    halo = 16
    hb = tm // halo
    last = m // halo - 1
    blk = min(128, tm)
    pad = DN_CONV // 2
    t = np.arange(blk)
    shift = np.zeros((DN_CONV - 1, blk, blk + 2 * halo), np.float32)
    for n, j in enumerate(x for x in range(DN_CONV) if x != pad):
        shift[n, t, halo + t + j - pad] = 1.0
    shift = jnp.asarray(shift.reshape((DN_CONV - 1) * blk, blk + 2 * halo), BF16)
    return pl.pallas_call(
        functools.partial(_dnprep_kernel, seq=seq), name="dnprep",
        grid=(m // tm,),
        in_specs=[
            pl.BlockSpec((tm, width), lambda i: (i, 0)),
            pl.BlockSpec((halo, width), lambda i: (jnp.maximum(i * hb - 1, 0), 0)),
            pl.BlockSpec((halo, width), lambda i: (jnp.minimum((i + 1) * hb, last), 0)),
            _full(cw.shape), _full(shift.shape), _full(bd.shape), _full(scale.shape),
        ],
        out_specs=(pl.BlockSpec((tm, 2 * DN_W), lambda i: (i, 0)),
                   pl.BlockSpec((tm, 2 * DN_W), lambda i: (i, 0))),
        out_shape=(jax.ShapeDtypeStruct((m, 2 * DN_W), BF16), jax.ShapeDtypeStruct((m, 2 * DN_W), BF16)),
        scratch_shapes=[pltpu.VMEM((tm + 2 * halo, width), BF16)],
        compiler_params=_cparams(("parallel",)),
    )(dnpre, dnpre, dnpre, cw, shift, bd, scale)


def _bd2(a, b):
    return jnp.concatenate([jnp.concatenate([a, jnp.zeros_like(b)], axis=1),
                            jnp.concatenate([jnp.zeros_like(a), b], axis=1)], axis=0)


def _softplus(x):
    return jnp.maximum(x, 0.0) + jnp.log1p(jnp.exp(-jnp.abs(x)))


def _delta_kernel(kqf_ref, kqb_ref, vkf_ref, vkb_ref, dabf_ref, dabb_ref, ea_ref, bias_ref, of_ref, ob_ref, s_ref):
    @pl.when(pl.program_id(1) == 0)
    def _():
        s_ref[...] = jnp.zeros_like(s_ref)

    for sub in range(DN_SUBCHUNKS):
        rf = pl.ds(sub * DN_CHUNK, DN_CHUNK)
        rb = pl.ds((DN_SUBCHUNKS - 1 - sub) * DN_CHUNK, DN_CHUNK)
        _delta_chunk(kqf_ref.at[rf], kqb_ref.at[rb], vkf_ref.at[rf], vkb_ref.at[rb], dabf_ref.at[:, rf],
                     dabb_ref.at[:, rb], ea_ref, bias_ref, of_ref.at[rf], ob_ref.at[rb], s_ref)


def _delta_chunk(kqf_ref, kqb_ref, vkf_ref, vkb_ref, dabf_ref, dabb_ref, ea_ref, bias_ref, of_ref, ob_ref, s_ref):
    C = DN_CHUNK
    H = DN_DIM
    row = lax.broadcasted_iota(jnp.int32, (C, C), 0)
    col = lax.broadcasted_iota(jnp.int32, (C, C), 1)
    incl = (row >= col, row <= col)
    strict = (row > col, row < col)
    rowp = lax.broadcasted_iota(jnp.int32, (C, 2 * C), 0)
    colp = lax.broadcasted_iota(jnp.int32, (C, 2 * C), 1) % C
    lane = lax.broadcasted_iota(jnp.int32, (C, LANES), 1)
    lo_half = lane < H
    lane_s = lax.broadcasted_iota(jnp.int32, (H, LANES), 1)
    eye_p =jnp.where(rowp == colp, 1.0, 0.0)
    blk2 = rowp // 2 == colp // 2
    levels = []
    b = 2
    while b < C:
        levels.append((rowp // (2 * b) == colp // (2 * b)) & (rowp // b != colp // b))
        b *= 2

    kq_refs = (kqf_ref, kqb_ref)
    vk_refs = (vkf_ref, vkb_ref)
    dab_refs = (dabf_ref, dabb_ref)
    o_refs = (of_ref, ob_ref)

    gc_all, gl_all, beta_all = [], [], []
    for d in range(2):
        ab = dab_refs[d][0]
        g = -ea_ref[d] * _softplus(ab + bias_ref[d])
        beta_all.append(jax.nn.sigmoid(ab))
        tri = jnp.where(incl[d], 1.0, 0.0).astype(BF16)
        g_hi = g.astype(BF16)
        g_lo = (g - g_hi.astype(F32)).astype(BF16)
        gc_all.append(jnp.dot(tri, g_hi, preferred_element_type=F32)
                      + jnp.dot(tri, g_lo, preferred_element_type=F32))
        gl_all.append(jnp.sum(g, axis=0, keepdims=True))

    units = [(d, j) for d in range(2) for j in range(DN_HEADS // 2)]

    def halves(x):
        return x[:, :x.shape[1] // 2], x[:, x.shape[1] // 2:]

    def rhs1(t):
        z = jnp.zeros_like(t)
        return jnp.concatenate([jnp.concatenate([t, z], axis=1), jnp.concatenate([z, t], axis=1)], axis=0)

    kq_p, kqf, vkf, gccol, gcrow, bcol, gl, kT, decay = ({} for _ in range(9))
    for u in units:
        d, j = u
        sl = slice(2 * j * LANES, (2 * j + 2) * LANES)
        kq_p[u] = kq_refs[d][:, sl]
        kqf[u] = kq_p[u].astype(F32)
        vkf[u] = vk_refs[d][:, sl].astype(F32)
        gccol[u], gcrow[u], bcol[u], gl[u], kT[u], decay[u] = [], [], [], [], [], []
        for hh in range(2):
            h = 2 * j + hh
            gccol[u].append(jnp.broadcast_to(gc_all[d][:, h:h + 1], (C, LANES)))
            bcol[u].append(jnp.broadcast_to(beta_all[d][:, DN_HEADS + h:DN_HEADS + h + 1], (C, LANES)))
            gl[u].append(gl_all[d][:, h:h + 1])
            gcrow[u].append(gccol[u][hh].T)
            kT[u].append(kqf[u][:, hh * LANES:(hh + 1) * LANES].T[0:H, :])
            decay[u].append(jnp.where(incl[d], jnp.exp(jnp.minimum(gccol[u][hh] - gcrow[u][hh], 0.0)), 0.0))

    p1 = {u: _mm(kq_p[u], _bd2(rhs1(kT[u][0]), rhs1(kT[u][1]))) for u in units}
    a_p, qkd, x_p = {}, {}, {}
    for u in units:
        d = u[0]
        a_h, qk_h = [], []
        for hh in range(2):
            kk = p1[u][:, (2 * hh) * C:(2 * hh + 1) * C]
            qk = p1[u][:, (2 * hh + 1) * C:(2 * hh + 2) * C]
            a_h.append(jnp.where(strict[d], kk * decay[u][hh], 0.0) * bcol[u][hh])
            qk_h.append(qk * decay[u][hh])
        a_p[u] = jnp.concatenate(a_h, axis=1)
        qkd[u] = jnp.concatenate(qk_h, axis=1)
        x_p[u] = eye_p - jnp.where(blk2, a_p[u], 0.0)

    for msk in levels:
        g_p = {u: _mm(jnp.where(msk, a_p[u], 0.0), _bd2(*halves(x_p[u]))) for u in units}
        x_p = {u: x_p[u] - _mm(x_p[u], _bd2(*halves(g_p[u]))) for u in units}

    uw = {}
    for u in units:
        rhs2 = [vkf[u][:, hh * LANES:(hh + 1) * LANES] * bcol[u][hh]
                * jnp.where(lo_half, 1.0, jnp.exp(gccol[u][hh])) for hh in range(2)]
        uw[u] = _mm(x_p[u], _bd2(rhs2[0], rhs2[1]))
    ol = {u: _mm(qkd[u], _bd2(*halves(uw[u]))) for u in units}
    nw = {}
    for u in units:
        kdT = jnp.concatenate([kT[u][hh] * jnp.exp(gl[u][hh] - gcrow[u][hh][0:H, :]) for hh in range(2)], axis=0)
        nw[u] = _mm(kdT, uw[u])
    for u in units:
        d, j = u
        sl = slice(2 * j * LANES, (2 * j + 2) * LANES)
        zmult = jnp.concatenate(
            [jnp.where(lo_half, jnp.exp(gl[u][hh] - gccol[u][hh]), jnp.exp(gccol[u][hh])) for hh in range(2)], axis=1)
        z_p = kqf[u] * zmult - ol[u]
        nw_h = (nw[u][0:H, 0:LANES], nw[u][H:2 * H, LANES:2 * LANES])
        s_old = [s_ref[d, 2 * j + hh] for hh in range(2)]
        sblk = [jnp.concatenate([jnp.zeros((H, LANES), F32), s_old[hh]], axis=0) for hh in range(2)]
        r = _mm(jnp.concatenate([_bd2(*nw_h), z_p], axis=0), _bd2(sblk[0], sblk[1]))
        o_pair = r[2 * H:, :] + ol[u]
        o_refs[d][:, j * LANES:(j + 1) * LANES] = jnp.where(
            lo_half, o_pair[:, :LANES], pltpu.roll(o_pair[:, LANES:], H, 1))
        wks = (r[0:H, 0:LANES], r[H:2 * H, LANES:2 * LANES])
        for hh in range(2):
            s_new = jnp.exp(gl[u][hh]) * s_old[hh] + nw_h[hh] - wks[hh]
            s_ref[d, 2 * j + hh] = jnp.where(lane_s < H, s_new, 0.0)


def _delta_rule(kq, vk, dab, ea, bias, batch, seq):
    m = kq.shape[0]
    C = DN_CHUNK * DN_SUBCHUNKS
    nc = seq // C
    width = kq.shape[1]
    fwd = lambda b, n: (b * nc + n, 0)
    bwd = lambda b, n: (b * nc + nc - 1 - n, 0)
    return pl.pallas_call(
        _delta_kernel, name="delta",
        grid=(batch, nc),
        in_specs=[
            pl.BlockSpec((C, width), fwd), pl.BlockSpec((C, width), bwd),
            pl.BlockSpec((C, width), fwd), pl.BlockSpec((C, width), bwd),
            pl.BlockSpec((1, C, 2 * DN_HEADS), lambda b, n: (0, b * nc + n, 0)),
            pl.BlockSpec((1, C, 2 * DN_HEADS), lambda b, n: (1, b * nc + nc - 1 - n, 0)),
            _full(ea.shape), _full(bias.shape),
        ],
        out_specs=(pl.BlockSpec((C, DN_W), fwd), pl.BlockSpec((C, DN_W), bwd)),
        out_shape=(jax.ShapeDtypeStruct((m, DN_W), F32), jax.ShapeDtypeStruct((m, DN_W), F32)),
        scratch_shapes=[pltpu.VMEM((2, DN_HEADS, DN_DIM, LANES), F32)],
        compiler_params=_cparams(("parallel", "arbitrary")),
    )(kq, kq, vk, vk, dab, dab, ea, bias)


def _merge_kernel(ot_ref, of_ref, ob_ref, dz_ref, sg_ref, x_ref, wa_ref, wb_ref, wo_ref, bdo_ref, onw_ref, n2_ref,
                  wr_hi_ref, wr_lo_ref, x1_ref, xp_ref, lg_ref):
    tm = x_ref.shape[0]
    ya = lax.dot_general(ot_ref[...].reshape(ATT_Q_W, tm), wa_ref[...], (((0,), (0,)), ((), ())),
                         preferred_element_type=F32)
    o = of_ref[...] + ob_ref[...]
    ss = _mm(o * o, bdo_ref[...])
    dz = dz_ref[...].astype(F32)
    dn = o * lax.rsqrt(ss * (1.0 / DN_DIM) + EPS) * onw_ref[...] * (dz * jax.nn.sigmoid(dz))
    yb = _mm(dn, wb_ref[...])
    sg = sg_ref[...].astype(F32)
    mix = sg[:, :D_MODEL] * ya + sg[:, D_MODEL:] * yb
    x1 = x_ref[...] + _mm(mix, wo_ref[...])
    x1_ref[...] = x1
    ms = jnp.mean(x1 * x1, axis=-1, keepdims=True)
    xn = x1 * lax.rsqrt(ms + EPS) * n2_ref[...]
    xn_hi = xn.astype(BF16)
    xn_lo = (xn - xn_hi.astype(F32)).astype(BF16)
    xp_ref[0], xp_ref[1] = _pack_rows(xn)
    whi = wr_hi_ref[...]
    lg_ref[...] = (jnp.dot(xn_hi, whi, preferred_element_type=F32)
                   + jnp.dot(xn_lo, whi, preferred_element_type=F32)
                   + jnp.dot(xn_hi, wr_lo_ref[...], preferred_element_type=F32))


def _merge(ot, o_f, o_b, dz, sg, x2, wa, wb, wo, bdo, onw, n2, wr_hi, wr_lo):
    m = x2.shape[0]
    tm = 256
    row = lambda i: (i, 0)
    return pl.pallas_call(
        _merge_kernel, name="merge", grid=(m // tm,),
        in_specs=[
            pl.BlockSpec((ATT_HEADS, HEAD_DIM, tm), lambda i: (0, 0, i)),
            pl.BlockSpec((tm, o_f.shape[1]), row),
            pl.BlockSpec((tm, o_b.shape[1]), row),
            pl.BlockSpec((tm, dz.shape[1]), row),
            pl.BlockSpec((tm, sg.shape[1]), row),
            pl.BlockSpec((tm, D_MODEL), row),
            _full(wa.shape), _full(wb.shape), _full(wo.shape), _full(bdo.shape), _full(onw.shape),
            _full(n2.shape), _full(wr_hi.shape), _full(wr_lo.shape),
        ],
        out_specs=(pl.BlockSpec((tm, D_MODEL), row), pl.BlockSpec((2, tm, PACK_W), lambda i: (0, i, 0)),
                   pl.BlockSpec((tm, LANES), row)),
        out_shape=(jax.ShapeDtypeStruct((m, D_MODEL), F32), jax.ShapeDtypeStruct((2, m, PACK_W), jnp.uint32),
                   jax.ShapeDtypeStruct((m, LANES), F32)),
        compiler_params=_cparams(("parallel",)),
    )(ot, o_f, o_b, dz, sg, x2, wa, wb, wo, bdo, onw, n2, wr_hi, wr_lo)


def _route_kernel(lg_ref, later_ref, info_ref, info_t_ref, cnt_ref):
    @pl.when(pl.program_id(0) == 0)
    def _():
        cnt_ref[...] = jnp.zeros_like(cnt_ref)

    tm = lg_ref.shape[0]
    lt = lg_ref[...].T
    neg = -jnp.inf

    def first_argmax(v):
        rows = lax.broadcasted_iota(jnp.int32, v.shape, 0).astype(F32)
        mx = jnp.max(v, axis=0, keepdims=True)
        idx = jnp.min(jnp.where(v == mx, rows, float(v.shape[0])), axis=0, keepdims=True)
        return mx, idx

    g = lt[0:N_GROUPS]
    gmax, gidx = first_argmax(g)
    gval = 1.0 / jnp.sum(jnp.exp(g - gmax), axis=0, keepdims=True)
    el = lt[ROUTER_OFF:ROUTER_OFF + EXPERTS_PER_GROUP]
    for gi in range(1, N_GROUPS):
        lo = ROUTER_OFF + gi * EXPERTS_PER_GROUP
        el = jnp.where(gidx == float(gi), lt[lo:lo + EXPERTS_PER_GROUP], el)
    erow = lax.broadcasted_iota(jnp.int32, el.shape, 0).astype(F32)
    m1, i1 = first_argmax(el)
    m2, i2 = first_argmax(jnp.where(erow == i1, neg, el))
    r = jnp.exp(m2 - m1)
    w1 = gval / (1.0 + r)
    w2 = gval * r / (1.0 + r)
    e1 = gidx * EXPERTS_PER_GROUP + i1
    e2 = gidx * EXPERTS_PER_GROUP + i2
    xrow = lax.broadcasted_iota(jnp.int32, (N_EXPERTS, tm), 0).astype(F32)
    chosen = jnp.where((xrow == e1) | (xrow == e2), 1.0, 0.0)
    before = _mm(chosen, later_ref[...]) + cnt_ref[:, 0:1]
    rank1 = jnp.sum(jnp.where(xrow == e1, before, 0.0), axis=0, keepdims=True)
    rank2 = jnp.sum(jnp.where(xrow == e2, before, 0.0), axis=0, keepdims=True)
    cnt_ref[...] = cnt_ref[...] + jnp.sum(chosen, axis=1, keepdims=True)
    r8 = lax.broadcasted_iota(jnp.int32, (8, tm), 0)
    info_t = jnp.zeros((8, tm), F32)
    for k, v in enumerate((e1, e2, w1, w2, rank1, rank2)):
        info_t = jnp.where(r8 == k, v, info_t)
    info_t_ref[...] = info_t
    info_ref[...] = jnp.concatenate([info_t, jnp.zeros((LANES - 8, tm), F32)], axis=0).T


def _route(logits):
    m = logits.shape[0]
    tm = 1024 if m % 1024 == 0 else 256
    spec = pl.BlockSpec((tm, LANES), lambda i: (i, 0))
    t = np.arange(tm)
    later = jnp.asarray(t[:, None] < t[None, :], BF16)
    return pl.pallas_call(
        _route_kernel, name="route", grid=(m // tm,), in_specs=[spec, _full(later.shape)],
        out_specs=(spec, pl.BlockSpec((8, tm), lambda i: (0, i)),
                   pl.BlockSpec((N_EXPERTS, LANES), lambda i: (0, 0))),
        out_shape=(jax.ShapeDtypeStruct((m, LANES), F32), jax.ShapeDtypeStruct((8, m), F32),
                   jax.ShapeDtypeStruct((N_EXPERTS, LANES), F32)),
        compiler_params=_cparams(("arbitrary",)),
    )(logits, later)


def _pack_rows(x):
    bits = pltpu.bitcast(x.astype(BF16).astype(F32), jnp.uint32)
    half = x.shape[1] // 2
    word = (bits[:, :half] >> 16) | (bits[:, half:] & jnp.uint32(0xFFFF0000))
    return word[:, :PACK_W], word[:, PACK_W:]


def _unpack_rows(w0, w1):
    lo = [pltpu.bitcast(w << 16, F32) for w in (w0, w1)]
    hi = [pltpu.bitcast(w & jnp.uint32(0xFFFF0000), F32) for w in (w0, w1)]
    return jnp.concatenate(lo + hi, axis=1)


def _sc_mesh():
    return plsc.VectorSubcoreMesh(core_axis_name="c", subcore_axis_name="s")


def _sc_scatter_rows(src, idx, n_out, reps):
    s, width = src.shape
    nblk = s // SC_WINDOW

    @pl.kernel(out_type=jax.ShapeDtypeStruct((n_out, width), src.dtype), mesh=_sc_mesh(), scratch_types=[])
    def scatter_kernel(x_hbm, i_hbm, o_hbm):
        def body(x_vmem, i_vmem):
            pltpu.sync_copy(x_vmem, o_hbm.at[i_vmem.at[0]])

        pltpu.emit_pipeline(
            body, grid=(reps * nblk,),
            in_specs=[pl.BlockSpec((SC_WINDOW, width), index_map=lambda i: (i % nblk, 0)),
                      pl.BlockSpec((1, SC_WINDOW), index_map=lambda i: (0, i))],
            out_specs=[], core_axis_name=("c", "s"), dimension_semantics=(pltpu.PARALLEL,),
        )(x_hbm, i_hbm)

    return scatter_kernel(src, idx.reshape(1, reps * s))


def _sc_gather_rows(table, idx):
    k = idx.shape[0]
    width = table.shape[1]

    @pl.kernel(out_type=jax.ShapeDtypeStruct((k, width), table.dtype), mesh=_sc_mesh())
    def gather_kernel(x_hbm, i_hbm, o_hbm):
        def body(i_vmem, o_vmem):
            pltpu.sync_copy(x_hbm.at[i_vmem.at[0]], o_vmem)

        pltpu.emit_pipeline(
            body, grid=(k // SC_WINDOW,),
            in_specs=[pl.BlockSpec((1, SC_WINDOW), index_map=lambda i: (0, i))],
            out_specs=[pl.BlockSpec((SC_WINDOW, width), index_map=lambda i: (i, 0))],
            core_axis_name=("c", "s"), dimension_semantics=(pltpu.PARALLEL,),
        )(i_hbm, o_hbm)

    return gather_kernel(table, idx.reshape(1, k))


def _routing_tables(info_t, counts):
    m = info_t.shape[1]
    e = info_t[0:2].astype(jnp.int32)
    rank = info_t[4:6].astype(jnp.int32)
    cnt = counts[:, 0].astype(jnp.int32)
    tiles_e = (cnt + MOE_TILE - 1) // MOE_TILE
    tile_end = jnp.cumsum(tiles_e)
    slot_start = (tile_end - tiles_e) * MOE_TILE
    expert_ids = jnp.arange(N_EXPERTS, dtype=jnp.int32)
    pos = jnp.sum(jnp.where(e[None] == expert_ids[:, None, None], slot_start[:, None, None], 0), axis=0) + rank
    n_tiles = 2 * m // MOE_TILE + N_EXPERTS
    tile_ids = jnp.arange(n_tiles, dtype=jnp.int32)
    tile_expert = jnp.sum(tile_ids[:, None] >= tile_end[None, :], axis=1)
    tile_expert = jnp.minimum(tile_expert, N_EXPERTS - 1).astype(jnp.int32)
    n_used = tile_end[-1:].astype(jnp.int32)
    has = tiles_e > 0
    run_index = jnp.cumsum(has.astype(jnp.int32)) - 1
    later = (expert_ids[None, :] > expert_ids[:, None]) & has[None, :]
    next_e = jnp.min(jnp.where(later, expert_ids[None, :], N_EXPERTS), axis=1)
    next_e = jnp.where(next_e < N_EXPERTS, next_e, -1)
    of_tile = (tile_expert[:, None] == expert_ids[None, :]).astype(jnp.int32)
    lookup = lambda table: jnp.sum(of_tile * table[None, :], axis=1)
    first = (tile_ids == lookup(tile_end - tiles_e)) & (tile_ids < n_used[0])
    sched = (tile_expert, n_used, lookup(next_e).astype(jnp.int32), first.astype(jnp.int32),
             (lookup(run_index) % 2).astype(jnp.int32))
    return pos, sched


def _experts_kernel(te_ref, nu_ref, nxt_ref, first_ref, slot_ref, xs_ref, wg_hbm, wu_hbm, wd_hbm, ys_ref,
                    wg_buf, wu_buf, wd_buf, sem):
    i = pl.program_id(0)

    def weight_copies(e, s):
        return (pltpu.make_async_copy(wg_hbm.at[e], wg_buf.at[s], sem.at[0, s]),
                pltpu.make_async_copy(wu_hbm.at[e], wu_buf.at[s], sem.at[1, s]),
                pltpu.make_async_copy(wd_hbm.at[e], wd_buf.at[s], sem.at[2, s]))

    @pl.when(i == 0)
    def _():
        for c in weight_copies(te_ref[0], slot_ref[0]):
            c.start()

    s = slot_ref[i]

    @pl.when(first_ref[i] == 1)
    def _():
        for c in weight_copies(te_ref[i], s):
            c.wait()

        @pl.when(nxt_ref[i] >= 0)
        def _():
            for c in weight_copies(nxt_ref[i], 1 - s):
                c.start()

    @pl.when(i < nu_ref[0])
    def _():
        x = _unpack_rows(xs_ref[0], xs_ref[1]).astype(BF16)
        hg = _mm(x, wg_buf[s])
        hu = _mm(x, wu_buf[s])
        y = _mm(hg * jax.nn.sigmoid(hg) * hu, wd_buf[s])
        ys_ref[0], ys_ref[1] = _pack_rows(y)


def _experts(xs, sched, wg, wu, wd):
    n_slots = xs.shape[1]
    slots = pl.BlockSpec((2, MOE_TILE, PACK_W), lambda i, *_: (0, i, 0))
    hbm = pl.BlockSpec(memory_space=pl.ANY)
    return pl.pallas_call(
        _experts_kernel, name="experts",
        grid_spec=pltpu.PrefetchScalarGridSpec(
            num_scalar_prefetch=len(sched), grid=(n_slots // MOE_TILE,),
            in_specs=[slots, hbm, hbm, hbm],
            out_specs=slots,
            scratch_shapes=[pltpu.VMEM((2,) + wg.shape[1:], wg.dtype), pltpu.VMEM((2,) + wu.shape[1:], wu.dtype),
                            pltpu.VMEM((2,) + wd.shape[1:], wd.dtype), pltpu.SemaphoreType.DMA((3, 2))],
        ),
        out_shape=jax.ShapeDtypeStruct(xs.shape, jnp.uint32),
        compiler_params=_cparams(("arbitrary",)),
    )(*sched, xs, wg, wu, wd)


def _combine_kernel(x1_ref, info_ref, yg_ref, fw_ref, y_ref):
    info = info_ref[...]
    x = (x1_ref[...] + info[:, 2:3] * _unpack_rows(yg_ref[0, 0], yg_ref[0, 1])
         + info[:, 3:4] * _unpack_rows(yg_ref[1, 0], yg_ref[1, 1]))
    ms = jnp.mean(x * x, axis=-1, keepdims=True)
    y_ref[...] = x * lax.rsqrt(ms + EPS) * fw_ref[...]


def _combine(x1, info, yg, fw):
    m = x1.shape[0]
    tm = 1024 if m % 1024 == 0 else 256
    row = lambda i: (i, 0)
    return pl.pallas_call(
        _combine_kernel, name="combine", grid=(m // tm,),
        in_specs=[pl.BlockSpec((tm, D_MODEL), row), pl.BlockSpec((tm, LANES), row),
                  pl.BlockSpec((2, 2, tm, PACK_W), lambda i: (0, 0, i, 0)), _full(fw.shape)],
        out_specs=pl.BlockSpec((tm, D_MODEL), row),
        out_shape=jax.ShapeDtypeStruct((m, D_MODEL), F32),
        compiler_params=_cparams(("parallel",)),
    )(x1, info, yg, fw)


def _moe_final(xp, x1, info, info_t, counts, wg, wu, wd, fw):
    m = x1.shape[0]
    pos, sched = _routing_tables(info_t, counts)
    n_slots = 2 * m + N_EXPERTS * MOE_TILE
    idx = jnp.concatenate([h * n_slots + pos[k] for k in range(2) for h in range(2)])
    xs = _sc_scatter_rows(xp.reshape(2 * m, PACK_W), idx, 2 * n_slots, 2)
    ys = _experts(xs.reshape(2, n_slots, PACK_W), sched, wg, wu, wd)
    yg = _sc_gather_rows(ys.reshape(2 * n_slots, PACK_W), idx)
    return _combine(x1, info, yg.reshape(2, 2, m, PACK_W), fw)


def _block_ones(n, blk):
    idx = np.arange(n)
    return jnp.asarray((idx[:, None] // blk == idx[None, :] // blk), dtype=BF16)


def _prepare(norm1_w, w_in, att_q_norm, att_k_norm, dn_conv_w, dn_a_log, dn_dt_bias, dn_out_norm,
             w_branch_att, w_branch_dn, w_out, norm2_w, moe_group_router, moe_expert_router,
             moe_w_gate, moe_w_up, moe_w_down, final_norm_w):
    w_in = w_in[0]
    o_q, o_k, o_v = 0, ATT_Q_W, ATT_Q_W + ATT_KV_W
    o_dq = o_v + ATT_KV_W
    o_dz = o_dq + 3 * DN_W
    o_da = o_dz + DN_W
    o_db = o_da + 2 * DN_HEADS
    o_ga = o_db + 2 * DN_HEADS
    o_gb = o_ga + D_MODEL

    def deinterleave(w, heads):
        lead = w.shape[:-1]
        w = w.reshape(lead + (heads, HEAD_DIM // 2, 2))
        return jnp.swapaxes(w, -1, -2).reshape(lead + (heads * HEAD_DIM,))

    watt = jnp.concatenate([deinterleave(w_in[:, o_q:o_k], ATT_HEADS), deinterleave(w_in[:, o_k:o_v], ATT_KV_HEADS),
                            w_in[:, o_v:o_dq]], axis=1).astype(BF16)
    qnw = jnp.tile(deinterleave(att_q_norm[0], 1), ATT_HEADS)[None, :]
    knw = jnp.tile(deinterleave(att_k_norm[0], 1), ATT_KV_HEADS)[None, :]

    wdn =w_in[:, o_dq:o_dz].astype(BF16)
    cw = jnp.concatenate([dn_conv_w[0], jnp.zeros((8 - DN_CONV, 3 * DN_W), F32)], axis=0)
    scale = jnp.asarray(np.concatenate([np.full(DN_W, DN_DIM ** -0.5), np.ones(DN_W)]), F32)[None, :]

    wdz = w_in[:, o_dz:o_dz + DN_W].astype(BF16)
    onw = jnp.tile(dn_out_norm[0], DN_HEADS)[None, :]
    wb = w_branch_dn[0].astype(BF16)

    wab = jnp.zeros((D_MODEL, 2 * LANES), F32)
    for dirn in range(2):
        wab = wab.at[:, dirn * LANES:dirn * LANES + DN_HEADS].set(
            w_in[:, o_da + dirn * DN_HEADS:o_da + (dirn + 1) * DN_HEADS])
        wab = wab.at[:, dirn * LANES + DN_HEADS:dirn * LANES + 2 * DN_HEADS].set(
            w_in[:, o_db + dirn * DN_HEADS:o_db + (dirn + 1) * DN_HEADS])
    wab = wab.astype(BF16)
    wg = w_in[:, o_ga:o_gb + D_MODEL].astype(BF16)

    zeros8 = jnp.zeros((2, DN_HEADS), F32)
    ea = jnp.concatenate([jnp.exp(dn_a_log[0]), zeros8], axis=1)[:, None, :]
    bias = jnp.concatenate([dn_dt_bias[0], zeros8], axis=1)[:, None, :]


    wr = jnp.concatenate([moe_group_router[0], jnp.zeros((D_MODEL, ROUTER_OFF - N_GROUPS), F32),
                          moe_expert_router[0], jnp.zeros((D_MODEL, LANES - ROUTER_OFF - N_EXPERTS), F32)], axis=1)
    wr_hi = wr.astype(BF16)
    wr_lo = (wr - wr_hi.astype(F32)).astype(BF16)

    return dict(
        n1=norm1_w[0][None, :], watt=watt, wdn=wdn, wdz=wdz, wab=wab, wg=wg,
        bd_att=_block_ones(ATT_Q_W, HEAD_DIM), qnw=qnw, knw=knw,
        cw=cw, bd_dn=_block_ones(2 * LANES, DN_DIM), scale=scale, ea=ea, bias=bias,
        wa=w_branch_att[0].astype(BF16), wb=wb, wo=w_out[0].astype(BF16), bdo=_block_ones(DN_W, DN_DIM), onw=onw,
        n2=norm2_w[0][None, :], wr_hi=wr_hi, wr_lo=wr_lo,
        wge=moe_w_gate[0], wue=moe_w_up[0], wde=moe_w_down[0],
        fw=final_norm_w[None, :],
    )


def _rope_tables(seq):
    axis_dim = HEAD_DIM // 2
    inv = ROPE_THETA ** (-np.arange(0, axis_dim, 2, dtype=np.float32) / axis_dim)
    lane = np.arange(LANES)
    pair = lane % (HEAD_DIM // 2)
    freq = jnp.asarray(inv[pair % (axis_dim // 2)], F32)[None, :]
    use_row = jnp.asarray(pair < axis_dim // 2)[None, :]
    sign = jnp.asarray(np.where(lane % HEAD_DIM < HEAD_DIM // 2, -1.0, 1.0), F32)[None, :]
    rows = jnp.arange(seq // GRID_W, dtype=jnp.int32).astype(F32)[:, None] * freq
    cols = jnp.arange(GRID_W, dtype=jnp.int32).astype(F32)[:, None] * freq

    def expand(fn, scale):
        by_row = jnp.where(use_row, fn(rows) * scale, 0.0)[:, None, :]
        by_col = jnp.where(use_row, 0.0, fn(cols) * scale)[None, :, :]
        return (by_row + by_col).reshape(seq, LANES)

    return expand(jnp.cos, 1.0), expand(jnp.sin, sign)


def _trunk(x, p):
    batch, seq, _ = x.shape
    x2 = x.reshape(batch * seq, D_MODEL)
    cos_t, sin_t = _rope_tables(seq)
    qt, kn, vt, dnpre, dz, dab, sg, kmx_tiles = _inproj(
        x2, seq, p["n1"], p["watt"], p["wdn"], p["wdz"], p["wab"], p["wg"], p["bd_att"], p["qnw"], p["knw"],
        cos_t, sin_t)
    ot = _attention(qt, kn, vt, kmx_tiles.reshape(batch, -1, 8, ATT_KV_W), batch, seq)
    kq, vk = _dnprep(dnpre, seq, p["cw"], p["bd_dn"], p["scale"])
    o_f, o_b = _delta_rule(kq, vk, dab, p["ea"], p["bias"], batch, seq)
    x1, xp, logits = _merge(ot, o_f, o_b, dz, sg, x2, p["wa"], p["wb"], p["wo"], p["bdo"], p["onw"], p["n2"],
                            p["wr_hi"], p["wr_lo"])
    info, info_t, counts = _route(logits)
    y = _moe_final(xp, x1, info, info_t, counts, p["wge"], p["wue"], p["wde"], p["fw"])
    return y.reshape(batch, seq, D_MODEL)


def kernel(x_prompt, x_sample, norm1_w, w_in, att_q_norm, att_k_norm, dn_conv_w, dn_a_log, dn_dt_bias, dn_out_norm, w_branch_att, w_branch_dn, w_out, norm2_w, moe_group_router, moe_expert_router, moe_w_gate, moe_w_up, moe_w_down, final_norm_w):
    p = _prepare(norm1_w, w_in, att_q_norm, att_k_norm, dn_conv_w, dn_a_log, dn_dt_bias, dn_out_norm,
                 w_branch_att, w_branch_dn, w_out, norm2_w, moe_group_router, moe_expert_router,
                 moe_w_gate, moe_w_up, moe_w_down, final_norm_w)
    return (_trunk(x_prompt, p), _trunk(x_sample, p))
```

```python
import functools
import math

import numpy as np
import jax
import jax.numpy as jnp
from jax import lax
from jax.experimental import pallas as pl
from jax.experimental.pallas import tpu as pltpu
from jax.experimental.pallas import tpu_sc as plsc

F32 = jnp.float32
BF16 = jnp.bfloat16

D_MODEL = 1024
GRID_W = 64
EPS = 1e-6
ATT_HEADS = 8
ATT_KV_HEADS = 2
ATT_GROUP = ATT_HEADS // ATT_KV_HEADS
HEAD_DIM = 64
ROPE_THETA = 10000.0
DN_HEADS = 8
DN_DIM = 64
DN_CONV = 5
N_GROUPS = 4
EXPERTS_PER_GROUP = 8
N_EXPERTS = N_GROUPS * EXPERTS_PER_GROUP
EXPERT_FF = 256

ATT_Q_W = ATT_HEADS * HEAD_DIM
ATT_KV_W = ATT_KV_HEADS * HEAD_DIM
DN_W = DN_HEADS * DN_DIM
LANES = 128
DN_CHUNK = 128
DN_SUBCHUNKS = 4
MERGE_SUBTILES = 2
ROUTER_OFF = 8
PACK_W = D_MODEL // 4
SC_WINDOW = 128
MOE_TILE = 512
ATT_TQ = 1024
ATT_TK = 1024
VMEM_LIMIT = 52 * 1024 * 1024
LOG2E = math.log2(math.e)
SCORE_BOUND_MAX = 50.0
SCORE_BOUND_SLACK = 1.05


def _mm(a, b):
    return jnp.dot(a.astype(BF16), b.astype(BF16), preferred_element_type=F32)


def _cparams(sem):
    return pltpu.CompilerParams(dimension_semantics=sem, vmem_limit_bytes=VMEM_LIMIT)


def _full(shape):
    nd = len(shape)
    return pl.BlockSpec(shape, lambda *_: (0,) * nd, pipeline_mode=pl.Buffered(1))


def _rope(x, cos, sin_signed):
    n = x.shape[1]
    lane = lax.broadcasted_iota(jnp.int32, x.shape, 1)
    first = (lane % HEAD_DIM) < (HEAD_DIM // 2)
    partner = jnp.where(first, pltpu.roll(x, n - HEAD_DIM // 2, 1), pltpu.roll(x, HEAD_DIM // 2, 1))
    return x * cos + partner * sin_signed


def _inproj_kernel(x_ref, n1_ref, watt_ref, wdn_ref, wdz_ref, wab_ref, wg_ref, bd_ref, qnw_ref, knw_ref,
                   cos_ref, sin_ref, qt_ref, k_ref, vt_ref, dn_ref, dz_ref, dab_ref, sg_ref, kmx_ref):
    tm = x_ref.shape[0]
    x = x_ref[...]
    ms = jnp.mean(x * x, axis=-1, keepdims=True)
    xn = (x * lax.rsqrt(ms + EPS) * n1_ref[...]).astype(BF16)

    att = jnp.dot(xn, watt_ref[...], preferred_element_type=F32)
    aq = att[:, :ATT_Q_W]
    ak = att[:, ATT_Q_W:ATT_Q_W + ATT_KV_W]
    av = att[:, ATT_Q_W + ATT_KV_W:]
    bd = bd_ref[...]
    qss = _mm(aq * aq, bd)
    kss = _mm(ak * ak, bd[:ATT_KV_W, :ATT_KV_W])
    cos = cos_ref[...]
    sin = sin_ref[...]
    cos4 = jnp.concatenate([cos] * (ATT_Q_W // LANES), axis=1)
    sin4 = jnp.concatenate([sin] * (ATT_Q_W // LANES), axis=1)
    q = aq * lax.rsqrt(qss * (1.0 / HEAD_DIM) + EPS) * qnw_ref[...]
    k = ak * lax.rsqrt(kss * (1.0 / HEAD_DIM) + EPS) * knw_ref[...]
    q = _rope(q, cos4, sin4) * (HEAD_DIM ** -0.5 * LOG2E)
    kmx_ref[0] = jnp.broadcast_to(jnp.max(_mm(k * k, bd[:ATT_KV_W, :ATT_KV_W]), axis=0, keepdims=True),
                                  (8, ATT_KV_W))
    k = _rope(k, cos, sin)
    qt_ref[...] = q.T.reshape(ATT_HEADS, HEAD_DIM, tm).astype(BF16)
    k_ref[0] = k[:, :HEAD_DIM].astype(BF16)
    k_ref[1] = k[:, HEAD_DIM:].astype(BF16)
    vt_ref[...] = av.T.reshape(ATT_KV_HEADS, HEAD_DIM, tm).astype(BF16)

    dn_ref[...] = jnp.dot(xn, wdn_ref[...], preferred_element_type=F32).astype(BF16)
    dz_ref[...] = jnp.dot(xn, wdz_ref[...], preferred_element_type=F32).astype(BF16)
    ab = jnp.dot(xn, wab_ref[...], preferred_element_type=F32)
    dab_ref[0] = ab[:, 0:2 * DN_HEADS]
    dab_ref[1] = ab[:, LANES:LANES + 2 * DN_HEADS]
    sg_ref[...] = jax.nn.sigmoid(jnp.dot(xn, wg_ref[...], preferred_element_type=F32)).astype(BF16)


def _inproj(x2, seq, n1, watt, wdn, wdz, wab, wg, bd, qnw, knw, cos_t, sin_t):
    m = x2.shape[0]
    tm = min(512, seq)
    per_seq = seq // tm
    row = lambda i: (i, 0)
    out_shape = (
        jax.ShapeDtypeStruct((ATT_HEADS, HEAD_DIM, m), BF16),
        jax.ShapeDtypeStruct((ATT_KV_HEADS, m, HEAD_DIM), BF16),
        jax.ShapeDtypeStruct((ATT_KV_HEADS, HEAD_DIM, m), BF16),
        jax.ShapeDtypeStruct((m, 3 * DN_W), BF16),
        jax.ShapeDtypeStruct((m, DN_W), BF16),
        jax.ShapeDtypeStruct((2, m, 2 * DN_HEADS), F32),
        jax.ShapeDtypeStruct((m, 2 * D_MODEL), BF16),
        jax.ShapeDtypeStruct((m // tm, 8, ATT_KV_W), F32),
    )
    in_specs = [
        pl.BlockSpec((tm, D_MODEL), row),
        _full(n1.shape), _full(watt.shape), _full(wdn.shape), _full(wdz.shape), _full(wab.shape),
        _full(wg.shape), _full(bd.shape), _full(qnw.shape), _full(knw.shape),
        pl.BlockSpec((tm, LANES), lambda i: (i % per_seq, 0)),
        pl.BlockSpec((tm, LANES), lambda i: (i % per_seq, 0)),
    ]
    out_specs = (
        pl.BlockSpec((ATT_HEADS, HEAD_DIM, tm), lambda i: (0, 0, i)),
        pl.BlockSpec((ATT_KV_HEADS, tm, HEAD_DIM), lambda i: (0, i, 0)),
        pl.BlockSpec((ATT_KV_HEADS, HEAD_DIM, tm), lambda i: (0, 0, i)),
        pl.BlockSpec((tm, 3 * DN_W), row),
        pl.BlockSpec((tm, DN_W), row),
        pl.BlockSpec((2, tm, 2 * DN_HEADS), lambda i: (0, i, 0)),
        pl.BlockSpec((tm, 2 * D_MODEL), row),
        pl.BlockSpec((1, 8, ATT_KV_W), lambda i: (i, 0, 0)),
    )
    return pl.pallas_call(
        _inproj_kernel, name="inproj", grid=(m // tm,), in_specs=in_specs, out_specs=out_specs, out_shape=out_shape,
        compiler_params=_cparams(("parallel",)),
    )(x2, n1, watt, wdn, wdz, wab, wg, bd, qnw, knw, cos_t, sin_t)


def _attn_kernel(qt_ref, k_ref, vt_ref, kmx_ref, ot_ref, p_ref, *, tk, nk):
    tq = qt_ref.shape[2]
    q = jnp.concatenate([qt_ref[r] for r in range(ATT_GROUP)], axis=1)
    n = q.shape[1]
    qf = q.astype(F32)
    kmx = jnp.max(kmx_ref[0], axis=0)
    head_lanes = lax.broadcasted_iota(jnp.int32, kmx.shape, 1) // HEAD_DIM == pl.program_id(1)
    kmx = jnp.max(jnp.where(head_lanes, kmx, 0.0), axis=1, keepdims=True)[0:1, :]
    bound = jnp.sqrt(jnp.sum(qf * qf, axis=0, keepdims=True) * kmx) * SCORE_BOUND_SLACK
    small = jnp.max(bound) <= SCORE_BOUND_MAX
    ones_rows = 16

    def blocks(kb):
        off = pl.multiple_of(kb * tk, tk)
        return k_ref[0, pl.ds(off, tk), :], vt_ref[0, :, pl.ds(off, tk)]

    def write(out):
        for r in range(ATT_GROUP):
            ot_ref[r] = out[:, r * tq:(r + 1) * tq].astype(BF16)

    @pl.when(small)
    def _():
        def probs(kb):
            s = jnp.dot(blocks(kb)[0], q, preferred_element_type=F32)
            return jnp.exp2(s - bound).astype(BF16)

        def weighted(kb, p):
            vaug = jnp.concatenate([blocks(kb)[1], jnp.ones((ones_rows, tk), BF16)], axis=0)
            return jnp.dot(vaug, p, preferred_element_type=F32)

        acc0 = jnp.zeros((HEAD_DIM + ones_rows, n), F32)
        p_ref[0] = probs(0)
        if nk == 1:
            acc = acc0 + weighted(0, p_ref[0])
        else:
            def pair(kb, acc, last):
                p_ref[1] = probs(kb + 1)
                acc = acc + weighted(kb, p_ref[0])
                if not last:
                    p_ref[0] = probs(kb + 2)
                return acc + weighted(kb + 1, p_ref[1])

            acc = lax.fori_loop(0, nk // 2 - 1, lambda j, a: pair(2 * j, a, False), acc0)
            acc = pair(nk - 2, acc, True)
        write(acc[0:HEAD_DIM] / acc[HEAD_DIM:HEAD_DIM + 1])

    @pl.when(jnp.logical_not(small))
    def _():
        def body(kb, carry):
            m_run, l_run, acc = carry
            kblk, vblk = blocks(kb)
            s = jnp.dot(kblk, q, preferred_element_type=F32)
            m_new = jnp.maximum(m_run, jnp.max(s, axis=0, keepdims=True))
            p = jnp.exp2(s - m_new)
            alpha = jnp.exp2(m_run - m_new)
            l_new = alpha * l_run + jnp.sum(p, axis=0, keepdims=True)
            acc_new = alpha * acc + jnp.dot(vblk, p.astype(BF16), preferred_element_type=F32)
            return m_new, l_new, acc_new

        init = (jnp.full((1, n), -jnp.inf, F32), jnp.zeros((1, n), F32), jnp.zeros((HEAD_DIM, n), F32))
        _, l_fin, acc = lax.fori_loop(0, nk, body, init)
        write(acc / l_fin)


def _attention(qt, kn, vt, kmx, batch, seq):
    m = qt.shape[2]
    tq = min(ATT_TQ, seq)
    tk = min(ATT_TK, seq)
    nq = seq // tq
    qspec = pl.BlockSpec((ATT_GROUP, HEAD_DIM, tq), lambda b, g, i: (g, 0, b * nq + i))
    return pl.pallas_call(
        functools.partial(_attn_kernel, tk=tk, nk=seq // tk), name="attention",
        grid=(batch, ATT_KV_HEADS, nq),
        in_specs=[
            qspec,
            pl.BlockSpec((1, seq, HEAD_DIM), lambda b, g, i: (g, b, 0)),
            pl.BlockSpec((1, HEAD_DIM, seq), lambda b, g, i: (g, 0, b)),
            pl.BlockSpec((1,) + kmx.shape[1:], lambda b, g, i: (b, 0, 0, 0)),
        ],
        out_specs=qspec,
        out_shape=jax.ShapeDtypeStruct((ATT_HEADS, HEAD_DIM, m), BF16),
        scratch_shapes=[pltpu.VMEM((2, tk, ATT_GROUP * tq), BF16)],
        compiler_params=_cparams(("parallel", "parallel", "parallel")),
    )(qt, kn, vt, kmx)


def _dnprep_kernel(cur_ref, prev_ref, next_ref, cw_ref, shift_ref, bd_ref, scale_ref, kq_ref, vk_ref,
                   buf_ref, *, seq):
    tm = cur_ref.shape[0]
    halo = prev_ref.shape[0]
    blk = shift_ref.shape[1] - 2 * halo
    i = pl.program_id(0)
    pos = (i * tm) % seq
    has_prev = pos != 0
    has_next = pos + tm != seq
    buf_ref[0:halo, :] = jnp.where(has_prev, prev_ref[...], jnp.zeros_like(prev_ref))
    buf_ref[halo:halo + tm, :] = cur_ref[...]
    buf_ref[halo + tm:, :] = jnp.where(has_next, next_ref[...], jnp.zeros_like(next_ref))
    cw = cw_ref[...]
    pad = DN_CONV // 2
    shift = shift_ref[...]
    rows = []
    for r in range(tm // blk):
        win = buf_ref[r * blk:(r + 1) * blk + 2 * halo, :]
        shifted = jnp.dot(shift, win, preferred_element_type=F32)
        acc = win[halo:halo + blk, :].astype(F32) * cw[pad:pad + 1, :]
        for n, j in enumerate(t for t in range(DN_CONV) if t != pad):
            acc = acc + shifted[n * blk:(n + 1) * blk, :] * cw[j:j + 1, :]
        rows.append(acc)
    y = jnp.concatenate(rows, axis=0) if len(rows) > 1 else rows[0]
    y = y * jax.nn.sigmoid(y)
    bd = bd_ref[...]
    w = bd.shape[0]
    qk = y[:, :2 * DN_W]
    qksq = qk * qk
    ss = jnp.concatenate([_mm(qksq[:, c * w:(c + 1) * w], bd) for c in range(2 * DN_W // w)], axis=1)
    qk = qk * lax.rsqrt(ss + EPS) * scale_ref[...]
    lo_half = lax.broadcasted_iota(jnp.int32, (tm, LANES), 1) < DN_DIM
    kq_cols, vk_cols = [], []
    for j in range(DN_HEADS // 2):
        sl = slice(j * LANES, (j + 1) * LANES)
        qb = qk[:, sl]
        kb = qk[:, DN_W + j * LANES:DN_W + (j + 1) * LANES]
        vb = y[:, 2 * DN_W + j * LANES:2 * DN_W + (j + 1) * LANES]
        qr, kr, vr = (pltpu.roll(t, DN_DIM, 1) for t in (qb, kb, vb))
        kq_cols += [jnp.where(lo_half, kb, qr), jnp.where(lo_half, kr, qb)]
        vk_cols += [jnp.where(lo_half, vb, kr), jnp.where(lo_half, vr, kb)]
    kq_ref[...] = jnp.concatenate(kq_cols, axis=1).astype(BF16)
    vk_ref[...] = jnp.concatenate(vk_cols, axis=1).astype(BF16)


def _dnprep(dnpre, seq, cw, bd, scale):
    m, width = dnpre.shape
    tm = min(512, seq)
    halo = 16
    hb = tm // halo
    last = m // halo - 1
    blk = min(128, tm)
    pad = DN_CONV // 2
    t = np.arange(blk)
    shift = np.zeros((DN_CONV - 1, blk, blk + 2 * halo), np.float32)
    for n, j in enumerate(x for x in range(DN_CONV) if x != pad):
        shift[n, t, halo + t + j - pad] = 1.0
    shift = jnp.asarray(shift.reshape((DN_CONV - 1) * blk, blk + 2 * halo), BF16)
    return pl.pallas_call(
        functools.partial(_dnprep_kernel, seq=seq), name="dnprep",
        grid=(m // tm,),
        in_specs=[
            pl.BlockSpec((tm, width), lambda i: (i, 0)),
            pl.BlockSpec((halo, width), lambda i: (jnp.maximum(i * hb - 1, 0), 0)),
            pl.BlockSpec((halo, width), lambda i: (jnp.minimum((i + 1) * hb, last), 0)),
            _full(cw.shape), _full(shift.shape), _full(bd.shape), _full(scale.shape),
        ],
        out_specs=(pl.BlockSpec((tm, 2 * DN_W), lambda i: (i, 0)),
                   pl.BlockSpec((tm, 2 * DN_W), lambda i: (i, 0))),
        out_shape=(jax.ShapeDtypeStruct((m, 2 * DN_W), BF16), jax.ShapeDtypeStruct((m, 2 * DN_W), BF16)),
        scratch_shapes=[pltpu.VMEM((tm + 2 * halo, width), BF16)],
        compiler_params=_cparams(("parallel",)),
    )(dnpre, dnpre, dnpre, cw, shift, bd, scale)


def _bd2(a, b):
    return jnp.concatenate([jnp.concatenate([a, jnp.zeros_like(b)], axis=1),
                            jnp.concatenate([jnp.zeros_like(a), b], axis=1)], axis=0)


def _softplus(x):
    return jnp.maximum(x, 0.0) + jnp.log1p(jnp.exp(-jnp.abs(x)))


def _delta_kernel(kqf_ref, kqb_ref, vkf_ref, vkb_ref, dabf_ref, dabb_ref, ea_ref, bias_ref, of_ref, ob_ref, s_ref):
    @pl.when(pl.program_id(1) == 0)
    def _():
        s_ref[...] = jnp.zeros_like(s_ref)

    for sub in range(DN_SUBCHUNKS):
        rf = pl.ds(sub * DN_CHUNK, DN_CHUNK)
        rb = pl.ds((DN_SUBCHUNKS - 1 - sub) * DN_CHUNK, DN_CHUNK)
        _delta_chunk(kqf_ref.at[rf], kqb_ref.at[rb], vkf_ref.at[rf], vkb_ref.at[rb], dabf_ref.at[:, rf],
                     dabb_ref.at[:, rb], ea_ref, bias_ref, of_ref.at[rf], ob_ref.at[rb], s_ref)


def _delta_chunk(kqf_ref, kqb_ref, vkf_ref, vkb_ref, dabf_ref, dabb_ref, ea_ref, bias_ref, of_ref, ob_ref, s_ref):
    C = DN_CHUNK
    H = DN_DIM
    row = lax.broadcasted_iota(jnp.int32, (C, C), 0)
    col = lax.broadcasted_iota(jnp.int32, (C, C), 1)
    incl = (row >= col, row <= col)
    strict = (row > col, row < col)
    rowp = lax.broadcasted_iota(jnp.int32, (C, 2 * C), 0)
    colp = lax.broadcasted_iota(jnp.int32, (C, 2 * C), 1) % C
    lane = lax.broadcasted_iota(jnp.int32, (C, LANES), 1)
    lo_half = lane < H
    lane_s = lax.broadcasted_iota(jnp.int32, (H, LANES), 1)
    eye_p =jnp.where(rowp == colp, 1.0, 0.0)
    blk2 = rowp // 2 == colp // 2
    levels = []
    b = 2
    while b < C:
        levels.append((rowp // (2 * b) == colp // (2 * b)) & (rowp // b != colp // b))
        b *= 2

    kq_refs = (kqf_ref, kqb_ref)
    vk_refs = (vkf_ref, vkb_ref)
    dab_refs = (dabf_ref, dabb_ref)
    o_refs = (of_ref, ob_ref)

    gc_all, gl_all, beta_all = [], [], []
    for d in range(2):
        ab = dab_refs[d][0]
        g = -ea_ref[d] * _softplus(ab + bias_ref[d])
        beta_all.append(jax.nn.sigmoid(ab))
        tri = jnp.where(incl[d], 1.0, 0.0).astype(BF16)
        g_hi = g.astype(BF16)
        g_lo = (g - g_hi.astype(F32)).astype(BF16)
        gc_all.append(jnp.dot(tri, g_hi, preferred_element_type=F32)
                      + jnp.dot(tri, g_lo, preferred_element_type=F32))
        gl_all.append(jnp.sum(g, axis=0, keepdims=True))

    units = [(d, j) for d in range(2) for j in range(DN_HEADS // 2)]

    def halves(x):
        return x[:, :x.shape[1] // 2], x[:, x.shape[1] // 2:]

    def rhs1(t):
        z = jnp.zeros_like(t)
        return jnp.concatenate([jnp.concatenate([t, z], axis=1), jnp.concatenate([z, t], axis=1)], axis=0)

    kq_p, kqf, vkf, gccol, gcrow, bcol, gl, kT, decay = ({} for _ in range(9))
    for u in units:
        d, j = u
        sl = slice(2 * j * LANES, (2 * j + 2) * LANES)
        kq_p[u] = kq_refs[d][:, sl]
        kqf[u] = kq_p[u].astype(F32)
        vkf[u] = vk_refs[d][:, sl].astype(F32)
        gccol[u], gcrow[u], bcol[u], gl[u], kT[u], decay[u] = [], [], [], [], [], []
        for hh in range(2):
            h = 2 * j + hh
            gccol[u].append(jnp.broadcast_to(gc_all[d][:, h:h + 1], (C, LANES)))
            bcol[u].append(jnp.broadcast_to(beta_all[d][:, DN_HEADS + h:DN_HEADS + h + 1], (C, LANES)))
            gl[u].append(gl_all[d][:, h:h + 1])
            gcrow[u].append(gccol[u][hh].T)
            kT[u].append(kqf[u][:, hh * LANES:(hh + 1) * LANES].T[0:H, :])
            decay[u].append(jnp.where(incl[d], jnp.exp(jnp.minimum(gccol[u][hh] - gcrow[u][hh], 0.0)), 0.0))

    p1 = {u: _mm(kq_p[u], _bd2(rhs1(kT[u][0]), rhs1(kT[u][1]))) for u in units}
    a_p, qkd, x_p = {}, {}, {}
    for u in units:
        d = u[0]
        a_h, qk_h = [], []
        for hh in range(2):
            kk = p1[u][:, (2 * hh) * C:(2 * hh + 1) * C]
            qk = p1[u][:, (2 * hh + 1) * C:(2 * hh + 2) * C]
            a_h.append(jnp.where(strict[d], kk * decay[u][hh], 0.0) * bcol[u][hh])
            qk_h.append(qk * decay[u][hh])
        a_p[u] = jnp.concatenate(a_h, axis=1)
        qkd[u] = jnp.concatenate(qk_h, axis=1)
        x_p[u] = eye_p - jnp.where(blk2, a_p[u], 0.0)

    for msk in levels:
        g_p = {u: _mm(jnp.where(msk, a_p[u], 0.0), _bd2(*halves(x_p[u]))) for u in units}
        x_p = {u: x_p[u] - _mm(x_p[u], _bd2(*halves(g_p[u]))) for u in units}

    uw = {}
    for u in units:
        rhs2 = [vkf[u][:, hh * LANES:(hh + 1) * LANES] * bcol[u][hh]
                * jnp.where(lo_half, 1.0, jnp.exp(gccol[u][hh])) for hh in range(2)]
        uw[u] = _mm(x_p[u], _bd2(rhs2[0], rhs2[1]))
    ol = {u: _mm(qkd[u], _bd2(*halves(uw[u]))) for u in units}
    nw = {}
    for u in units:
        kdT = jnp.concatenate([kT[u][hh] * jnp.exp(gl[u][hh] - gcrow[u][hh][0:H, :]) for hh in range(2)], axis=0)
        nw[u] = _mm(kdT, uw[u])
    for u in units:
        d, j = u
        sl = slice(2 * j * LANES, (2 * j + 2) * LANES)
        zmult = jnp.concatenate(
            [jnp.where(lo_half, jnp.exp(gl[u][hh] - gccol[u][hh]), jnp.exp(gccol[u][hh])) for hh in range(2)], axis=1)
        z_p = kqf[u] * zmult - ol[u]
        nw_h = (nw[u][0:H, 0:LANES], nw[u][H:2 * H, LANES:2 * LANES])
        s_old = [s_ref[d, 2 * j + hh] for hh in range(2)]
        sblk = [jnp.concatenate([jnp.zeros((H, LANES), F32), s_old[hh]], axis=0) for hh in range(2)]
        r = _mm(jnp.concatenate([_bd2(*nw_h), z_p], axis=0), _bd2(sblk[0], sblk[1]))
        o_pair = r[2 * H:, :] + ol[u]
        o_refs[d][:, j * LANES:(j + 1) * LANES] = jnp.where(
            lo_half, o_pair[:, :LANES], pltpu.roll(o_pair[:, LANES:], H, 1))
        wks = (r[0:H, 0:LANES], r[H:2 * H, LANES:2 * LANES])
        for hh in range(2):
            s_new = jnp.exp(gl[u][hh]) * s_old[hh] + nw_h[hh] - wks[hh]
            s_ref[d, 2 * j + hh] = jnp.where(lane_s < H, s_new, 0.0)


def _delta_rule(kq, vk, dab, ea, bias, batch, seq):
    m = kq.shape[0]
    C = DN_CHUNK * DN_SUBCHUNKS
    nc = seq // C
    width = kq.shape[1]
    fwd = lambda b, n: (b * nc + n, 0)
    bwd = lambda b, n: (b * nc + nc - 1 - n, 0)
    return pl.pallas_call(
        _delta_kernel, name="delta",
        grid=(batch, nc),
        in_specs=[
            pl.BlockSpec((C, width), fwd), pl.BlockSpec((C, width), bwd),
            pl.BlockSpec((C, width), fwd), pl.BlockSpec((C, width), bwd),
            pl.BlockSpec((1, C, 2 * DN_HEADS), lambda b, n: (0, b * nc + n, 0)),
            pl.BlockSpec((1, C, 2 * DN_HEADS), lambda b, n: (1, b * nc + nc - 1 - n, 0)),
            _full(ea.shape), _full(bias.shape),
        ],
        out_specs=(pl.BlockSpec((C, DN_W), fwd), pl.BlockSpec((C, DN_W), bwd)),
        out_shape=(jax.ShapeDtypeStruct((m, DN_W), F32), jax.ShapeDtypeStruct((m, DN_W), F32)),
        scratch_shapes=[pltpu.VMEM((2, DN_HEADS, DN_DIM, LANES), F32)],
        compiler_params=_cparams(("parallel", "arbitrary")),
    )(kq, kq, vk, vk, dab, dab, ea, bias)


def _merge_kernel(ot_ref, of_ref, ob_ref, dz_ref, sg_ref, x_ref, wa_ref, wb_ref, wo_ref, bdo_ref, onw_ref, n2_ref,
                  wr_hi_ref, wr_lo_ref, x1_ref, xp_ref, lg_ref):
    sub = x_ref.shape[0] // MERGE_SUBTILES
    for t in range(MERGE_SUBTILES):
        r = pl.ds(t * sub, sub)
        _merge_tile(ot_ref.at[:, :, r], of_ref.at[r], ob_ref.at[r], dz_ref.at[r], sg_ref.at[r], x_ref.at[r],
                    wa_ref, wb_ref, wo_ref, bdo_ref, onw_ref, n2_ref, wr_hi_ref, wr_lo_ref,
                    x1_ref.at[r], xp_ref.at[:, r], lg_ref.at[r])


def _merge_tile(ot_ref, of_ref, ob_ref, dz_ref, sg_ref, x_ref, wa_ref, wb_ref, wo_ref, bdo_ref, onw_ref, n2_ref,
                wr_hi_ref, wr_lo_ref, x1_ref, xp_ref, lg_ref):
    tm = x_ref.shape[0]
    ya = lax.dot_general(ot_ref[...].reshape(ATT_Q_W, tm), wa_ref[...], (((0,), (0,)), ((), ())),
                         preferred_element_type=F32)
    o = of_ref[...] + ob_ref[...]
    ss = _mm(o * o, bdo_ref[...])
    dz = dz_ref[...].astype(F32)
    dn = o * lax.rsqrt(ss * (1.0 / DN_DIM) + EPS) * onw_ref[...] * (dz * jax.nn.sigmoid(dz))
    yb = _mm(dn, wb_ref[...])
    sg = sg_ref[...].astype(F32)
    mix = sg[:, :D_MODEL] * ya + sg[:, D_MODEL:] * yb
    x1 = x_ref[...] + _mm(mix, wo_ref[...])
    x1_ref[...] = x1
    ms = jnp.mean(x1 * x1, axis=-1, keepdims=True)
    xn = x1 * lax.rsqrt(ms + EPS) * n2_ref[...]
    xn_hi = xn.astype(BF16)
    xn_lo = (xn - xn_hi.astype(F32)).astype(BF16)
    xp_ref[0], xp_ref[1] = _pack_rows(xn)
    whi = wr_hi_ref[...]
    lg_ref[...] = (jnp.dot(xn_hi, whi, preferred_element_type=F32)
                   + jnp.dot(xn_lo, whi, preferred_element_type=F32)
                   + jnp.dot(xn_hi, wr_lo_ref[...], preferred_element_type=F32))


def _merge(ot, o_f, o_b, dz, sg, x2, wa, wb, wo, bdo, onw, n2, wr_hi, wr_lo):
    m = x2.shape[0]
    tm = 256 * MERGE_SUBTILES
    row = lambda i: (i, 0)
    return pl.pallas_call(
        _merge_kernel, name="merge", grid=(m // tm,),
        in_specs=[
            pl.BlockSpec((ATT_HEADS, HEAD_DIM, tm), lambda i: (0, 0, i)),
            pl.BlockSpec((tm, o_f.shape[1]), row),
            pl.BlockSpec((tm, o_b.shape[1]), row),
            pl.BlockSpec((tm, dz.shape[1]), row),
            pl.BlockSpec((tm, sg.shape[1]), row),
            pl.BlockSpec((tm, D_MODEL), row),
            _full(wa.shape), _full(wb.shape), _full(wo.shape), _full(bdo.shape), _full(onw.shape),
            _full(n2.shape), _full(wr_hi.shape), _full(wr_lo.shape),
        ],
        out_specs=(pl.BlockSpec((tm, D_MODEL), row), pl.BlockSpec((2, tm, PACK_W), lambda i: (0, i, 0)),
                   pl.BlockSpec((tm, LANES), row)),
        out_shape=(jax.ShapeDtypeStruct((m, D_MODEL), F32), jax.ShapeDtypeStruct((2, m, PACK_W), jnp.uint32),
                   jax.ShapeDtypeStruct((m, LANES), F32)),
        compiler_params=_cparams(("parallel",)),
    )(ot, o_f, o_b, dz, sg, x2, wa, wb, wo, bdo, onw, n2, wr_hi, wr_lo)


def _route_kernel(lg_ref, later_ref, info_ref, info_t_ref, cnt_ref):
    @pl.when(pl.program_id(0) == 0)
    def _():
        cnt_ref[...] = jnp.zeros_like(cnt_ref)

    tm = lg_ref.shape[0]
    lt = lg_ref[...].T
    neg = -jnp.inf

    def first_argmax(v):
        rows = lax.broadcasted_iota(jnp.int32, v.shape, 0).astype(F32)
        mx = jnp.max(v, axis=0, keepdims=True)
        idx = jnp.min(jnp.where(v == mx, rows, float(v.shape[0])), axis=0, keepdims=True)
        return mx, idx

    g = lt[0:N_GROUPS]
    gmax, gidx = first_argmax(g)
    gval = 1.0 / jnp.sum(jnp.exp(g - gmax), axis=0, keepdims=True)
    el = lt[ROUTER_OFF:ROUTER_OFF + EXPERTS_PER_GROUP]
    for gi in range(1, N_GROUPS):
        lo = ROUTER_OFF + gi * EXPERTS_PER_GROUP
        el = jnp.where(gidx == float(gi), lt[lo:lo + EXPERTS_PER_GROUP], el)
    erow = lax.broadcasted_iota(jnp.int32, el.shape, 0).astype(F32)
    m1, i1 = first_argmax(el)
    m2, i2 = first_argmax(jnp.where(erow == i1, neg, el))
    r = jnp.exp(m2 - m1)
    w1 = gval / (1.0 + r)
    w2 = gval * r / (1.0 + r)
    e1 = gidx * EXPERTS_PER_GROUP + i1
    e2 = gidx * EXPERTS_PER_GROUP + i2
    xrow = lax.broadcasted_iota(jnp.int32, (N_EXPERTS, tm), 0).astype(F32)
    chosen = jnp.where((xrow == e1) | (xrow == e2), 1.0, 0.0)
    before = _mm(chosen, later_ref[...]) + cnt_ref[:, 0:1]
    rank1 = jnp.sum(jnp.where(xrow == e1, before, 0.0), axis=0, keepdims=True)
    rank2 = jnp.sum(jnp.where(xrow == e2, before, 0.0), axis=0, keepdims=True)
    cnt_ref[...] = cnt_ref[...] + jnp.sum(chosen, axis=1, keepdims=True)
    r8 = lax.broadcasted_iota(jnp.int32, (8, tm), 0)
    info_t = jnp.zeros((8, tm), F32)
    for k, v in enumerate((e1, e2, w1, w2, rank1, rank2)):
        info_t = jnp.where(r8 == k, v, info_t)
    info_t_ref[...] = info_t
    info_ref[...] = jnp.concatenate([info_t, jnp.zeros((LANES - 8, tm), F32)], axis=0).T


def _route(logits):
    m = logits.shape[0]
    tm = 1024 if m % 1024 == 0 else 256
    spec = pl.BlockSpec((tm, LANES), lambda i: (i, 0))
    t = np.arange(tm)
    later = jnp.asarray(t[:, None] < t[None, :], BF16)
    return pl.pallas_call(
        _route_kernel, name="route", grid=(m // tm,), in_specs=[spec, _full(later.shape)],
        out_specs=(spec, pl.BlockSpec((8, tm), lambda i: (0, i)),
                   pl.BlockSpec((N_EXPERTS, LANES), lambda i: (0, 0))),
        out_shape=(jax.ShapeDtypeStruct((m, LANES), F32), jax.ShapeDtypeStruct((8, m), F32),
                   jax.ShapeDtypeStruct((N_EXPERTS, LANES), F32)),
        compiler_params=_cparams(("arbitrary",)),
    )(logits, later)


def _pack_rows(x):
    bits = pltpu.bitcast(x.astype(BF16).astype(F32), jnp.uint32)
    half = x.shape[1] // 2
    word = (bits[:, :half] >> 16) | (bits[:, half:] & jnp.uint32(0xFFFF0000))
    return word[:, :PACK_W], word[:, PACK_W:]


def _unpack_rows(w0, w1):
    lo = [pltpu.bitcast(w << 16, F32) for w in (w0, w1)]
    hi = [pltpu.bitcast(w & jnp.uint32(0xFFFF0000), F32) for w in (w0, w1)]
    return jnp.concatenate(lo + hi, axis=1)


def _sc_mesh():
    return plsc.VectorSubcoreMesh(core_axis_name="c", subcore_axis_name="s")


def _sc_scatter_rows(src, idx, n_out, reps):
    s, width = src.shape
    nblk = s // SC_WINDOW

    @pl.kernel(out_type=jax.ShapeDtypeStruct((n_out, width), src.dtype), mesh=_sc_mesh(), scratch_types=[])
    def scatter_kernel(x_hbm, i_hbm, o_hbm):
        def body(x_vmem, i_vmem):
            pltpu.sync_copy(x_vmem, o_hbm.at[i_vmem.at[0]])

        pltpu.emit_pipeline(
            body, grid=(reps * nblk,),
            in_specs=[pl.BlockSpec((SC_WINDOW, width), index_map=lambda i: (i % nblk, 0)),
                      pl.BlockSpec((1, SC_WINDOW), index_map=lambda i: (0, i))],
            out_specs=[], core_axis_name=("c", "s"), dimension_semantics=(pltpu.PARALLEL,),
        )(x_hbm, i_hbm)

    return scatter_kernel(src, idx.reshape(1, reps * s))


def _sc_gather_rows(table, idx):
    k = idx.shape[0]
    width = table.shape[1]

    @pl.kernel(out_type=jax.ShapeDtypeStruct((k, width), table.dtype), mesh=_sc_mesh())
    def gather_kernel(x_hbm, i_hbm, o_hbm):
        def body(i_vmem, o_vmem):
            pltpu.sync_copy(x_hbm.at[i_vmem.at[0]], o_vmem)

        pltpu.emit_pipeline(
            body, grid=(k // SC_WINDOW,),
            in_specs=[pl.BlockSpec((1, SC_WINDOW), index_map=lambda i: (0, i))],
            out_specs=[pl.BlockSpec((SC_WINDOW, width), index_map=lambda i: (i, 0))],
            core_axis_name=("c", "s"), dimension_semantics=(pltpu.PARALLEL,),
        )(i_hbm, o_hbm)

    return gather_kernel(table, idx.reshape(1, k))


def _routing_tables(info_t, counts):
    m = info_t.shape[1]
    e = info_t[0:2].astype(jnp.int32)
    rank = info_t[4:6].astype(jnp.int32)
    cnt = counts[:, 0].astype(jnp.int32)
    tiles_e = (cnt + MOE_TILE - 1) // MOE_TILE
    tile_end = jnp.cumsum(tiles_e)
    slot_start = (tile_end - tiles_e) * MOE_TILE
    expert_ids = jnp.arange(N_EXPERTS, dtype=jnp.int32)
    pos = jnp.sum(jnp.where(e[None] == expert_ids[:, None, None], slot_start[:, None, None], 0), axis=0) + rank
    n_tiles = 2 * m // MOE_TILE + N_EXPERTS
    tile_ids = jnp.arange(n_tiles, dtype=jnp.int32)
    tile_expert = jnp.sum(tile_ids[:, None] >= tile_end[None, :], axis=1)
    tile_expert = jnp.minimum(tile_expert, N_EXPERTS - 1).astype(jnp.int32)
    n_used = tile_end[-1:].astype(jnp.int32)
    has = tiles_e > 0
    run_index = jnp.cumsum(has.astype(jnp.int32)) - 1
    later = (expert_ids[None, :] > expert_ids[:, None]) & has[None, :]
    next_e = jnp.min(jnp.where(later, expert_ids[None, :], N_EXPERTS), axis=1)
    next_e = jnp.where(next_e < N_EXPERTS, next_e, -1)
    of_tile = (tile_expert[:, None] == expert_ids[None, :]).astype(jnp.int32)
    lookup = lambda table: jnp.sum(of_tile * table[None, :], axis=1)
    first = (tile_ids == lookup(tile_end - tiles_e)) & (tile_ids < n_used[0])
    sched = (tile_expert, n_used, lookup(next_e).astype(jnp.int32), first.astype(jnp.int32),
             (lookup(run_index) % 2).astype(jnp.int32))
    return pos, sched


def _experts_kernel(te_ref, nu_ref, nxt_ref, first_ref, slot_ref, xs_ref, wg_hbm, wu_hbm, wd_hbm, ys_ref,
                    wg_buf, wu_buf, wd_buf, sem):
    i = pl.program_id(0)

    def weight_copies(e, s):
        return (pltpu.make_async_copy(wg_hbm.at[e], wg_buf.at[s], sem.at[0, s]),
                pltpu.make_async_copy(wu_hbm.at[e], wu_buf.at[s], sem.at[1, s]),
                pltpu.make_async_copy(wd_hbm.at[e], wd_buf.at[s], sem.at[2, s]))

    @pl.when(i == 0)
    def _():
        for c in weight_copies(te_ref[0], slot_ref[0]):
            c.start()

    s = slot_ref[i]

    @pl.when(first_ref[i] == 1)
    def _():
        for c in weight_copies(te_ref[i], s):
            c.wait()

        @pl.when(nxt_ref[i] >= 0)
        def _():
            for c in weight_copies(nxt_ref[i], 1 - s):
                c.start()

    @pl.when(i < nu_ref[0])
    def _():
        x = _unpack_rows(xs_ref[0], xs_ref[1]).astype(BF16)
        hg = _mm(x, wg_buf[s])
        hu = _mm(x, wu_buf[s])
        y = _mm(hg * jax.nn.sigmoid(hg) * hu, wd_buf[s])
        ys_ref[0], ys_ref[1] = _pack_rows(y)


def _experts(xs, sched, wg, wu, wd):
    n_slots = xs.shape[1]
    slots = pl.BlockSpec((2, MOE_TILE, PACK_W), lambda i, *_: (0, i, 0))
    hbm = pl.BlockSpec(memory_space=pl.ANY)
    return pl.pallas_call(
        _experts_kernel, name="experts",
        grid_spec=pltpu.PrefetchScalarGridSpec(
            num_scalar_prefetch=len(sched), grid=(n_slots // MOE_TILE,),
            in_specs=[slots, hbm, hbm, hbm],
            out_specs=slots,
            scratch_shapes=[pltpu.VMEM((2,) + wg.shape[1:], wg.dtype), pltpu.VMEM((2,) + wu.shape[1:], wu.dtype),
                            pltpu.VMEM((2,) + wd.shape[1:], wd.dtype), pltpu.SemaphoreType.DMA((3, 2))],
        ),
        out_shape=jax.ShapeDtypeStruct(xs.shape, jnp.uint32),
        compiler_params=_cparams(("arbitrary",)),
    )(*sched, xs, wg, wu, wd)


def _combine_kernel(x1_ref, info_ref, yg_ref, fw_ref, y_ref):
    info = info_ref[...]
    x = (x1_ref[...] + info[:, 2:3] * _unpack_rows(yg_ref[0, 0], yg_ref[0, 1])
         + info[:, 3:4] * _unpack_rows(yg_ref[1, 0], yg_ref[1, 1]))
    ms = jnp.mean(x * x, axis=-1, keepdims=True)
    y_ref[...] = x * lax.rsqrt(ms + EPS) * fw_ref[...]


def _combine(x1, info, yg, fw):
    m = x1.shape[0]
    tm = 1024 if m % 1024 == 0 else 256
    row = lambda i: (i, 0)
    return pl.pallas_call(
        _combine_kernel, name="combine", grid=(m // tm,),
        in_specs=[pl.BlockSpec((tm, D_MODEL), row), pl.BlockSpec((tm, LANES), row),
                  pl.BlockSpec((2, 2, tm, PACK_W), lambda i: (0, 0, i, 0)), _full(fw.shape)],
        out_specs=pl.BlockSpec((tm, D_MODEL), row),
        out_shape=jax.ShapeDtypeStruct((m, D_MODEL), F32),
        compiler_params=_cparams(("parallel",)),
    )(x1, info, yg, fw)


def _moe_final(xp, x1, info, info_t, counts, wg, wu, wd, fw):
    m = x1.shape[0]
    pos, sched = _routing_tables(info_t, counts)
    n_slots = 2 * m + N_EXPERTS * MOE_TILE
    idx = jnp.concatenate([h * n_slots + pos[k] for k in range(2) for h in range(2)])
    xs = _sc_scatter_rows(xp.reshape(2 * m, PACK_W), idx, 2 * n_slots, 2)
    ys = _experts(xs.reshape(2, n_slots, PACK_W), sched, wg, wu, wd)
    yg = _sc_gather_rows(ys.reshape(2 * n_slots, PACK_W), idx)
    return _combine(x1, info, yg.reshape(2, 2, m, PACK_W), fw)


def _block_ones(n, blk):
    idx = np.arange(n)
    return jnp.asarray((idx[:, None] // blk == idx[None, :] // blk), dtype=BF16)


def _prepare(norm1_w, w_in, att_q_norm, att_k_norm, dn_conv_w, dn_a_log, dn_dt_bias, dn_out_norm,
             w_branch_att, w_branch_dn, w_out, norm2_w, moe_group_router, moe_expert_router,
             moe_w_gate, moe_w_up, moe_w_down, final_norm_w):
    w_in = w_in[0]
    o_q, o_k, o_v = 0, ATT_Q_W, ATT_Q_W + ATT_KV_W
    o_dq = o_v + ATT_KV_W
    o_dz = o_dq + 3 * DN_W
    o_da = o_dz + DN_W
    o_db = o_da + 2 * DN_HEADS
    o_ga = o_db + 2 * DN_HEADS
    o_gb = o_ga + D_MODEL

    def deinterleave(w, heads):
        lead = w.shape[:-1]
        w = w.reshape(lead + (heads, HEAD_DIM // 2, 2))
        return jnp.swapaxes(w, -1, -2).reshape(lead + (heads * HEAD_DIM,))

    watt = jnp.concatenate([deinterleave(w_in[:, o_q:o_k], ATT_HEADS), deinterleave(w_in[:, o_k:o_v], ATT_KV_HEADS),
                            w_in[:, o_v:o_dq]], axis=1).astype(BF16)
    qnw = jnp.tile(deinterleave(att_q_norm[0], 1), ATT_HEADS)[None, :]
    knw = jnp.tile(deinterleave(att_k_norm[0], 1), ATT_KV_HEADS)[None, :]

    wdn =w_in[:, o_dq:o_dz].astype(BF16)
    cw = jnp.concatenate([dn_conv_w[0], jnp.zeros((8 - DN_CONV, 3 * DN_W), F32)], axis=0)
    scale = jnp.asarray(np.concatenate([np.full(DN_W, DN_DIM ** -0.5), np.ones(DN_W)]), F32)[None, :]

    wdz = w_in[:, o_dz:o_dz + DN_W].astype(BF16)
    onw = jnp.tile(dn_out_norm[0], DN_HEADS)[None, :]
    wb = w_branch_dn[0].astype(BF16)

    wab = jnp.zeros((D_MODEL, 2 * LANES), F32)
    for dirn in range(2):
        wab = wab.at[:, dirn * LANES:dirn * LANES + DN_HEADS].set(
            w_in[:, o_da + dirn * DN_HEADS:o_da + (dirn + 1) * DN_HEADS])
        wab = wab.at[:, dirn * LANES + DN_HEADS:dirn * LANES + 2 * DN_HEADS].set(
            w_in[:, o_db + dirn * DN_HEADS:o_db + (dirn + 1) * DN_HEADS])
    wab = wab.astype(BF16)
    wg = w_in[:, o_ga:o_gb + D_MODEL].astype(BF16)

    zeros8 = jnp.zeros((2, DN_HEADS), F32)
    ea = jnp.concatenate([jnp.exp(dn_a_log[0]), zeros8], axis=1)[:, None, :]
    bias = jnp.concatenate([dn_dt_bias[0], zeros8], axis=1)[:, None, :]


    wr = jnp.concatenate([moe_group_router[0], jnp.zeros((D_MODEL, ROUTER_OFF - N_GROUPS), F32),
                          moe_expert_router[0], jnp.zeros((D_MODEL, LANES - ROUTER_OFF - N_EXPERTS), F32)], axis=1)
    wr_hi = wr.astype(BF16)
    wr_lo = (wr - wr_hi.astype(F32)).astype(BF16)

    return dict(
        n1=norm1_w[0][None, :], watt=watt, wdn=wdn, wdz=wdz, wab=wab, wg=wg,
        bd_att=_block_ones(ATT_Q_W, HEAD_DIM), qnw=qnw, knw=knw,
        cw=cw, bd_dn=_block_ones(2 * LANES, DN_DIM), scale=scale, ea=ea, bias=bias,
        wa=w_branch_att[0].astype(BF16), wb=wb, wo=w_out[0].astype(BF16), bdo=_block_ones(DN_W, DN_DIM), onw=onw,
        n2=norm2_w[0][None, :], wr_hi=wr_hi, wr_lo=wr_lo,
        wge=moe_w_gate[0], wue=moe_w_up[0], wde=moe_w_down[0],
        fw=final_norm_w[None, :],
    )


def _rope_tables(seq):
    axis_dim = HEAD_DIM // 2
    inv = ROPE_THETA ** (-np.arange(0, axis_dim, 2, dtype=np.float32) / axis_dim)
    lane = np.arange(LANES)
    pair = lane % (HEAD_DIM // 2)
    freq = jnp.asarray(inv[pair % (axis_dim // 2)], F32)[None, :]
    use_row = jnp.asarray(pair < axis_dim // 2)[None, :]
    sign = jnp.asarray(np.where(lane % HEAD_DIM < HEAD_DIM // 2, -1.0, 1.0), F32)[None, :]
    rows = jnp.arange(seq // GRID_W, dtype=jnp.int32).astype(F32)[:, None] * freq
    cols = jnp.arange(GRID_W, dtype=jnp.int32).astype(F32)[:, None] * freq

    def expand(fn, scale):
        by_row = jnp.where(use_row, fn(rows) * scale, 0.0)[:, None, :]
        by_col = jnp.where(use_row, 0.0, fn(cols) * scale)[None, :, :]
        return (by_row + by_col).reshape(seq, LANES)

    return expand(jnp.cos, 1.0), expand(jnp.sin, sign)


def _trunk(x, p):
    batch, seq, _ = x.shape
    x2 = x.reshape(batch * seq, D_MODEL)
    cos_t, sin_t = _rope_tables(seq)
    qt, kn, vt, dnpre, dz, dab, sg, kmx_tiles = _inproj(
        x2, seq, p["n1"], p["watt"], p["wdn"], p["wdz"], p["wab"], p["wg"], p["bd_att"], p["qnw"], p["knw"],
        cos_t, sin_t)
    ot = _attention(qt, kn, vt, kmx_tiles.reshape(batch, -1, 8, ATT_KV_W), batch, seq)
    kq, vk = _dnprep(dnpre, seq, p["cw"], p["bd_dn"], p["scale"])
    o_f, o_b = _delta_rule(kq, vk, dab, p["ea"], p["bias"], batch, seq)
    x1, xp, logits = _merge(ot, o_f, o_b, dz, sg, x2, p["wa"], p["wb"], p["wo"], p["bdo"], p["onw"], p["n2"],
                            p["wr_hi"], p["wr_lo"])
    info, info_t, counts = _route(logits)
    y = _moe_final(xp, x1, info, info_t, counts, p["wge"], p["wue"], p["wde"], p["fw"])
    return y.reshape(batch, seq, D_MODEL)


def kernel(x_prompt, x_sample, norm1_w, w_in, att_q_norm, att_k_norm, dn_conv_w, dn_a_log, dn_dt_bias, dn_out_norm, w_branch_att, w_branch_dn, w_out, norm2_w, moe_group_router, moe_expert_router, moe_w_gate, moe_w_up, moe_w_down, final_norm_w):
    p = _prepare(norm1_w, w_in, att_q_norm, att_k_norm, dn_conv_w, dn_a_log, dn_dt_bias, dn_out_norm,
                 w_branch_att, w_branch_dn, w_out, norm2_w, moe_group_router, moe_expert_router,
                 moe_w_gate, moe_w_up, moe_w_down, final_norm_w)
    return (_trunk(x_prompt, p), _trunk(x_sample, p))
```
